```python
import jax, jax.numpy as jnp
from jax import lax
import numpy as np

D_MODEL = 1024
BATCH = 4
SEQ = 4096
DEPTH = 4

N_MIXERS = 4
N_HEADS = 16
HEAD_DIM = D_MODEL // N_HEADS
BLOCK_Q = 128
CHUNK = 128
GMLP_GROUPS = 8
GMLP_DIM = D_MODEL
LORA_W = 64
LORA_A = 64
LORA_G = 128
N_EXPERTS = 16
N_GROUPS = 4
EXPERTS_PER_GROUP = N_EXPERTS // N_GROUPS
TOP_K = 2
D_EXPERT = 512
ALPHA = (2 * DEPTH) ** 0.25
BETA = (8 * DEPTH) ** -0.25
LN_EPS = 1e-5
GN_EPS = 64e-5

kernel_name = "hybrid_fox_gmlp_stickbreak_rwkv7_grouped_moe"


def _layer_norm(x, g, b, eps=LN_EPS):
    xf = x.astype(jnp.float32)
    mu = jnp.mean(xf, axis=-1, keepdims=True)
    var = jnp.mean(jnp.square(xf - mu), axis=-1, keepdims=True)
    return ((xf - mu) * lax.rsqrt(var + eps)).astype(x.dtype) * g + b


def _split_heads(t):
    B, S, _ = t.shape
    return t.reshape(B, S, N_HEADS, HEAD_DIM)


def _query_blocks(t):
    B, S, H, Dh = t.shape
    return t.reshape(B, S // BLOCK_Q, BLOCK_Q, H, Dh).transpose(1, 0, 2, 3, 4)


def _merge_blocks(o):
    nb, B, BQ, H, Dh = o.shape
    return o.transpose(1, 0, 2, 3, 4).reshape(B, nb * BQ, H * Dh)


def forgetting_attention(h, w_in, b_f, w_out):
    B, S, _ = h.shape
    proj = h @ w_in
    q, k, v, f_logit, o_logit = jnp.split(
        proj, [D_MODEL, 2 * D_MODEL, 3 * D_MODEL, 3 * D_MODEL + N_HEADS], axis=-1)
    q, k, v = _split_heads(q), _split_heads(k), _split_heads(v)
    log_f = jax.nn.log_sigmoid((f_logit + b_f).astype(jnp.float32))
    c = jnp.cumsum(log_f, axis=1).transpose(0, 2, 1)
    c_blocks = c.reshape(B, N_HEADS, S // BLOCK_Q, BLOCK_Q).transpose(2, 0, 1, 3)
    key_pos = jnp.arange(S)
    scale = HEAD_DIM ** -0.5

    def block(args):
        i, q_i, c_i = args
        q_pos = i * BLOCK_Q + jnp.arange(BLOCK_Q)
        logits = (jnp.einsum('bqhd,bkhd->bhqk', q_i, k).astype(jnp.float32) * scale
                  + c_i[..., :, None] - c[:, :, None, :])
        logits = jnp.where(key_pos[None, :] <= q_pos[:, None], logits, -jnp.inf)
        p = jax.nn.softmax(logits, axis=-1).astype(v.dtype)
        return jnp.einsum('bhqk,bkhd->bqhd', p, v)

    o = lax.map(block, (jnp.arange(S // BLOCK_Q), _query_blocks(q), c_blocks))
    o = _merge_blocks(o) * jax.nn.sigmoid(o_logit)
    return o @ w_out


def chunked_spatial_gating(h, w_in, b_in, ln_g, ln_b, w_s, b_s, w_out):
    B, S, _ = h.shape
    z = jax.nn.gelu(h @ w_in + b_in)
    u, v = jnp.split(z, 2, axis=-1)
    v = _layer_norm(v, ln_g, ln_b)
    v = v.reshape(B, S // CHUNK, CHUNK, GMLP_GROUPS, GMLP_DIM // GMLP_GROUPS)
    causal = jnp.tril(jnp.ones((CHUNK, CHUNK), dtype=bool))
    w_causal = jnp.where(causal, w_s, 0.0).astype(v.dtype)
    sv = jnp.einsum('gts,bnsgc->bntgc', w_causal, v) + b_s.T[:, :, None]
    y = u * sv.reshape(B, S, GMLP_DIM)
    return y @ w_out


def stick_breaking_attention(h, w_in, w_out):
    B, S, _ = h.shape
    q, k, v = jnp.split(h @ w_in, 3, axis=-1)
    q, k, v = _split_heads(q), _split_heads(k), _split_heads(v)
    key_pos = jnp.arange(S)
    scale = HEAD_DIM ** -0.5

    def block(args):
        i, q_i = args
        q_pos = i * BLOCK_Q + jnp.arange(BLOCK_Q)
        z = jnp.einsum('bqhd,bkhd->bhqk', q_i, k).astype(jnp.float32) * scale
        strict = key_pos[None, :] < q_pos[:, None]
        log_not = jnp.where(strict, jax.nn.log_sigmoid(-z), 0.0)
        log_rest = lax.cumsum(log_not, axis=3, reverse=True) - log_not
        a = jnp.where(strict, jnp.exp(jax.nn.log_sigmoid(z) + log_rest), 0.0).astype(v.dtype)
        return jnp.einsum('bhqk,bkhd->bqhd', a, v)

    o = lax.map(block, (jnp.arange(S // BLOCK_Q), _query_blocks(q)))
    return _merge_blocks(o) @ w_out


def rwkv7_time_mix(h, mu, w_rkv, w0, w1, w2, a0, a1, a2, g1, g2, k_k, k_a, r_k, gn_g, gn_b, w_out):
    B, S, _ = h.shape
    f32 = jnp.float32
    h_prev = jnp.pad(h, ((0, 0), (1, 0), (0, 0)))[:, :-1]
    hm = h[None] + (h_prev - h)[None] * mu[:, None, None, :]
    r, k, v = jnp.einsum('cbsd,cde->cbse', hm[:3], w_rkv)
    x_w, x_a, x_g = hm[3], hm[4], hm[5]
    d = (w0 + jnp.tanh(x_w @ w1) @ w2).astype(f32)
    log_w = -jnp.exp(-jax.nn.softplus(-d) - 0.5)
    a = jax.nn.sigmoid(a0 + (x_a @ a1) @ a2)
    g = jax.nn.sigmoid(x_g @ g1) @ g2
    kk = _split_heads(k * k_k).astype(f32)
    kk = kk / jnp.maximum(jnp.linalg.norm(kk, axis=-1, keepdims=True), 1e-12)
    k = k * (1 + (a - 1) * k_a)

    def heads_t(t):
        return _split_heads(t).astype(f32).transpose(1, 0, 2, 3)

    def step(state, inp):
        r_t, w_t, k_t, v_t, kk_t, a_t = inp
        sa = jnp.einsum('bhij,bhj->bhi', state, -kk_t)
        state = (state * w_t[:, :, None, :]
                 + sa[..., None] * (kk_t * a_t)[:, :, None, :]
                 + v_t[..., None] * k_t[:, :, None, :])
        return state, jnp.einsum('bhij,bhj->bhi', state, r_t)

    state0 = jnp.zeros((B, N_HEADS, HEAD_DIM, HEAD_DIM), f32)
    xs = (heads_t(r), heads_t(jnp.exp(log_w)), heads_t(k), heads_t(v),
          kk.transpose(1, 0, 2, 3), heads_t(a))
    _, y = lax.scan(step, state0, xs)
    y = y.transpose(1, 0, 2, 3)
    mean = jnp.mean(y, axis=-1, keepdims=True)
    var = jnp.mean(jnp.square(y - mean), axis=-1, keepdims=True)
    y = ((y - mean) * lax.rsqrt(var + GN_EPS)).reshape(B, S, D_MODEL) * gn_g + gn_b
    r_h, k_h, v_h = (_split_heads(t).astype(f32) for t in (r, k, v))
    bonus = jnp.sum(r_h * k_h * r_k, axis=-1, keepdims=True) * v_h
    y = (y + bonus.reshape(B, S, D_MODEL)).astype(h.dtype)
    return (y * g) @ w_out


def grouped_moe(h, router_w, router_b, w_gate, w_up, w_down):
    B, S, D = h.shape
    T = B * S
    hf = h.reshape(T, D)
    scores = jax.nn.sigmoid((hf @ router_w).astype(jnp.float32))
    sel = (scores + router_b.astype(jnp.float32)).reshape(T, N_GROUPS, EXPERTS_PER_GROUP)
    group_score = jnp.sum(lax.top_k(sel, TOP_K)[0], axis=-1)
    g_idx = jnp.argmax(group_score, axis=-1)
    in_group = jnp.take_along_axis(sel, g_idx[:, None, None], axis=1)[:, 0]
    _, local = lax.top_k(in_group, TOP_K)
    e_idx = g_idx[:, None] * EXPERTS_PER_GROUP + local
    gate = jnp.take_along_axis(scores, e_idx, axis=-1)
    gate = gate / jnp.sum(gate, axis=-1, keepdims=True)
    combine = jnp.sum(jax.nn.one_hot(e_idx, N_EXPERTS, dtype=jnp.float32) * gate[..., None], axis=1)
    combine = combine.astype(h.dtype)
    y = jnp.zeros_like(hf)
    for e in range(N_EXPERTS):
        he = jax.nn.silu(hf @ w_gate[e]) * (hf @ w_up[e])
        y = y + combine[:, e:e + 1] * (he @ w_down[e])
    return y.reshape(B, S, D)


def setup_inputs(seed: int = 0) -> dict:
    key = jax.random.key(seed)
    keys = iter(jax.random.split(key, 64))

    def normal(shape, scale):
        return scale * jax.random.normal(next(keys), shape, jnp.float32)

    D, H, N = D_MODEL, N_HEADS, HEAD_DIM
    n_fox, n_gm, n_sb, n_rw = [len(range(m, DEPTH, N_MIXERS)) for m in range(N_MIXERS)]
    s_in = D ** -0.5
    inp = {}
    inp["x"] = normal((BATCH, SEQ, D), 1.0)
    inp["ln1_g"] = 1.0 + normal((DEPTH, D), 0.02)
    inp["ln1_b"] = normal((DEPTH, D), 0.02)
    inp["ln2_g"] = 1.0 + normal((DEPTH, D), 0.02)
    inp["ln2_b"] = normal((DEPTH, D), 0.02)
    inp["fox_w_in"] = jnp.concatenate([
        normal((n_fox, D, 2 * D), s_in), normal((n_fox, D, D), s_in * BETA),
        normal((n_fox, D, H), s_in), normal((n_fox, D, D), s_in)], axis=-1)
    inp["fox_b_f"] = 3.0 + normal((n_fox, H), 0.1)
    inp["fox_w_out"] = normal((n_fox, D, D), s_in * BETA)
    inp["gm_w_in"] = normal((n_gm, D, 2 * GMLP_DIM), s_in)
    inp["gm_b_in"] = normal((n_gm, 2 * GMLP_DIM), 0.02)
    inp["gm_ln_g"] = 1.0 + normal((n_gm, GMLP_DIM), 0.02)
    inp["gm_ln_b"] = normal((n_gm, GMLP_DIM), 0.02)
    inp["gm_w_s"] = normal((n_gm, GMLP_GROUPS, CHUNK, CHUNK), CHUNK ** -0.5)
    inp["gm_b_s"] = 1.0 + normal((n_gm, GMLP_GROUPS, CHUNK), 0.02)
    inp["gm_w_out"] = normal((n_gm, GMLP_DIM, D), GMLP_DIM ** -0.5 * BETA)
    inp["sb_w_in"] = jnp.concatenate([
        normal((n_sb, D, 2 * D), s_in), normal((n_sb, D, D), s_in * BETA)], axis=-1)
    inp["sb_w_out"] = normal((n_sb, D, D), s_in * BETA)
    inp["rw_mu"] = jax.random.uniform(next(keys), (n_rw, 6, D), jnp.float32)
    inp["rw_w_rkv"] = jnp.concatenate([
        normal((n_rw, 2, D, D), s_in), normal((n_rw, 1, D, D), s_in * BETA)], axis=1)
    inp["rw_w0"] = normal((n_rw, D), 0.5)
    inp["rw_w1"] = normal((n_rw, D, LORA_W), s_in)
    inp["rw_w2"] = normal((n_rw, LORA_W, D), LORA_W ** -0.5)
    inp["rw_a0"] = normal((n_rw, D), 0.1)
    inp["rw_a1"] = normal((n_rw, D, LORA_A), s_in)
    inp["rw_a2"] = normal((n_rw, LORA_A, D), LORA_A ** -0.5)
    inp["rw_g1"] = normal((n_rw, D, LORA_G), s_in)
    inp["rw_g2"] = normal((n_rw, LORA_G, D), LORA_G ** -0.5)
    inp["rw_k_k"] = 0.85 + normal((n_rw, D), 0.1)
    inp["rw_k_a"] = 1.0 + normal((n_rw, D), 0.1)
    inp["rw_r_k"] = normal((n_rw, H, N), 0.1)
    inp["rw_gn_g"] = 1.0 + normal((n_rw, D), 0.02)
    inp["rw_gn_b"] = normal((n_rw, D), 0.02)
    inp["rw_w_out"] = normal((n_rw, D, D), s_in * BETA)
    inp["router_w"] = normal((D, N_EXPERTS), s_in)
    inp["router_b"] = normal((N_EXPERTS,), 0.01)
    inp["moe_w_gate"] = normal((DEPTH, N_EXPERTS, D, D_EXPERT), s_in)
    inp["moe_w_up"] = normal((DEPTH, N_EXPERTS, D, D_EXPERT), s_in)
    inp["moe_w_down"] = normal((DEPTH, N_EXPERTS, D_EXPERT, D), D_EXPERT ** -0.5 * BETA)
    return inp


def reference(x, ln1_g, ln1_b, ln2_g, ln2_b,
              fox_w_in, fox_b_f, fox_w_out,
              gm_w_in, gm_b_in, gm_ln_g, gm_ln_b, gm_w_s, gm_b_s, gm_w_out,
              sb_w_in, sb_w_out,
              rw_mu, rw_w_rkv, rw_w0, rw_w1, rw_w2, rw_a0, rw_a1, rw_a2, rw_g1, rw_g2,
              rw_k_k, rw_k_a, rw_r_k, rw_gn_g, rw_gn_b, rw_w_out,
              router_w, router_b, moe_w_gate, moe_w_up, moe_w_down):
    h = x
    for i in range(DEPTH):
        kind, j = i % N_MIXERS, i // N_MIXERS
        if kind == 0:
            mix = forgetting_attention(h, fox_w_in[j], fox_b_f[j], fox_w_out[j])
        elif kind == 1:
            mix = chunked_spatial_gating(h, gm_w_in[j], gm_b_in[j], gm_ln_g[j], gm_ln_b[j],
                                         gm_w_s[j], gm_b_s[j], gm_w_out[j])
        elif kind == 2:
            mix = stick_breaking_attention(h, sb_w_in[j], sb_w_out[j])
        else:
            mix = rwkv7_time_mix(h, rw_mu[j], rw_w_rkv[j], rw_w0[j], rw_w1[j], rw_w2[j],
                                 rw_a0[j], rw_a1[j], rw_a2[j], rw_g1[j], rw_g2[j],
                                 rw_k_k[j], rw_k_a[j], rw_r_k[j], rw_gn_g[j], rw_gn_b[j],
                                 rw_w_out[j])
        h = _layer_norm(ALPHA * h + mix, ln1_g[i], ln1_b[i])
        ffn = grouped_moe(h, router_w, router_b, moe_w_gate[i], moe_w_up[i], moe_w_down[i])
        h = _layer_norm(ALPHA * h + ffn, ln2_g[i], ln2_b[i])
    return h
```

```python
import functools

import jax
import jax.numpy as jnp
from jax import lax
from jax.experimental import pallas as pl
from jax.experimental.pallas import tpu as pltpu

F32 = jnp.float32
BF16 = jnp.bfloat16
I32 = jnp.int32

N_HEADS = 16
HEAD_DIM = 64
CHUNK = 128
GMLP_GROUPS = 8
N_EXPERTS = 16
N_GROUPS = 4
EXPERTS_PER_GROUP = 4
DEPTH = 4
ALPHA = (2 * DEPTH) ** 0.25
LN_EPS = 1e-5
GN_EPS = 64e-5

VMEM_LIMIT_BYTES = 48 * 1024 * 1024
MOE_TILE = 256
RW_CHUNK = 64


def _cparams(*sem):
    return pltpu.CompilerParams(dimension_semantics=sem, vmem_limit_bytes=VMEM_LIMIT_BYTES)


def _layer_norm_rows(y, g, b):
    mu = jnp.mean(y, axis=-1, keepdims=True)
    yc = y - mu
    var = jnp.mean(yc * yc, axis=-1, keepdims=True)
    return yc * lax.rsqrt(var + LN_EPS) * g + b


def _split3(x):
    hi = x.astype(BF16)
    r1 = x - hi.astype(F32)
    mid = r1.astype(BF16)
    lo = (r1 - mid.astype(F32)).astype(BF16)
    return hi, mid, lo


def _dot(a, b):
    return jnp.dot(a, b, preferred_element_type=F32)


def _dot_nt(a, b):
    return lax.dot_general(a, b, (((1,), (1,)), ((), ())), preferred_element_type=F32)


def _linear_kernel(x_ref, w_ref, o_ref):
    o_ref[...] = _dot(x_ref[...], w_ref[...]).astype(o_ref.dtype)


def linear(x, w, out_dtype, tm=512, tn=512):
    M, K = x.shape
    N = w.shape[1]
    tn = min(tn, N)
    return pl.pallas_call(
        _linear_kernel,
        grid=(M // tm, N // tn),
        in_specs=[pl.BlockSpec((tm, K), lambda i, j: (i, 0)),
                  pl.BlockSpec((K, tn), lambda i, j: (0, j))],
        out_specs=pl.BlockSpec((tm, tn), lambda i, j: (i, j)),
        out_shape=jax.ShapeDtypeStruct((M, N), out_dtype),
        compiler_params=_cparams("parallel", "parallel"),
        name="linear",
    )(x, w)


def _proj_ln_kernel(a_ref, w_ref, h_ref, g_ref, b_ref, of_ref, ob_ref):
    mix = _dot(a_ref[...], w_ref[...])
    out = _layer_norm_rows(ALPHA * h_ref[...] + mix, g_ref[...], b_ref[...])
    of_ref[...] = out
    ob_ref[...] = out.astype(BF16)


def proj_ln(a, w, h, g, b, tm=512):
    M, K = a.shape
    D = w.shape[1]
    return pl.pallas_call(
        _proj_ln_kernel,
        grid=(M // tm,),
        in_specs=[pl.BlockSpec((tm, K), lambda i: (i, 0)),
                  pl.BlockSpec((K, D), lambda i: (0, 0)),
                  pl.BlockSpec((tm, D), lambda i: (i, 0)),
                  pl.BlockSpec((1, D), lambda i: (0, 0)),
                  pl.BlockSpec((1, D), lambda i: (0, 0))],
        out_specs=[pl.BlockSpec((tm, D), lambda i: (i, 0)),
                   pl.BlockSpec((tm, D), lambda i: (i, 0))],
        out_shape=[jax.ShapeDtypeStruct((M, D), F32), jax.ShapeDtypeStruct((M, D), BF16)],
        compiler_params=_cparams("parallel"),
        name="proj_ln",
    )(a, w, h, g.reshape(1, D), b.reshape(1, D))


def _router_kernel(h_ref, rwt_ref, rb_ref, su_ref, idx_ref, gate_ref, cnt_ref, base_ref):
    step = pl.program_id(0)
    tr = h_ref.shape[0]

    @pl.when(step == 0)
    def _():
        base_ref[...] = jnp.zeros_like(base_ref)

    logits = lax.dot_general(rwt_ref[...], h_ref[...], (((1,), (1,)), ((), ())),
                             precision=lax.Precision.HIGHEST, preferred_element_type=F32)
    scores = jax.nn.sigmoid(logits)
    sel = scores + rb_ref[...]
    s = [sel[e:e + 1, :] for e in range(N_EXPERTS)]
    sc = [scores[e:e + 1, :] for e in range(N_EXPERTS)]

    def top2sum(v):
        best = v[0] + v[1]
        for a in range(4):
            for b in range(a + 1, 4):
                if (a, b) != (0, 1):
                    best = jnp.maximum(best, v[a] + v[b])
        return best

    gs = [top2sum(s[4 * g:4 * g + 4]) for g in range(N_GROUPS)]
    best, gi = gs[0], jnp.zeros((1, tr), I32)
    for g in range(1, N_GROUPS):
        better = gs[g] > best
        gi = jnp.where(better, g, gi)
        best = jnp.where(better, gs[g], best)

    def pick_group(rows):
        out = []
        for j in range(EXPERTS_PER_GROUP):
            v = rows[j]
            for g in range(1, N_GROUPS):
                v = jnp.where(gi == g, rows[4 * g + j], v)
            out.append(v)
        return out

    v = pick_group(s)
    raw = pick_group(sc)
    m1, l1, r1 = v[0], jnp.zeros((1, tr), I32), raw[0]
    for j in range(1, 4):
        better = v[j] > m1
        l1 = jnp.where(better, j, l1)
        m1 = jnp.where(better, v[j], m1)
        r1 = jnp.where(better, raw[j], r1)
    m2 = jnp.full((1, tr), -jnp.inf, F32)
    l2 = jnp.zeros((1, tr), I32)
    r2 = jnp.zeros((1, tr), F32)
    for j in range(4):
        better = jnp.logical_and(l1 != j, v[j] > m2)
        l2 = jnp.where(better, j, l2)
        m2 = jnp.where(better, v[j], m2)
        r2 = jnp.where(better, raw[j], r2)
    e1 = gi * EXPERTS_PER_GROUP + l1
    e2 = gi * EXPERTS_PER_GROUP + l2
    den = r1 + r2
    gate_ref[...] = jnp.concatenate([r1 / den, r2 / den], axis=0)

    eidx = lax.broadcasted_iota(I32, (N_EXPERTS, tr), 0)
    oh1 = eidx == e1
    oh2 = eidx == e2
    oh = jnp.where(jnp.logical_or(oh1, oh2), 1.0, 0.0)
    rank = _dot(oh.astype(BF16), su_ref[...]) + base_ref[...]
    rk1 = jnp.sum(jnp.where(oh1, rank, 0.0), axis=0, keepdims=True).astype(I32)
    rk2 = jnp.sum(jnp.where(oh2, rank, 0.0), axis=0, keepdims=True).astype(I32)
    idx_ref[...] = jnp.concatenate([e1, e2, rk1, rk2], axis=0)
    base_ref[...] = base_ref[...] + jnp.sum(oh, axis=1, keepdims=True)
    cnt_ref[...] = jnp.broadcast_to(base_ref[...], cnt_ref.shape)


def moe_router(h, router_w, router_b, tr=512):
    T, D = h.shape
    su = jnp.triu(jnp.ones((tr, tr), BF16), k=1)
    idx, gate, cnt = pl.pallas_call(
        _router_kernel,
        grid=(T // tr,),
        in_specs=[pl.BlockSpec((tr, D), lambda i: (i, 0)),
                  pl.BlockSpec((N_EXPERTS, D), lambda i: (0, 0)),
                  pl.BlockSpec((N_EXPERTS, 1), lambda i: (0, 0)),
                  pl.BlockSpec((tr, tr), lambda i: (0, 0))],
        out_specs=[pl.BlockSpec((4, tr), lambda i: (0, i)),
                   pl.BlockSpec((2, tr), lambda i: (0, i)),
                   pl.BlockSpec((N_EXPERTS, 128), lambda i: (0, 0))],
        out_shape=[jax.ShapeDtypeStruct((4, T), I32), jax.ShapeDtypeStruct((2, T), F32),
                   jax.ShapeDtypeStruct((N_EXPERTS, 128), F32)],
        scratch_shapes=[pltpu.VMEM((N_EXPERTS, 1), F32)],
        compiler_params=_cparams("arbitrary"),
        name="moe_router",
    )(h, router_w.T, router_b.reshape(N_EXPERTS, 1), su)
    return idx, gate, cnt[:, 0].astype(I32)


def _dispatch_kernel(p1_ref, p2_ref, h_ref, z_ref, xs_ref, sem):
    del z_ref
    td = h_ref.shape[0]

    def issue(r, c):
        pltpu.make_async_copy(h_ref.at[pl.ds(r, 1)], xs_ref.at[pl.ds(p1_ref[r], 1)], sem).start()
        pltpu.make_async_copy(h_ref.at[pl.ds(r, 1)], xs_ref.at[pl.ds(p2_ref[r], 1)], sem).start()
        return c

    lax.fori_loop(0, td, issue, 0)

    def drain(r, c):
        pltpu.make_async_copy(h_ref.at[pl.ds(0, 1)], xs_ref.at[pl.ds(0, 1)], sem).wait()
        pltpu.make_async_copy(h_ref.at[pl.ds(0, 1)], xs_ref.at[pl.ds(0, 1)], sem).wait()
        return c

    lax.fori_loop(0, td, drain, 0)


def moe_dispatch(h, pos1, pos2, n_rows, td=256):
    T, D = h.shape
    zeros = jnp.zeros((n_rows, D), h.dtype)
    return pl.pallas_call(
        _dispatch_kernel,
        grid=(T // td,),
        in_specs=[pl.BlockSpec((td,), lambda i: (i,), memory_space=pltpu.SMEM),
                  pl.BlockSpec((td,), lambda i: (i,), memory_space=pltpu.SMEM),
                  pl.BlockSpec((td, D), lambda i: (i, 0)),
                  pl.BlockSpec(memory_space=pl.ANY)],
        out_specs=pl.BlockSpec(memory_space=pl.ANY),
        out_shape=jax.ShapeDtypeStruct((n_rows, D), h.dtype),
        scratch_shapes=[pltpu.SemaphoreType.DMA(())],
        input_output_aliases={3: 0},
        compiler_params=_cparams("arbitrary"),
        name="moe_dispatch",
    )(pos1, pos2, h, zeros)


def _expert_kernel(te_ref, nu_ref, x_ref, wg_ref, wu_ref, wd_ref, o_ref):
    del te_ref

    @pl.when(pl.program_id(0) < nu_ref[0])
    def _():
        x = x_ref[...].astype(BF16)
        gate = _dot(x, wg_ref[0])
        up = _dot(x, wu_ref[0])
        he = (jax.nn.silu(gate) * up).astype(BF16)
        o_ref[...] = _dot(he, wd_ref[0])

    @pl.when(pl.program_id(0) >= nu_ref[0])
    def _():
        o_ref[...] = jnp.zeros_like(o_ref)


def moe_experts(xs, tile_expert, n_used, w_gate, w_up, w_down):
    n_rows, D = xs.shape
    E, _, De = w_gate.shape
    n_tiles = n_rows // MOE_TILE

    def row_map(i, te, nu):
        return (jnp.minimum(i, nu[0] - 1), 0)

    def w_map(i, te, nu):
        return (te[i], 0, 0)

    grid_spec = pltpu.PrefetchScalarGridSpec(
        num_scalar_prefetch=2,
        grid=(n_tiles,),
        in_specs=[pl.BlockSpec((MOE_TILE, D), row_map),
                  pl.BlockSpec((1, D, De), w_map),
                  pl.BlockSpec((1, D, De), w_map),
                  pl.BlockSpec((1, De, D), w_map)],
        out_specs=pl.BlockSpec((MOE_TILE, D), lambda i, te, nu: (i, 0)),
    )
    return pl.pallas_call(
        _expert_kernel,
        grid_spec=grid_spec,
        out_shape=jax.ShapeDtypeStruct((n_rows, D), F32),
        compiler_params=_cparams("arbitrary"),
        name="moe_experts",
    )(tile_expert, n_used, xs, w_gate, w_up, w_down)


def _combine_kernel(p1_ref, p2_ref, g1_ref, g2_ref, h_ref, lg_ref, lb_ref, os_ref,
                    of_ref, ob_ref, buf1, buf2, sem):
    tc = h_ref.shape[0]

    def issue(r, c):
        pltpu.make_async_copy(os_ref.at[pl.ds(p1_ref[r], 1)], buf1.at[pl.ds(r, 1)], sem).start()
        pltpu.make_async_copy(os_ref.at[pl.ds(p2_ref[r], 1)], buf2.at[pl.ds(r, 1)], sem).start()
        return c

    lax.fori_loop(0, tc, issue, 0)

    def drain(r, c):
        pltpu.make_async_copy(os_ref.at[pl.ds(0, 1)], buf1.at[pl.ds(0, 1)], sem).wait()
        pltpu.make_async_copy(os_ref.at[pl.ds(0, 1)], buf2.at[pl.ds(0, 1)], sem).wait()
        return c

    lax.fori_loop(0, tc, drain, 0)
    ffn = g1_ref[...] * buf1[...] + g2_ref[...] * buf2[...]
    out = _layer_norm_rows(ALPHA * h_ref[...] + ffn, lg_ref[...], lb_ref[...])
    of_ref[...] = out
    ob_ref[...] = out.astype(BF16)


def moe_combine(osorted, pos1, pos2, g1, g2, h, ln_g, ln_b, tc=256):
    T, D = h.shape
    return pl.pallas_call(
        _combine_kernel,
        grid=(T // tc,),
        in_specs=[pl.BlockSpec((tc,), lambda i: (i,), memory_space=pltpu.SMEM),
                  pl.BlockSpec((tc,), lambda i: (i,), memory_space=pltpu.SMEM),
                  pl.BlockSpec((tc, 1), lambda i: (i, 0)),
                  pl.BlockSpec((tc, 1), lambda i: (i, 0)),
                  pl.BlockSpec((tc, D), lambda i: (i, 0)),
                  pl.BlockSpec((1, D), lambda i: (0, 0)),
                  pl.BlockSpec((1, D), lambda i: (0, 0)),
                  pl.BlockSpec(memory_space=pl.ANY)],
        out_specs=[pl.BlockSpec((tc, D), lambda i: (i, 0)),
                   pl.BlockSpec((tc, D), lambda i: (i, 0))],
        out_shape=[jax.ShapeDtypeStruct((T, D), F32), jax.ShapeDtypeStruct((T, D), BF16)],
        scratch_shapes=[pltpu.VMEM((tc, D), F32), pltpu.VMEM((tc, D), F32),
                        pltpu.SemaphoreType.DMA(())],
        compiler_params=_cparams("arbitrary"),
        name="moe_combine",
    )(pos1, pos2, g1, g2, h, ln_g.reshape(1, D), ln_b.reshape(1, D), osorted)


def grouped_moe_ln(h, router_w, router_b, w_gate, w_up, w_down, ln_g, ln_b):
    T, D = h.shape
    idx, gate, cnt = moe_router(h, router_w, router_b)
    e1, e2, rk1, rk2 = idx[0], idx[1], idx[2], idx[3]
    tiles = (cnt + MOE_TILE - 1) // MOE_TILE
    tile_end = jnp.cumsum(tiles)
    start = (tile_end - tiles) * MOE_TILE
    pos1 = start[e1] + rk1
    pos2 = start[e2] + rk2
    n_tiles = 2 * T // MOE_TILE + N_EXPERTS
    tile_ids = jnp.arange(n_tiles, dtype=I32)
    tile_expert = jnp.minimum(
        jnp.sum((tile_ids[:, None] >= tile_end[None, :]).astype(I32), axis=1), N_EXPERTS - 1)
    n_used = tile_end[-1:].astype(I32)
    xs = moe_dispatch(h, pos1, pos2, n_tiles * MOE_TILE)
    osorted = moe_experts(xs, tile_expert.astype(I32), n_used, w_gate, w_up, w_down)
    return moe_combine(osorted, pos1, pos2, gate[0].reshape(T, 1), gate[1].reshape(T, 1),
                       h, ln_g, ln_b)


def _forget_kernel(h_ref, wft_ref, bf_ref, ui_ref, c_ref, carry_ref):
    @pl.when(pl.program_id(1) == 0)
    def _():
        carry_ref[...] = jnp.zeros_like(carry_ref)

    logits = lax.dot_general(wft_ref[...], h_ref[...], (((1,), (1,)), ((), ())),
                             precision=lax.Precision.HIGHEST, preferred_element_type=F32)
    log_f = jax.nn.log_sigmoid(logits + bf_ref[...])
    hi, mid, lo = _split3(log_f)
    ui = ui_ref[...]
    cum = _dot(hi, ui) + _dot(mid, ui) + _dot(lo, ui) + carry_ref[...]
    c_ref[0] = cum
    carry_ref[...] = cum[:, -1:]


def fox_forget_cumsum(h, w_f, b_f, B, S, ts=512):
    T, D = h.shape
    H = w_f.shape[1]
    ui = jnp.triu(jnp.ones((ts, ts), BF16))
    ns = S // ts
    return pl.pallas_call(
        _forget_kernel,
        grid=(B, ns),
        in_specs=[pl.BlockSpec((ts, D), lambda b, i: (b * ns + i, 0)),
                  pl.BlockSpec((H, D), lambda b, i: (0, 0)),
                  pl.BlockSpec((H, 1), lambda b, i: (0, 0)),
                  pl.BlockSpec((ts, ts), lambda b, i: (0, 0))],
        out_specs=pl.BlockSpec((1, H, ts), lambda b, i: (b, 0, i)),
        out_shape=jax.ShapeDtypeStruct((B, H, S), F32),
        scratch_shapes=[pltpu.VMEM((H, 1), F32)],
        compiler_params=_cparams("parallel", "arbitrary"),
        name="fox_forget",
    )(h, w_f.T, b_f.reshape(H, 1), ui)


def _head_pair_masks(q):
    lane = lax.broadcasted_iota(I32, q.shape, 1)
    zero = jnp.zeros_like(q)
    return jnp.where(lane < HEAD_DIM, q, zero), jnp.where(lane >= HEAD_DIM, q, zero)


def _fox_kernel(q_ref, k_ref, v_ref, crow_ref, ccol_ref, og_ref, o_ref, qa, qb, ct, m, l, acc):
    hp, i, j = pl.program_id(1), pl.program_id(2), pl.program_id(3)
    tq, tk = q_ref.shape[1], k_ref.shape[1]

    @pl.when(j == 0)
    def _():
        qa[...], qb[...] = _head_pair_masks(q_ref[0])
        cc = ccol_ref[0]
        head = lax.broadcasted_iota(I32, cc.shape, 1)
        ct[0] = jnp.sum(jnp.where(head == 2 * hp, cc, 0.0), axis=1, keepdims=True)
        ct[1] = jnp.sum(jnp.where(head == 2 * hp + 1, cc, 0.0), axis=1, keepdims=True)
        m[...] = jnp.full(m.shape, -jnp.inf, F32)
        l[...] = jnp.zeros_like(l)
        acc[...] = jnp.zeros_like(acc)

    def step(masked):
        k = k_ref[0]
        v = v_ref[0]
        crow = crow_ref[0, 0]
        if masked:
            row = lax.broadcasted_iota(I32, (tq, tk), 0)
            col = lax.broadcasted_iota(I32, (tq, tk), 1)
            keep = col <= row
        for x, qx in enumerate((qa, qb)):
            s = _dot_nt(qx[...], k) - crow[x:x + 1, :]
            if masked:
                s = jnp.where(keep, s, -jnp.inf)
            c_t = ct[x]
            m_prev = m[x]
            m_new = jnp.maximum(m_prev, jnp.max(s, axis=1, keepdims=True) + c_t)
            alpha = jnp.exp(m_prev - m_new)
            p = jnp.exp(s - (m_new - c_t))
            l[x] = alpha * l[x] + jnp.sum(p, axis=1, keepdims=True)
            acc[x] = alpha * acc[x] + _dot(p.astype(BF16), v)
            m[x] = m_new

    pl.when(j < i)(lambda: step(False))
    pl.when(j == i)(lambda: step(True))

    @pl.when(j == i)
    def _():
        lane = lax.broadcasted_iota(I32, (tq, 2 * HEAD_DIM), 1)
        o = jnp.where(lane < HEAD_DIM, acc[0] / l[0], acc[1] / l[1])
        o_ref[0] = (o * jax.nn.sigmoid(og_ref[0].astype(F32))).astype(BF16)


def fox_attention(qkv, og, c, B, S, tq=512):
    T = qkv.shape[0]
    D = N_HEADS * HEAD_DIM
    hp_n = N_HEADS // 2
    nq = S // tq
    qkv3 = qkv.reshape(B, S, 3 * D)
    crow = c.reshape(B, hp_n, 2, S)
    ccol = jnp.transpose(c, (0, 2, 1))
    kv_blk = lambda off: (lambda b, hp, i, j: (b, jnp.minimum(j, i), off + hp))
    out = pl.pallas_call(
        _fox_kernel,
        grid=(B, hp_n, nq, nq),
        in_specs=[pl.BlockSpec((1, tq, 128), lambda b, hp, i, j: (b, i, hp)),
                  pl.BlockSpec((1, tq, 128), kv_blk(hp_n)),
                  pl.BlockSpec((1, tq, 128), kv_blk(2 * hp_n)),
                  pl.BlockSpec((1, 1, 2, tq), lambda b, hp, i, j: (b, hp, 0, jnp.minimum(j, i))),
                  pl.BlockSpec((1, tq, N_HEADS), lambda b, hp, i, j: (b, i, 0)),
                  pl.BlockSpec((1, tq, 128), lambda b, hp, i, j: (b, i, hp))],
        out_specs=pl.BlockSpec((1, tq, 128), lambda b, hp, i, j: (b, i, hp)),
        out_shape=jax.ShapeDtypeStruct((B, S, D), BF16),
        scratch_shapes=[pltpu.VMEM((tq, 128), BF16), pltpu.VMEM((tq, 128), BF16),
                        pltpu.VMEM((2, tq, 1), F32), pltpu.VMEM((2, tq, 1), F32),
                        pltpu.VMEM((2, tq, 1), F32), pltpu.VMEM((2, tq, 128), F32)],
        compiler_params=_cparams("parallel", "parallel", "parallel", "arbitrary"),
        name="fox_attention",
    )(qkv3, qkv3, qkv3, crow, ccol, og.reshape(B, S, D))
    return out.reshape(T, D)


def _sb_kernel(q_ref, k_ref, v_ref, u_ref, o_ref, qa, qb, rs, acc):
    i, j = pl.program_id(2), pl.program_id(3)
    tq, tk = q_ref.shape[1], k_ref.shape[1]

    @pl.when(j == 0)
    def _():
        qa[...], qb[...] = _head_pair_masks(q_ref[0])
        rs[...] = jnp.zeros_like(rs)
        acc[...] = jnp.zeros_like(acc)

    def step(masked):
        k = k_ref[0]
        v = v_ref[0]
        u = u_ref[...]
        if masked:
            row = lax.broadcasted_iota(I32, (tq, tk), 0)
            col = lax.broadcasted_iota(I32, (tq, tk), 1)
            keep = col < row
        for x, qx in enumerate((qa, qb)):
            z = _dot_nt(qx[...], k)
            soft = jnp.log(1.0 + jnp.exp(-jnp.abs(z)))
            log_beta = jnp.minimum(z, 0.0) - soft
            log_not = log_beta - z
            if masked:
                log_not = jnp.where(keep, log_not, 0.0)
            hi = log_not.astype(BF16)
            lo = (log_not - hi.astype(F32)).astype(BF16)
            log_rest = _dot(hi, u) + _dot(lo, u) + rs[x]
            a = jnp.exp(log_beta + log_rest)
            if masked:
                a = jnp.where(keep, a, 0.0)
            acc[x] = acc[x] + _dot(a.astype(BF16), v)
            rs[x] = rs[x] + jnp.sum(log_not, axis=1, keepdims=True)

    pl.when(j == 0)(lambda: step(True))
    pl.when(jnp.logical_and(j > 0, j <= i))(lambda: step(False))

    @pl.when(j == i)
    def _():
        lane = lax.broadcasted_iota(I32, (tq, 2 * HEAD_DIM), 1)
        o_ref[0] = jnp.where(lane < HEAD_DIM, acc[0], acc[1]).astype(BF16)


def stick_breaking_attention(qkv, B, S, tq=256):
    T = qkv.shape[0]
    D = N_HEADS * HEAD_DIM
    hp_n = N_HEADS // 2
    nq = S // tq
    qkv3 = qkv.reshape(B, S, 3 * D)
    u = jnp.tril(jnp.ones((tq, tq), BF16), k=-1)
    kv_blk = lambda off: (lambda b, hp, i, j: (b, jnp.maximum(i - j, 0), off + hp))
    out = pl.pallas_call(
        _sb_kernel,
        grid=(B, hp_n, nq, nq),
        in_specs=[pl.BlockSpec((1, tq, 128), lambda b, hp, i, j: (b, i, hp)),
                  pl.BlockSpec((1, tq, 128), kv_blk(hp_n)),
                  pl.BlockSpec((1, tq, 128), kv_blk(2 * hp_n)),
                  pl.BlockSpec((tq, tq), lambda b, hp, i, j: (0, 0))],
        out_specs=pl.BlockSpec((1, tq, 128), lambda b, hp, i, j: (b, i, hp)),
        out_shape=jax.ShapeDtypeStruct((B, S, D), BF16),
        scratch_shapes=[pltpu.VMEM((tq, 128), BF16), pltpu.VMEM((tq, 128), BF16),
                        pltpu.VMEM((2, tq, 1), F32), pltpu.VMEM((2, tq, 128), F32)],
        compiler_params=_cparams("parallel", "parallel", "parallel", "arbitrary"),
        name="stick_breaking",
    )(qkv3, qkv3, qkv3, u)
    return out.reshape(T, D)


def _gmlp_kernel(h_ref, win_ref, bin_ref, lg_ref, lb_ref, ws_ref, bst_ref, o_ref):
    tm = h_ref.shape[0]
    dg = o_ref.shape[1]
    z = jax.nn.gelu(_dot(h_ref[...], win_ref[...]) + bin_ref[...])
    u = z[:, :dg]
    v = _layer_norm_rows(z[:, dg:], lg_ref[...], lb_ref[...]).astype(BF16)
    row = lax.broadcasted_iota(I32, (CHUNK, CHUNK), 0)
    col = lax.broadcasted_iota(I32, (CHUNK, CHUNK), 1)
    gw = dg // GMLP_GROUPS
    for g in range(GMLP_GROUPS):
        wc = jnp.where(col <= row, ws_ref[g], 0.0).astype(BF16)
        cs = slice(g * gw, (g + 1) * gw)
        for c in range(tm // CHUNK):
            rs = slice(c * CHUNK, (c + 1) * CHUNK)
            sv = _dot(wc, v[rs, cs]) + bst_ref[:, g:g + 1]
            o_ref[rs, cs] = (u[rs, cs] * sv).astype(BF16)


def gmlp_gate(hb, w_in, b_in, ln_g, ln_b, w_s, b_s, tm=256):
    T, D = hb.shape
    dg = w_in.shape[1] // 2
    return pl.pallas_call(
        _gmlp_kernel,
        grid=(T // tm,),
        in_specs=[pl.BlockSpec((tm, D), lambda i: (i, 0)),
                  pl.BlockSpec((D, 2 * dg), lambda i: (0, 0)),
                  pl.BlockSpec((1, 2 * dg), lambda i: (0, 0)),
                  pl.BlockSpec((1, dg), lambda i: (0, 0)),
                  pl.BlockSpec((1, dg), lambda i: (0, 0)),
                  pl.BlockSpec((GMLP_GROUPS, CHUNK, CHUNK), lambda i: (0, 0, 0)),
                  pl.BlockSpec((CHUNK, GMLP_GROUPS), lambda i: (0, 0))],
        out_specs=pl.BlockSpec((tm, dg), lambda i: (i, 0)),
        out_shape=jax.ShapeDtypeStruct((T, dg), BF16),
        compiler_params=_cparams("parallel"),
        name="gmlp_gate",
    )(hb, w_in, b_in.reshape(1, -1), ln_g.reshape(1, -1), ln_b.reshape(1, -1), w_s, b_s.T)


RW_LANES = 4 * HEAD_DIM


def _seg_sum(x, bones):
    hi = x.astype(BF16)
    lo = (x - hi.astype(F32)).astype(BF16)
    return _dot(hi, bones) + _dot(lo, bones)


def _rwkv_prep_kernel(seq_len, h_ref, hp_ref, mu_ref, wrkv_ref, w1_ref, w2_ref, a1_ref, a2_ref,
                      g1_ref, g2_ref, w0_ref, a0_ref, kk_ref, ka_ref, bones_ref,
                      r_ref, lw_ref, k_ref, v_ref, kkn_ref, b_ref, g_ref):
    i = pl.program_id(0)
    tm, D = h_ref.shape
    h = h_ref[...]
    at_start = (i * tm) % seq_len == 0
    prev_row = jnp.where(at_start, 0.0, hp_ref[7:8, :])
    row = lax.broadcasted_iota(I32, (tm, D), 0)
    h_prev = jnp.where(row == 0, prev_row, pltpu.roll(h, 1, axis=0))
    dx = h_prev - h

    def mixed(c):
        return (h + dx * mu_ref[c:c + 1, :]).astype(BF16)

    r = _dot(mixed(0), wrkv_ref[0])
    k = _dot(mixed(1), wrkv_ref[1])
    v = _dot(mixed(2), wrkv_ref[2])
    d = w0_ref[...] + _dot(jnp.tanh(_dot(mixed(3), w1_ref[...])).astype(BF16), w2_ref[...])
    lw = -jnp.exp(-jax.nn.softplus(-d) - 0.5)
    a = jax.nn.sigmoid(a0_ref[...] + _dot(_dot(mixed(4), a1_ref[...]).astype(BF16), a2_ref[...]))
    g = _dot(jax.nn.sigmoid(_dot(mixed(5), g1_ref[...])).astype(BF16), g2_ref[...])
    kk = k * kk_ref[...]
    bones = bones_ref[...]
    norm_sq = jnp.concatenate(
        [_seg_sum(jnp.square(kk[:, q * RW_LANES:(q + 1) * RW_LANES]), bones) for q in range(D // RW_LANES)],
        axis=1)
    kkn = kk / jnp.maximum(jnp.sqrt(norm_sq), 1e-12)
    r_ref[...] = r
    lw_ref[...] = lw
    k_ref[...] = k * (1.0 + (a - 1.0) * ka_ref[...])
    v_ref[...] = v
    kkn_ref[...] = kkn
    b_ref[...] = kkn * a
    g_ref[...] = g


def _block_ones():
    idx = jnp.arange(RW_LANES) // HEAD_DIM
    return (idx[:, None] == idx[None, :]).astype(BF16)


def rwkv_prep(h, mu, w_rkv, w0, w1, w2, a0, a1, a2, g1, g2, k_k, k_a, S, tm=256):
    T, D = h.shape
    row = lambda i: (i, 0)
    full2 = lambda i: (0, 0)
    vec = lambda x: x.reshape(1, D)
    bf = lambda x: x.astype(BF16)
    outs = pl.pallas_call(
        functools.partial(_rwkv_prep_kernel, S),
        grid=(T // tm,),
        in_specs=[pl.BlockSpec((tm, D), row),
                  pl.BlockSpec((8, D), lambda i: (jnp.maximum(i * (tm // 8) - 1, 0), 0)),
                  pl.BlockSpec((6, D), full2),
                  pl.BlockSpec((3, D, D), lambda i: (0, 0, 0)),
                  pl.BlockSpec(w1.shape, full2), pl.BlockSpec(w2.shape, full2),
                  pl.BlockSpec(a1.shape, full2), pl.BlockSpec(a2.shape, full2),
                  pl.BlockSpec(g1.shape, full2), pl.BlockSpec(g2.shape, full2),
                  pl.BlockSpec((1, D), full2), pl.BlockSpec((1, D), full2),
                  pl.BlockSpec((1, D), full2), pl.BlockSpec((1, D), full2),
                  pl.BlockSpec((RW_LANES, RW_LANES), full2)],
        out_specs=[pl.BlockSpec((tm, D), row)] * 7,
        out_shape=[jax.ShapeDtypeStruct((T, D), F32)] * 7,
        compiler_params=_cparams("parallel"),
        name="rwkv_prep",
    )(h, h, mu, bf(w_rkv), bf(w1), bf(w2), bf(a1), bf(a2), bf(g1), bf(g2),
      vec(w0), vec(a0), vec(k_k), vec(k_a), _block_ones())
    return outs


def _rwkv_chunk(r, lw, k, v, kk, b, P, cst):
    C = RW_CHUNK
    lincl, mbd, eye, strict, incl, levels = cst
    eye_f = jnp.where(eye, 1.0, 0.0)
    eye_b = eye_f.astype(BF16)

    def bd(y):
        return jnp.where(mbd, jnp.concatenate([y.astype(BF16)] * 4, axis=0), jnp.zeros((), BF16))

    hi, mid, lo = _split3(lw)
    cl = _dot(lincl, hi) + _dot(lincl, mid) + _dot(lincl, lo)
    cl_last = cl[C - 1:C, :]
    kkt = kk * jnp.exp(cl - lw)
    rt = r * jnp.exp(cl)
    g_inv = jnp.exp(-cl)
    g_tail = jnp.exp(cl_last - cl)
    kbd, bbd = bd(k * g_inv), bd(b * g_inv)
    kkt_b, rt_b = kkt.astype(BF16), rt.astype(BF16)
    a_k = jnp.where(strict, _dot_nt(kkt_b, kbd), 0.0)
    a_b = jnp.where(strict, _dot_nt(kkt_b, bbd), 0.0)
    b_k = jnp.where(incl, _dot_nt(rt_b, kbd), 0.0)
    b_b = jnp.where(incl, _dot_nt(rt_b, bbd), 0.0).astype(BF16)

    dm = eye_f - jnp.where(levels[0], a_b, 0.0)
    for lvl in levels[1:]:
        w = _dot(dm.astype(BF16), bd(jnp.where(lvl, a_b, 0.0)))
        dm = dm - _dot(w.astype(BF16), bd(dm))
    tm_b = dm.astype(BF16)

    vbd = bd(v)
    kp = _dot(tm_b, bd(kkt))
    vp = _dot(tm_b, bd(_dot(a_k.astype(BF16), vbd)))
    kpbd, vpbd = bd(kp), bd(vp)
    y_loc = _dot(b_k.astype(BF16), vbd) - _dot(b_b, vpbd)
    r_eff = (rt - _dot(b_b, kpbd)).astype(BF16)
    kh_t = _dot_nt(eye_b, bd(k * g_tail)).astype(BF16)
    bh_t = _dot_nt(eye_b, bd(b * g_tail)).astype(BF16)
    m = eye_f * jnp.exp(cl_last) - _dot(bh_t, kpbd)
    n = _dot(kh_t, vbd) - _dot(bh_t, vpbd)

    p_hi = P.astype(BF16)
    pbd_hi, pbd_lo = bd(p_hi), bd(P - p_hi.astype(F32))
    y = _dot(r_eff, pbd_hi) + _dot(r_eff, pbd_lo) + y_loc
    m_hi = m.astype(BF16)
    m_lo = (m - m_hi.astype(F32)).astype(BF16)
    p_new = _dot(m_hi, pbd_hi) + _dot(m_hi, pbd_lo) + _dot(m_lo, pbd_hi) + n
    return y, p_new


def _rwkv_scan_kernel(r_ref, lw_ref, k_ref, v_ref, kk_ref, b_ref, g_ref, rk_ref, gg_ref, gb_ref,
                      lincl_ref, bones_ref, o_ref, state):
    C = RW_CHUNK

    @pl.when(pl.program_id(1) == 0)
    def _():
        state[...] = jnp.zeros_like(state)

    t = lax.broadcasted_iota(I32, (C, RW_LANES), 0)
    i = lax.broadcasted_iota(I32, (C, RW_LANES), 1) % C
    levels = []
    for sh in range(6):
        tb, ib = t >> sh, i >> sh
        levels.append(jnp.logical_and(tb == ib + 1, (tb & 1) == 1))
    rr = lax.broadcasted_iota(I32, (RW_LANES, RW_LANES), 0) // HEAD_DIM
    cc = lax.broadcasted_iota(I32, (RW_LANES, RW_LANES), 1) // HEAD_DIM
    cst = (lincl_ref[...], rr == cc, i == t, i < t, i <= t, levels)
    bones = bones_ref[...]

    for q in range(r_ref.shape[1] // RW_LANES):
        sl = slice(q * RW_LANES, (q + 1) * RW_LANES)
        r, k, v = r_ref[:, sl], k_ref[:, sl], v_ref[:, sl]
        y, p_new = _rwkv_chunk(r, lw_ref[:, sl], k, v, kk_ref[:, sl], b_ref[:, sl], state[q], cst)
        state[q] = p_new
        mean = _seg_sum(y, bones) * (1.0 / HEAD_DIM)
        yc = y - mean
        var = _seg_sum(yc * yc, bones) * (1.0 / HEAD_DIM)
        yn = yc * lax.rsqrt(var + GN_EPS) * gg_ref[:, sl] + gb_ref[:, sl]
        bonus = _seg_sum(r * k * rk_ref[:, sl], bones) * v
        o_ref[:, sl] = ((yn + bonus) * g_ref[:, sl]).astype(BF16)


def rwkv_scan(r, lw, k, v, kkn, b, g, r_k, gn_g, gn_b, B, S):
    T, D = r.shape
    C = RW_CHUNK
    nc = S // C
    row = lambda bi, ci: (bi * nc + ci, 0)
    full2 = lambda bi, ci: (0, 0)
    lincl = jnp.tril(jnp.ones((C, C), BF16))
    return pl.pallas_call(
        _rwkv_scan_kernel,
        grid=(B, nc),
        in_specs=[pl.BlockSpec((C, D), row)] * 7 + [pl.BlockSpec((1, D), full2)] * 3
        + [pl.BlockSpec((C, C), full2), pl.BlockSpec((RW_LANES, RW_LANES), full2)],
        out_specs=pl.BlockSpec((C, D), row),
        out_shape=jax.ShapeDtypeStruct((T, D), BF16),
        scratch_shapes=[pltpu.VMEM((D // RW_LANES, HEAD_DIM, RW_LANES), F32)],
        compiler_params=_cparams("parallel", "arbitrary"),
        name="rwkv_scan",
    )(r, lw, k, v, kkn, b, g, r_k.reshape(1, D), gn_g.reshape(1, D), gn_b.reshape(1, D),
      lincl, _block_ones())


def rwkv_layer(h, mu, w_rkv, w0, w1, w2, a0, a1, a2, g1, g2, k_k, k_a, r_k, gn_g, gn_b, w_out,
               ln_g, ln_b, B, S):
    r, lw, k, v, kkn, b, g = rwkv_prep(h, mu, w_rkv, w0, w1, w2, a0, a1, a2, g1, g2, k_k, k_a, S)
    y = rwkv_scan(r, lw, k, v, kkn, b, g, r_k, gn_g, gn_b, B, S)
    return proj_ln(y, w_out.astype(BF16), h, ln_g, ln_b)


def fox_layer(h, hb, w_in, b_f, w_out, ln_g, ln_b, B, S):
    D = h.shape[1]
    scale = HEAD_DIM ** -0.5
    w_qkv = jnp.concatenate([w_in[:, :D] * scale, w_in[:, D:3 * D]], axis=1).astype(BF16)
    w_f = w_in[:, 3 * D:3 * D + N_HEADS]
    w_og = w_in[:, 3 * D + N_HEADS:].astype(BF16)
    qkv = linear(hb, w_qkv, BF16)
    og = linear(hb, w_og, BF16)
    c = fox_forget_cumsum(h, w_f, b_f, B, S)
    o = fox_attention(qkv, og, c, B, S)
    return proj_ln(o, w_out.astype(BF16), h, ln_g, ln_b)


def sb_layer(h, hb, w_in, w_out, ln_g, ln_b, B, S):
    D = h.shape[1]
    scale = HEAD_DIM ** -0.5
    w_qkv = jnp.concatenate([w_in[:, :D] * scale, w_in[:, D:]], axis=1).astype(BF16)
    qkv = linear(hb, w_qkv, BF16)
    o = stick_breaking_attention(qkv, B, S)
    return proj_ln(o, w_out.astype(BF16), h, ln_g, ln_b)


def gmlp_layer(h, hb, w_in, b_in, gm_ln_g, gm_ln_b, w_s, b_s, w_out, ln_g, ln_b):
    y = gmlp_gate(hb, w_in.astype(BF16), b_in, gm_ln_g, gm_ln_b, w_s, b_s)
    return proj_ln(y, w_out.astype(BF16), h, ln_g, ln_b)


def kernel(x, ln1_g, ln1_b, ln2_g, ln2_b, fox_w_in, fox_b_f, fox_w_out, gm_w_in, gm_b_in, gm_ln_g,
           gm_ln_b, gm_w_s, gm_b_s, gm_w_out, sb_w_in, sb_w_out, rw_mu, rw_w_rkv, rw_w0, rw_w1, rw_w2,
           rw_a0, rw_a1, rw_a2, rw_g1, rw_g2, rw_k_k, rw_k_a, rw_r_k, rw_gn_g, rw_gn_b, rw_w_out,
           router_w, router_b, moe_w_gate, moe_w_up, moe_w_down):
    B, S, D = x.shape
    h = x.reshape(B * S, D)
    hb = h.astype(BF16)
    for i in range(DEPTH):
        kind, j = i % 4, i // 4
        if kind == 0:
            h, hb = fox_layer(h, hb, fox_w_in[j], fox_b_f[j], fox_w_out[j], ln1_g[i], ln1_b[i], B, S)
        elif kind == 1:
            h, hb = gmlp_layer(h, hb, gm_w_in[j], gm_b_in[j], gm_ln_g[j], gm_ln_b[j], gm_w_s[j],
                               gm_b_s[j], gm_w_out[j], ln1_g[i], ln1_b[i])
        elif kind == 2:
            h, hb = sb_layer(h, hb, sb_w_in[j], sb_w_out[j], ln1_g[i], ln1_b[i], B, S)
        else:
            h, hb = rwkv_layer(h, rw_mu[j], rw_w_rkv[j], rw_w0[j], rw_w1[j], rw_w2[j], rw_a0[j],
                               rw_a1[j], rw_a2[j], rw_g1[j], rw_g2[j], rw_k_k[j], rw_k_a[j], rw_r_k[j],
                               rw_gn_g[j], rw_gn_b[j], rw_w_out[j], ln1_g[i], ln1_b[i], B, S)
        h, hb = grouped_moe_ln(h, router_w, router_b, moe_w_gate[i].astype(BF16),
                               moe_w_up[i].astype(BF16), moe_w_down[i].astype(BF16),
                               ln2_g[i], ln2_b[i])
    return h.reshape(B, S, D)
```

```python
import functools

import jax
import jax.numpy as jnp
from jax import lax
from jax.experimental import pallas as pl
from jax.experimental.pallas import tpu as pltpu

F32 = jnp.float32
BF16 = jnp.bfloat16
I32 = jnp.int32

N_HEADS = 16
HEAD_DIM = 64
CHUNK = 128
GMLP_GROUPS = 8
N_EXPERTS = 16
N_GROUPS = 4
EXPERTS_PER_GROUP = 4
PAIRS_PER_GROUP = 6
N_BUCKETS = N_GROUPS * PAIRS_PER_GROUP
N_BUCKETS_PAD = 32
GATE_LANES = 128
DEPTH = 4
ALPHA = (2 * DEPTH) ** 0.25
LN_EPS = 1e-5
GN_EPS = 64e-5

VMEM_LIMIT_BYTES = 48 * 1024 * 1024
MOE_TILE = 256
RW_CHUNK = 64
LANES = 128


def _cparams(*sem):
    return pltpu.CompilerParams(dimension_semantics=sem, vmem_limit_bytes=VMEM_LIMIT_BYTES)


def _layer_norm_rows(y, g, b):
    mu = jnp.mean(y, axis=-1, keepdims=True)
    yc = y - mu
    var = jnp.mean(yc * yc, axis=-1, keepdims=True)
    return yc * lax.rsqrt(var + LN_EPS) * g + b


def _split3(x):
    hi = x.astype(BF16)
    r1 = x - hi.astype(F32)
    mid = r1.astype(BF16)
    lo = (r1 - mid.astype(F32)).astype(BF16)
    return hi, mid, lo


def _dot(a, b):
    return jnp.dot(a, b, preferred_element_type=F32)


def _dot_nt(a, b):
    return lax.dot_general(a, b, (((1,), (1,)), ((), ())), preferred_element_type=F32)


def _linear_kernel(x_ref, w_ref, o_ref):
    o_ref[...] = _dot(x_ref[...], w_ref[...]).astype(o_ref.dtype)


def linear(x, w, out_dtype, tm=512, tn=512):
    M, K = x.shape
    N = w.shape[1]
    tn = min(tn, N)
    return pl.pallas_call(
        _linear_kernel,
        grid=(M // tm, N // tn),
        in_specs=[pl.BlockSpec((tm, K), lambda i, j: (i, 0)),
                  pl.BlockSpec((K, tn), lambda i, j: (0, j))],
        out_specs=pl.BlockSpec((tm, tn), lambda i, j: (i, j)),
        out_shape=jax.ShapeDtypeStruct((M, N), out_dtype),
        compiler_params=_cparams("parallel", "parallel"),
        name="linear",
    )(x, w)


def _proj_ln_kernel(a_ref, w_ref, h_ref, g_ref, b_ref, of_ref, ob_ref):
    mix = _dot(a_ref[...], w_ref[...])
    out = _layer_norm_rows(ALPHA * h_ref[...] + mix, g_ref[...], b_ref[...])
    of_ref[...] = out
    ob_ref[...] = out.astype(BF16)


def proj_ln(a, w, h, g, b, tm=512):
    M, K = a.shape
    D = w.shape[1]
    return pl.pallas_call(
        _proj_ln_kernel,
        grid=(M // tm,),
        in_specs=[pl.BlockSpec((tm, K), lambda i: (i, 0)),
                  pl.BlockSpec((K, D), lambda i: (0, 0)),
                  pl.BlockSpec((tm, D), lambda i: (i, 0)),
                  pl.BlockSpec((1, D), lambda i: (0, 0)),
                  pl.BlockSpec((1, D), lambda i: (0, 0))],
        out_specs=[pl.BlockSpec((tm, D), lambda i: (i, 0)),
                   pl.BlockSpec((tm, D), lambda i: (i, 0))],
        out_shape=[jax.ShapeDtypeStruct((M, D), F32), jax.ShapeDtypeStruct((M, D), BF16)],
        compiler_params=_cparams("parallel"),
        name="proj_ln",
    )(a, w, h, g.reshape(1, D), b.reshape(1, D))


def _router_kernel(h_ref, rwt_ref, rb_ref, su_ref, idx_ref, gate_ref, cnt_ref, base_ref):
    step = pl.program_id(0)
    tr = h_ref.shape[0]

    @pl.when(step == 0)
    def _():
        base_ref[...] = jnp.zeros_like(base_ref)

    logits = lax.dot_general(rwt_ref[...], h_ref[...], (((1,), (1,)), ((), ())),
                             precision=lax.Precision.HIGHEST, preferred_element_type=F32)
    scores = jax.nn.sigmoid(logits)
    sel = scores + rb_ref[...]
    s = [sel[e:e + 1, :] for e in range(N_EXPERTS)]
    sc = [scores[e:e + 1, :] for e in range(N_EXPERTS)]

    def top2sum(v):
        best = v[0] + v[1]
        for a in range(4):
            for b in range(a + 1, 4):
                if (a, b) != (0, 1):
                    best = jnp.maximum(best, v[a] + v[b])
        return best

    gs = [top2sum(s[4 * g:4 * g + 4]) for g in range(N_GROUPS)]
    best, gi = gs[0], jnp.zeros((1, tr), I32)
    for g in range(1, N_GROUPS):
        better = gs[g] > best
        gi = jnp.where(better, g, gi)
        best = jnp.where(better, gs[g], best)

    def pick_group(rows):
        out = []
        for j in range(EXPERTS_PER_GROUP):
            v = rows[j]
            for g in range(1, N_GROUPS):
                v = jnp.where(gi == g, rows[4 * g + j], v)
            out.append(v)
        return out

    v = pick_group(s)
    raw = pick_group(sc)
    m1, l1, r1 = v[0], jnp.zeros((1, tr), I32), raw[0]
    for j in range(1, 4):
        better = v[j] > m1
        l1 = jnp.where(better, j, l1)
        m1 = jnp.where(better, v[j], m1)
        r1 = jnp.where(better, raw[j], r1)
    m2 = jnp.full((1, tr), -jnp.inf, F32)
    l2 = jnp.zeros((1, tr), I32)
    r2 = jnp.zeros((1, tr), F32)
    for j in range(4):
        better = jnp.logical_and(l1 != j, v[j] > m2)
        l2 = jnp.where(better, j, l2)
        m2 = jnp.where(better, v[j], m2)
        r2 = jnp.where(better, raw[j], r2)
    den = r1 + r2
    first_low = l1 < l2
    lo = jnp.where(first_low, l1, l2)
    hi = jnp.where(first_low, l2, l1)
    g_lo = jnp.where(first_low, r1, r2) / den
    g_hi = jnp.where(first_low, r2, r1) / den
    gate_ref[...] = jnp.concatenate([g_lo, g_hi], axis=0)
    bucket = gi * PAIRS_PER_GROUP + 2 * lo + hi - 1 - jnp.where(lo == 2, 1, 0)

    bidx = lax.broadcasted_iota(I32, (N_BUCKETS_PAD, tr), 0)
    hit = bidx == bucket
    oh = jnp.where(hit, 1.0, 0.0)
    rank = _dot(oh.astype(BF16), su_ref[...]) + base_ref[...]
    rk = jnp.sum(jnp.where(hit, rank, 0.0), axis=0, keepdims=True).astype(I32)
    idx_ref[...] = jnp.concatenate([bucket, rk], axis=0)
    base_ref[...] = base_ref[...] + jnp.sum(oh, axis=1, keepdims=True)
    cnt_ref[...] = jnp.broadcast_to(base_ref[...], cnt_ref.shape)


def moe_router(h, router_w, router_b, tr=512):
    T, D = h.shape
    su = jnp.triu(jnp.ones((tr, tr), BF16), k=1)
    idx, gate, cnt = pl.pallas_call(
        _router_kernel,
        grid=(T // tr,),
        in_specs=[pl.BlockSpec((tr, D), lambda i: (i, 0)),
                  pl.BlockSpec((N_EXPERTS, D), lambda i: (0, 0)),
                  pl.BlockSpec((N_EXPERTS, 1), lambda i: (0, 0)),
                  pl.BlockSpec((tr, tr), lambda i: (0, 0))],
        out_specs=[pl.BlockSpec((2, tr), lambda i: (0, i)),
                   pl.BlockSpec((2, tr), lambda i: (0, i)),
                   pl.BlockSpec((N_BUCKETS_PAD, LANES), lambda i: (0, 0))],
        out_shape=[jax.ShapeDtypeStruct((2, T), I32), jax.ShapeDtypeStruct((2, T), F32),
                   jax.ShapeDtypeStruct((N_BUCKETS_PAD, LANES), F32)],
        scratch_shapes=[pltpu.VMEM((N_BUCKETS_PAD, 1), F32)],
        compiler_params=_cparams("arbitrary"),
        name="moe_router",
    )(h, router_w.T, router_b.reshape(N_EXPERTS, 1), su)
    return idx, gate, cnt[:N_BUCKETS, 0].astype(I32)


def _dispatch_kernel(pos_ref, h_ref, gate_ref, z_ref, xs_ref, aug, sems):
    del z_ref
    i, n = pl.program_id(0), pl.num_programs(0)
    td, D = h_ref.shape
    slot = i % 2
    stage = aug.at[slot]
    stage[:, :D] = h_ref[...]
    stage[:, D:] = gate_ref[...]

    def row_copy(s, r, dst_row):
        return pltpu.make_async_copy(aug.at[s, pl.ds(r, 1)], xs_ref.at[pl.ds(dst_row, 1)], sems.at[s])

    def issue(r, c):
        row_copy(slot, r, pos_ref[r]).start()
        return c

    lax.fori_loop(0, td, issue, 0, unroll=8)

    def drain(s):
        def body(r, c):
            row_copy(s, 0, 0).wait()
            return c
        lax.fori_loop(0, td, body, 0, unroll=8)

    pl.when(i > 0)(lambda: drain(1 - slot))
    pl.when(i == n - 1)(lambda: drain(slot))


def moe_dispatch(h, pos, gcols, n_rows, td=256):
    T, D = h.shape
    W = D + GATE_LANES
    zeros = jnp.zeros((n_rows, W), h.dtype)
    return pl.pallas_call(
        _dispatch_kernel,
        grid=(T // td,),
        in_specs=[pl.BlockSpec((td,), lambda i: (i,), memory_space=pltpu.SMEM),
                  pl.BlockSpec((td, D), lambda i: (i, 0)),
                  pl.BlockSpec((td, GATE_LANES), lambda i: (i, 0)),
                  pl.BlockSpec(memory_space=pl.ANY)],
        out_specs=pl.BlockSpec(memory_space=pl.ANY),
        out_shape=jax.ShapeDtypeStruct((n_rows, W), h.dtype),
        scratch_shapes=[pltpu.VMEM((2, td, W), h.dtype), pltpu.SemaphoreType.DMA((2,))],
        input_output_aliases={3: 0},
        compiler_params=_cparams("arbitrary"),
        name="moe_dispatch",
    )(pos, h, gcols, zeros)


def _expert_kernel(ta_ref, tb_ref, nu_ref, x_ref, wga, wua, wda, wgb, wub, wdb, o_ref):
    del ta_ref, tb_ref
    D = o_ref.shape[1]

    @pl.when(pl.program_id(0) < nu_ref[0])
    def _():
        xa = x_ref[...]
        x = xa[:, :D].astype(BF16)

        def ffn(wg, wu, wd):
            he = (jax.nn.silu(_dot(x, wg[0])) * _dot(x, wu[0])).astype(BF16)
            return _dot(he, wd[0])

        o_ref[...] = xa[:, D:D + 1] * ffn(wga, wua, wda) + xa[:, D + 1:D + 2] * ffn(wgb, wub, wdb)

    @pl.when(pl.program_id(0) >= nu_ref[0])
    def _():
        o_ref[...] = jnp.zeros_like(o_ref)


def moe_experts(xs, tile_ea, tile_eb, n_used, w_gate, w_up, w_down):
    n_rows, W = xs.shape
    E, D, De = w_gate.shape
    n_tiles = n_rows // MOE_TILE

    def row_map(i, ta, tb, nu):
        return (jnp.minimum(i, nu[0] - 1), 0)

    wa = lambda i, ta, tb, nu: (ta[i], 0, 0)
    wb = lambda i, ta, tb, nu: (tb[i], 0, 0)
    grid_spec = pltpu.PrefetchScalarGridSpec(
        num_scalar_prefetch=3,
        grid=(n_tiles,),
        in_specs=[pl.BlockSpec((MOE_TILE, W), row_map),
                  pl.BlockSpec((1, D, De), wa), pl.BlockSpec((1, D, De), wa), pl.BlockSpec((1, De, D), wa),
                  pl.BlockSpec((1, D, De), wb), pl.BlockSpec((1, D, De), wb), pl.BlockSpec((1, De, D), wb)],
        out_specs=pl.BlockSpec((MOE_TILE, D), lambda i, ta, tb, nu: (i, 0)),
    )
    return pl.pallas_call(
        _expert_kernel,
        grid_spec=grid_spec,
        out_shape=jax.ShapeDtypeStruct((n_rows, D), F32),
        compiler_params=_cparams("arbitrary"),
        name="moe_experts",
    )(tile_ea, tile_eb, n_used, xs, w_gate, w_up, w_down, w_gate, w_up, w_down)


def _combine_kernel(pos_ref, posn_ref, h_ref, lg_ref, lb_ref, os_ref, of_ref, ob_ref, buf, sems):
    i, n = pl.program_id(0), pl.num_programs(0)
    tc = h_ref.shape[0]
    slot = i % 2

    def row_copy(s, src_row, r):
        return pltpu.make_async_copy(os_ref.at[pl.ds(src_row, 1)], buf.at[s, pl.ds(r, 1)], sems.at[s])

    def fetch(p_ref, s):
        def body(r, c):
            row_copy(s, p_ref[r], r).start()
            return c
        lax.fori_loop(0, tc, body, 0, unroll=8)

    pl.when(i == 0)(lambda: fetch(pos_ref, slot))
    pl.when(i + 1 < n)(lambda: fetch(posn_ref, 1 - slot))

    def drain(r, c):
        row_copy(slot, 0, 0).wait()
        return c

    lax.fori_loop(0, tc, drain, 0, unroll=8)
    out = _layer_norm_rows(ALPHA * h_ref[...] + buf[slot], lg_ref[...], lb_ref[...])
    of_ref[...] = out
    ob_ref[...] = out.astype(BF16)


def moe_combine(osorted, pos, h, ln_g, ln_b, tc=256):
    T, D = h.shape
    nt = T // tc
    return pl.pallas_call(
        _combine_kernel,
        grid=(nt,),
        in_specs=[pl.BlockSpec((tc,), lambda i: (i,), memory_space=pltpu.SMEM),
                  pl.BlockSpec((tc,), lambda i: (jnp.minimum(i + 1, nt - 1),), memory_space=pltpu.SMEM),
                  pl.BlockSpec((tc, D), lambda i: (i, 0)),
                  pl.BlockSpec((1, D), lambda i: (0, 0)),
                  pl.BlockSpec((1, D), lambda i: (0, 0)),
                  pl.BlockSpec(memory_space=pl.ANY)],
        out_specs=[pl.BlockSpec((tc, D), lambda i: (i, 0)),
                   pl.BlockSpec((tc, D), lambda i: (i, 0))],
        out_shape=[jax.ShapeDtypeStruct((T, D), F32), jax.ShapeDtypeStruct((T, D), BF16)],
        scratch_shapes=[pltpu.VMEM((2, tc, D), F32), pltpu.SemaphoreType.DMA((2,))],
        compiler_params=_cparams("arbitrary"),
        name="moe_combine",
    )(pos, pos, h, ln_g.reshape(1, D), ln_b.reshape(1, D), osorted)


_PAIRS = ((0, 1), (0, 2), (0, 3), (1, 2), (1, 3), (2, 3))


def grouped_moe_ln(h, router_w, router_b, w_gate, w_up, w_down, ln_g, ln_b):
    T, D = h.shape
    idx, gate, cnt = moe_router(h, router_w, router_b)
    bucket, rank = idx[0], idx[1]
    tiles = (cnt + MOE_TILE - 1) // MOE_TILE
    tile_end = jnp.cumsum(tiles)
    start = (tile_end - tiles) * MOE_TILE
    pos = start[bucket] + rank
    n_tiles = T // MOE_TILE + N_BUCKETS
    tile_ids = jnp.arange(n_tiles, dtype=I32)
    tile_bucket = jnp.minimum(
        jnp.sum((tile_ids[:, None] >= tile_end[None, :]).astype(I32), axis=1), N_BUCKETS - 1)
    ea = jnp.array([g * EXPERTS_PER_GROUP + a for g in range(N_GROUPS) for a, _ in _PAIRS], I32)
    eb = jnp.array([g * EXPERTS_PER_GROUP + b for g in range(N_GROUPS) for _, b in _PAIRS], I32)
    n_used = tile_end[-1:].astype(I32)
    gcols = jnp.pad(gate.T, ((0, 0), (0, GATE_LANES - 2)))
    xs = moe_dispatch(h, pos, gcols, n_tiles * MOE_TILE)
    osorted = moe_experts(xs, ea[tile_bucket], eb[tile_bucket], n_used, w_gate, w_up, w_down)
    return moe_combine(osorted, pos, h, ln_g, ln_b)


def _forget_kernel(h_ref, wft_ref, bf_ref, ui_ref, c_ref, carry_ref):
    @pl.when(pl.program_id(1) == 0)
    def _():
        carry_ref[...] = jnp.zeros_like(carry_ref)

    logits = lax.dot_general(wft_ref[...], h_ref[...], (((1,), (1,)), ((), ())),
                             precision=lax.Precision.HIGHEST, preferred_element_type=F32)
    log_f = jax.nn.log_sigmoid(logits + bf_ref[...])
    hi, mid, lo = _split3(log_f)
    ui = ui_ref[...]
    cum = _dot(hi, ui) + _dot(mid, ui) + _dot(lo, ui) + carry_ref[...]
    c_ref[0] = cum
    carry_ref[...] = cum[:, -1:]


def fox_forget_cumsum(h, w_f, b_f, B, S, ts=512):
    T, D = h.shape
    H = w_f.shape[1]
    ui = jnp.triu(jnp.ones((ts, ts), BF16))
    ns = S // ts
    return pl.pallas_call(
        _forget_kernel,
        grid=(B, ns),
        in_specs=[pl.BlockSpec((ts, D), lambda b, i: (b * ns + i, 0)),
                  pl.BlockSpec((H, D), lambda b, i: (0, 0)),
                  pl.BlockSpec((H, 1), lambda b, i: (0, 0)),
                  pl.BlockSpec((ts, ts), lambda b, i: (0, 0))],
        out_specs=pl.BlockSpec((1, H, ts), lambda b, i: (b, 0, i)),
        out_shape=jax.ShapeDtypeStruct((B, H, S), F32),
        scratch_shapes=[pltpu.VMEM((H, 1), F32)],
        compiler_params=_cparams("parallel", "arbitrary"),
        name="fox_forget",
    )(h, w_f.T, b_f.reshape(H, 1), ui)


def _head_pair_masks(q):
    lane = lax.broadcasted_iota(I32, q.shape, 1)
    zero = jnp.zeros_like(q)
    return jnp.where(lane < HEAD_DIM, q, zero), jnp.where(lane >= HEAD_DIM, q, zero)


def _lane_tile(x, n):
    return x if n == 1 else jnp.concatenate([x] * n, axis=1)


def _causal_steps(nq, ratio, descending):
    qi, kb = [], []
    for i in range(nq):
        ks = list(range(ratio * (i + 1)))
        for k in (reversed(ks) if descending else ks):
            qi.append(i)
            kb.append(k)
    return jnp.array(qi, I32), jnp.array(kb, I32)


def _fox_kernel(qi_ref, kb_ref, q_ref, k_ref, v_ref, crow_ref, ccol_ref, og_ref, o_ref,
                qa, qb, ct, m, acc):
    hp, step_id = pl.program_id(1), pl.program_id(2)
    i, j = qi_ref[step_id], kb_ref[step_id]
    tq, tk = q_ref.shape[1], k_ref.shape[1]

    @pl.when(j == 0)
    def _():
        qa[...], qb[...] = _head_pair_masks(q_ref[0])
        cc = ccol_ref[0]
        head = lax.broadcasted_iota(I32, cc.shape, 1)
        for x in range(2):
            col = jnp.sum(jnp.where(head == 2 * hp + x, cc, 0.0), axis=1, keepdims=True)
            ct[x] = jnp.broadcast_to(col, (tq, LANES))
        m[...] = jnp.full(m.shape, -jnp.inf, F32)
        acc[...] = jnp.zeros_like(acc)

    def step(masked):
        k = k_ref[0]
        v_ones = jnp.concatenate([v_ref[0], jnp.ones((tk, LANES), BF16)], axis=1)
        crow = crow_ref[0, 0]
        if masked:
            row = lax.broadcasted_iota(I32, (tq, tk), 0)
            col = lax.broadcasted_iota(I32, (tq, tk), 1)
            keep = col <= row
        for x, qx in enumerate((qa, qb)):
            s = _dot_nt(qx[...], k) - crow[x:x + 1, :]
            if masked:
                s = jnp.where(keep, s, -jnp.inf)
            c_t = ct[x]
            m_prev = m[x]
            m_new = jnp.maximum(m_prev, jnp.max(s, axis=1, keepdims=True) + c_t)
            alpha = jnp.exp(m_prev - m_new)
            p = jnp.exp(s - _lane_tile(m_new - c_t, tk // LANES))
            acc[x] = _lane_tile(alpha, 2) * acc[x] + _dot(p.astype(BF16), v_ones)
            m[x] = m_new

    pl.when(j < i)(lambda: step(False))
    pl.when(j == i)(lambda: step(True))

    @pl.when(j == i)
    def _():
        lane = lax.broadcasted_iota(I32, (tq, LANES), 1)
        a0, a1 = acc[0], acc[1]
        o = jnp.where(lane < HEAD_DIM, a0[:, :LANES] / a0[:, LANES:], a1[:, :LANES] / a1[:, LANES:])
        o_ref[0] = (o * jax.nn.sigmoid(og_ref[0].astype(F32))).astype(BF16)


def fox_attention(qkv, og, c, B, S, tq=512):
    T = qkv.shape[0]
    D = N_HEADS * HEAD_DIM
    hp_n = N_HEADS // 2
    nq = S // tq
    qkv3 = qkv.reshape(B, S, 3 * D)
    crow = c.reshape(B, hp_n, 2, S)
    ccol = jnp.transpose(c, (0, 2, 1))
    qi, kb = _causal_steps(nq, 1, descending=False)
    q_blk = lambda b, hp, s, qi, kb: (b, qi[s], hp)
    kv_blk = lambda off: (lambda b, hp, s, qi, kb: (b, kb[s], off + hp))
    grid_spec = pltpu.PrefetchScalarGridSpec(
        num_scalar_prefetch=2,
        grid=(B, hp_n, qi.shape[0]),
        in_specs=[pl.BlockSpec((1, tq, 128), q_blk),
                  pl.BlockSpec((1, tq, 128), kv_blk(hp_n)),
                  pl.BlockSpec((1, tq, 128), kv_blk(2 * hp_n)),
                  pl.BlockSpec((1, 1, 2, tq), lambda b, hp, s, qi, kb: (b, hp, 0, kb[s])),
                  pl.BlockSpec((1, tq, N_HEADS), lambda b, hp, s, qi, kb: (b, qi[s], 0)),
                  pl.BlockSpec((1, tq, 128), q_blk)],
        out_specs=pl.BlockSpec((1, tq, 128), q_blk),
        scratch_shapes=[pltpu.VMEM((tq, 128), BF16), pltpu.VMEM((tq, 128), BF16),
                        pltpu.VMEM((2, tq, LANES), F32), pltpu.VMEM((2, tq, LANES), F32),
                        pltpu.VMEM((2, tq, 2 * LANES), F32)],
    )
    out = pl.pallas_call(
        _fox_kernel,
        grid_spec=grid_spec,
        out_shape=jax.ShapeDtypeStruct((B, S, D), BF16),
        compiler_params=_cparams("parallel", "parallel", "arbitrary"),
        name="fox_attention",
    )(qi, kb, qkv3, qkv3, qkv3, crow, ccol, og.reshape(B, S, D))
    return out.reshape(T, D)


def _sb_kernel(qi_ref, kb_ref, q_ref, k_ref, v_ref, u_ref, o_ref, qa, qb, rs, acc):
    step_id = pl.program_id(2)
    i, kb = qi_ref[step_id], kb_ref[step_id]
    tq, tk = q_ref.shape[1], k_ref.shape[1]
    ratio = tq // tk

    @pl.when(kb == ratio * (i + 1) - 1)
    def _():
        qa[...], qb[...] = _head_pair_masks(q_ref[0])
        rs[...] = jnp.zeros_like(rs)
        acc[...] = jnp.zeros_like(acc)

    def step(masked):
        k = k_ref[0]
        v = v_ref[0]
        u = u_ref[...]
        if masked:
            row = lax.broadcasted_iota(I32, (tq, tk), 0)
            col = lax.broadcasted_iota(I32, (tq, tk), 1)
            keep = col - row < i * tq - kb * tk
        for x, qx in enumerate((qa, qb)):
            z = _dot_nt(qx[...], k)
            soft = jnp.log(1.0 + jnp.exp(-jnp.abs(z)))
            log_beta = jnp.minimum(z, 0.0) - soft
            log_not = log_beta - z
            if masked:
                log_not = jnp.where(keep, log_not, 0.0)
            hi = log_not.astype(BF16)
            lo = (log_not - hi.astype(F32)).astype(BF16)
            log_rest = _dot(hi, u) + _dot(lo, u) + _lane_tile(rs[x], tk // LANES)
            a = jnp.exp(log_beta + log_rest)
            if masked:
                a = jnp.where(keep, a, 0.0)
            acc[x] = acc[x] + _dot(a.astype(BF16), v)
            rs[x] = rs[x] + jnp.sum(log_not, axis=1, keepdims=True)

    pl.when(kb >= ratio * i)(lambda: step(True))
    pl.when(kb < ratio * i)(lambda: step(False))

    @pl.when(kb == 0)
    def _():
        lane = lax.broadcasted_iota(I32, (tq, 2 * HEAD_DIM), 1)
        o_ref[0] = jnp.where(lane < HEAD_DIM, acc[0], acc[1]).astype(BF16)


def stick_breaking_attention(qkv, B, S, tq=512, tk=256):
    T = qkv.shape[0]
    D = N_HEADS * HEAD_DIM
    hp_n = N_HEADS // 2
    nq = S // tq
    qkv3 = qkv.reshape(B, S, 3 * D)
    u = jnp.tril(jnp.ones((tk, tk), BF16), k=-1)
    qi, kb = _causal_steps(nq, tq // tk, descending=True)
    q_blk = lambda b, hp, s, qi, kb: (b, qi[s], hp)
    kv_blk = lambda off: (lambda b, hp, s, qi, kb: (b, kb[s], off + hp))
    grid_spec = pltpu.PrefetchScalarGridSpec(
        num_scalar_prefetch=2,
        grid=(B, hp_n, qi.shape[0]),
        in_specs=[pl.BlockSpec((1, tq, 128), q_blk),
                  pl.BlockSpec((1, tk, 128), kv_blk(hp_n)),
                  pl.BlockSpec((1, tk, 128), kv_blk(2 * hp_n)),
                  pl.BlockSpec((tk, tk), lambda b, hp, s, qi, kb: (0, 0))],
        out_specs=pl.BlockSpec((1, tq, 128), q_blk),
        scratch_shapes=[pltpu.VMEM((tq, 128), BF16), pltpu.VMEM((tq, 128), BF16),
                        pltpu.VMEM((2, tq, LANES), F32), pltpu.VMEM((2, tq, LANES), F32)],
    )
    out = pl.pallas_call(
        _sb_kernel,
        grid_spec=grid_spec,
        out_shape=jax.ShapeDtypeStruct((B, S, D), BF16),
        compiler_params=_cparams("parallel", "parallel", "arbitrary"),
        name="stick_breaking",
    )(qi, kb, qkv3, qkv3, qkv3, u)
    return out.reshape(T, D)


def _gmlp_kernel(h_ref, win_ref, bin_ref, lg_ref, lb_ref, ws_ref, bst_ref, o_ref):
    tm = h_ref.shape[0]
    dg = o_ref.shape[1]
    z = jax.nn.gelu(_dot(h_ref[...], win_ref[...]) + bin_ref[...])
    u = z[:, :dg]
    v = _layer_norm_rows(z[:, dg:], lg_ref[...], lb_ref[...]).astype(BF16)
    row = lax.broadcasted_iota(I32, (CHUNK, CHUNK), 0)
    col = lax.broadcasted_iota(I32, (CHUNK, CHUNK), 1)
    gw = dg // GMLP_GROUPS
    for g in range(GMLP_GROUPS):
        wc = jnp.where(col <= row, ws_ref[g], 0.0).astype(BF16)
        cs = slice(g * gw, (g + 1) * gw)
        for c in range(tm // CHUNK):
            rs = slice(c * CHUNK, (c + 1) * CHUNK)
            sv = _dot(wc, v[rs, cs]) + bst_ref[:, g:g + 1]
            o_ref[rs, cs] = (u[rs, cs] * sv).astype(BF16)


def gmlp_gate(hb, w_in, b_in, ln_g, ln_b, w_s, b_s, tm=256):
    T, D = hb.shape
    dg = w_in.shape[1] // 2
    return pl.pallas_call(
        _gmlp_kernel,
        grid=(T // tm,),
        in_specs=[pl.BlockSpec((tm, D), lambda i: (i, 0)),
                  pl.BlockSpec((D, 2 * dg), lambda i: (0, 0)),
                  pl.BlockSpec((1, 2 * dg), lambda i: (0, 0)),
                  pl.BlockSpec((1, dg), lambda i: (0, 0)),
                  pl.BlockSpec((1, dg), lambda i: (0, 0)),
                  pl.BlockSpec((GMLP_GROUPS, CHUNK, CHUNK), lambda i: (0, 0, 0)),
                  pl.BlockSpec((CHUNK, GMLP_GROUPS), lambda i: (0, 0))],
        out_specs=pl.BlockSpec((tm, dg), lambda i: (i, 0)),
        out_shape=jax.ShapeDtypeStruct((T, dg), BF16),
        compiler_params=_cparams("parallel"),
        name="gmlp_gate",
    )(hb, w_in, b_in.reshape(1, -1), ln_g.reshape(1, -1), ln_b.reshape(1, -1), w_s, b_s.T)


RW_LANES = 4 * HEAD_DIM


def _seg_sum(x, bones):
    hi = x.astype(BF16)
    lo = (x - hi.astype(F32)).astype(BF16)
    return _dot(hi, bones) + _dot(lo, bones)


def _rwkv_prep_kernel(seq_len, h_ref, hp_ref, mu_ref, wrkv_ref, w1_ref, w2_ref, a1_ref, a2_ref,
                      g1_ref, g2_ref, w0_ref, a0_ref, kk_ref, ka_ref, bones_ref, lchunk_ref,
                      r_ref, cl_ref, clx_ref, k_ref, v_ref, kkn_ref, b_ref, g_ref):
    i = pl.program_id(0)
    tm, D = h_ref.shape
    h = h_ref[...]
    at_start = (i * tm) % seq_len == 0
    prev_row = jnp.where(at_start, 0.0, hp_ref[7:8, :])
    row = lax.broadcasted_iota(I32, (tm, D), 0)
    h_prev = jnp.where(row == 0, prev_row, pltpu.roll(h, 1, axis=0))
    dx = h_prev - h

    def mixed(c):
        return (h + dx * mu_ref[c:c + 1, :]).astype(BF16)

    r = _dot(mixed(0), wrkv_ref[0])
    k = _dot(mixed(1), wrkv_ref[1])
    v = _dot(mixed(2), wrkv_ref[2])
    d = w0_ref[...] + _dot(jnp.tanh(_dot(mixed(3), w1_ref[...])).astype(BF16), w2_ref[...])
    lw = -jnp.exp(-jax.nn.softplus(-d) - 0.5)
    a = jax.nn.sigmoid(a0_ref[...] + _dot(_dot(mixed(4), a1_ref[...]).astype(BF16), a2_ref[...]))
    g = _dot(jax.nn.sigmoid(_dot(mixed(5), g1_ref[...])).astype(BF16), g2_ref[...])
    kk = k * kk_ref[...]
    bones = bones_ref[...]
    norm_sq = jnp.concatenate(
        [_seg_sum(jnp.square(kk[:, q * RW_LANES:(q + 1) * RW_LANES]), bones) for q in range(D // RW_LANES)],
        axis=1)
    kkn = kk / jnp.maximum(jnp.sqrt(norm_sq), 1e-12)
    hi, mid, lo = _split3(lw)
    lchunk = lchunk_ref[...]
    cl = _dot(lchunk, hi) + _dot(lchunk, mid) + _dot(lchunk, lo)
    r_ref[...] = r
    cl_ref[...] = cl
    clx_ref[...] = cl - lw
    k_ref[...] = k * (1.0 + (a - 1.0) * ka_ref[...])
    v_ref[...] = v
    kkn_ref[...] = kkn
    b_ref[...] = kkn * a
    g_ref[...] = g


def _block_ones():
    idx = jnp.arange(RW_LANES) // HEAD_DIM
    return (idx[:, None] == idx[None, :]).astype(BF16)


def rwkv_prep(h, mu, w_rkv, w0, w1, w2, a0, a1, a2, g1, g2, k_k, k_a, S, tm=256):
    T, D = h.shape
    row = lambda i: (i, 0)
    full2 = lambda i: (0, 0)
    vec = lambda x: x.reshape(1, D)
    bf = lambda x: x.astype(BF16)
    pos = jnp.arange(tm)
    lchunk = jnp.logical_and(pos[:, None] // RW_CHUNK == pos[None, :] // RW_CHUNK,
                             pos[None, :] <= pos[:, None]).astype(BF16)
    outs = pl.pallas_call(
        functools.partial(_rwkv_prep_kernel, S),
        grid=(T // tm,),
        in_specs=[pl.BlockSpec((tm, D), row),
                  pl.BlockSpec((8, D), lambda i: (jnp.maximum(i * (tm // 8) - 1, 0), 0)),
                  pl.BlockSpec((6, D), full2),
                  pl.BlockSpec((3, D, D), lambda i: (0, 0, 0)),
                  pl.BlockSpec(w1.shape, full2), pl.BlockSpec(w2.shape, full2),
                  pl.BlockSpec(a1.shape, full2), pl.BlockSpec(a2.shape, full2),
                  pl.BlockSpec(g1.shape, full2), pl.BlockSpec(g2.shape, full2),
                  pl.BlockSpec((1, D), full2), pl.BlockSpec((1, D), full2),
                  pl.BlockSpec((1, D), full2), pl.BlockSpec((1, D), full2),
                  pl.BlockSpec((RW_LANES, RW_LANES), full2),
                  pl.BlockSpec((tm, tm), full2)],
        out_specs=[pl.BlockSpec((tm, D), row)] * 8,
        out_shape=[jax.ShapeDtypeStruct((T, D), F32)] * 8,
        compiler_params=_cparams("parallel"),
        name="rwkv_prep",
    )(h, h, mu, bf(w_rkv), bf(w1), bf(w2), bf(a1), bf(a2), bf(g1), bf(g2),
      vec(w0), vec(a0), vec(k_k), vec(k_a), _block_ones(), lchunk)
    return outs


def _rwkv_chunk(r, cl, clx, k, v, kk, b, P, cst):
    C = RW_CHUNK
    mbd, eye, strict, incl, levels = cst
    eye_f = jnp.where(eye, 1.0, 0.0)
    eye_b = eye_f.astype(BF16)

    def bd(y):
        return jnp.where(mbd, jnp.concatenate([y.astype(BF16)] * 4, axis=0), jnp.zeros((), BF16))

    def rows(*parts):
        return jnp.concatenate([p.astype(BF16) for p in parts], axis=0)

    cl_last = cl[C - 1:C, :]
    kkt = kk * jnp.exp(clx)
    rt = r * jnp.exp(cl)
    g_inv = jnp.exp(-cl)
    g_tail = jnp.exp(cl_last - cl)
    kk_r = rows(kkt, rt)
    ab_k = _dot_nt(kk_r, bd(k * g_inv))
    ab_b = _dot_nt(kk_r, bd(b * g_inv))
    a_k = jnp.where(strict, ab_k[:C], 0.0)
    b_k = jnp.where(incl, ab_k[C:], 0.0)
    a_b = jnp.where(strict, ab_b[:C], 0.0)
    b_b = jnp.where(incl, ab_b[C:], 0.0)
    kh_t = _dot_nt(eye_b, bd(k * g_tail))
    bh_t = _dot_nt(eye_b, bd(b * g_tail))

    dm = eye_f - jnp.where(levels[0], a_b, 0.0)
    for lvl in levels[1:]:
        w = _dot(dm.astype(BF16), bd(jnp.where(lvl, a_b, 0.0)))
        dm = dm - _dot(w.astype(BF16), bd(dm))
    tm_b = dm.astype(BF16)

    xv = _dot(rows(a_k, b_k, kh_t), bd(v))
    kp = _dot(tm_b, bd(kkt))
    vp = _dot(tm_b, bd(xv[:C]))
    bb_bh = rows(b_b, bh_t)
    xkp = _dot(bb_bh, bd(kp))
    xvp = _dot(bb_bh, bd(vp))
    y_loc = xv[C:2 * C] - xvp[:C]
    r_eff = rt - xkp[:C]
    m = eye_f * jnp.exp(cl_last) - xkp[C:]
    n = xv[2 * C:] - xvp[C:]

    p_hi = P.astype(BF16)
    m_hi = m.astype(BF16)
    z_hi = _dot(rows(r_eff, m_hi, m - m_hi.astype(F32)), bd(p_hi))
    z_lo = _dot(rows(r_eff, m_hi), bd(P - p_hi.astype(F32)))
    y = z_hi[:C] + z_lo[:C] + y_loc
    p_new = z_hi[C:2 * C] + z_hi[2 * C:] + z_lo[C:] + n
    return y, p_new


def _rwkv_scan_kernel(r_ref, cl_ref, clx_ref, k_ref, v_ref, kk_ref, b_ref, g_ref, rk_ref, gg_ref, gb_ref,
                      bones_ref, o_ref, state):
    C = RW_CHUNK

    @pl.when(pl.program_id(1) == 0)
    def _():
        state[...] = jnp.zeros_like(state)

    t = lax.broadcasted_iota(I32, (C, RW_LANES), 0)
    i = lax.broadcasted_iota(I32, (C, RW_LANES), 1) % C
    levels = []
    for sh in range(6):
        tb, ib = t >> sh, i >> sh
        levels.append(jnp.logical_and(tb == ib + 1, (tb & 1) == 1))
    rr = lax.broadcasted_iota(I32, (RW_LANES, RW_LANES), 0) // HEAD_DIM
    cc = lax.broadcasted_iota(I32, (RW_LANES, RW_LANES), 1) // HEAD_DIM
    cst = (rr == cc, i == t, i < t, i <= t, levels)
    bones = bones_ref[...]

    for q in range(r_ref.shape[1] // RW_LANES):
        sl = slice(q * RW_LANES, (q + 1) * RW_LANES)
        r, k, v = r_ref[:, sl], k_ref[:, sl], v_ref[:, sl]
        y, p_new = _rwkv_chunk(r, cl_ref[:, sl], clx_ref[:, sl], k, v, kk_ref[:, sl], b_ref[:, sl],
                               state[q], cst)
        state[q] = p_new
        mean = _seg_sum(y, bones) * (1.0 / HEAD_DIM)
        yc = y - mean
        var = _seg_sum(yc * yc, bones) * (1.0 / HEAD_DIM)
        yn = yc * lax.rsqrt(var + GN_EPS) * gg_ref[:, sl] + gb_ref[:, sl]
        bonus = _seg_sum(r * k * rk_ref[:, sl], bones) * v
        o_ref[:, sl] = ((yn + bonus) * g_ref[:, sl]).astype(BF16)


def rwkv_scan(r, cl, clx, k, v, kkn, b, g, r_k, gn_g, gn_b, B, S):
    T, D = r.shape
    C = RW_CHUNK
    nc = S // C
    row = lambda bi, ci: (bi * nc + ci, 0)
    full2 = lambda bi, ci: (0, 0)
    return pl.pallas_call(
        _rwkv_scan_kernel,
        grid=(B, nc),
        in_specs=[pl.BlockSpec((C, D), row)] * 8 + [pl.BlockSpec((1, D), full2)] * 3
        + [pl.BlockSpec((RW_LANES, RW_LANES), full2)],
        out_specs=pl.BlockSpec((C, D), row),
        out_shape=jax.ShapeDtypeStruct((T, D), BF16),
        scratch_shapes=[pltpu.VMEM((D // RW_LANES, HEAD_DIM, RW_LANES), F32)],
        compiler_params=_cparams("parallel", "arbitrary"),
        name="rwkv_scan",
    )(r, cl, clx, k, v, kkn, b, g, r_k.reshape(1, D), gn_g.reshape(1, D), gn_b.reshape(1, D),
      _block_ones())


def rwkv_layer(h, mu, w_rkv, w0, w1, w2, a0, a1, a2, g1, g2, k_k, k_a, r_k, gn_g, gn_b, w_out,
               ln_g, ln_b, B, S):
    r, cl, clx, k, v, kkn, b, g = rwkv_prep(h, mu, w_rkv, w0, w1, w2, a0, a1, a2, g1, g2, k_k, k_a, S)
    y = rwkv_scan(r, cl, clx, k, v, kkn, b, g, r_k, gn_g, gn_b, B, S)
    return proj_ln(y, w_out.astype(BF16), h, ln_g, ln_b)


def fox_layer(h, hb, w_in, b_f, w_out, ln_g, ln_b, B, S):
    D = h.shape[1]
    scale = HEAD_DIM ** -0.5
    w_qkv = jnp.concatenate([w_in[:, :D] * scale, w_in[:, D:3 * D]], axis=1).astype(BF16)
    w_f = w_in[:, 3 * D:3 * D + N_HEADS]
    w_og = w_in[:, 3 * D + N_HEADS:].astype(BF16)
    qkv = linear(hb, w_qkv, BF16)
    og = linear(hb, w_og, BF16)
    c = fox_forget_cumsum(h, w_f, b_f, B, S)
    o = fox_attention(qkv, og, c, B, S)
    return proj_ln(o, w_out.astype(BF16), h, ln_g, ln_b)


def sb_layer(h, hb, w_in, w_out, ln_g, ln_b, B, S):
    D = h.shape[1]
    scale = HEAD_DIM ** -0.5
    w_qkv = jnp.concatenate([w_in[:, :D] * scale, w_in[:, D:]], axis=1).astype(BF16)
    qkv = linear(hb, w_qkv, BF16)
    o = stick_breaking_attention(qkv, B, S)
    return proj_ln(o, w_out.astype(BF16), h, ln_g, ln_b)


def gmlp_layer(h, hb, w_in, b_in, gm_ln_g, gm_ln_b, w_s, b_s, w_out, ln_g, ln_b):
    y = gmlp_gate(hb, w_in.astype(BF16), b_in, gm_ln_g, gm_ln_b, w_s, b_s)
    return proj_ln(y, w_out.astype(BF16), h, ln_g, ln_b)


def kernel(x, ln1_g, ln1_b, ln2_g, ln2_b, fox_w_in, fox_b_f, fox_w_out, gm_w_in, gm_b_in, gm_ln_g,
           gm_ln_b, gm_w_s, gm_b_s, gm_w_out, sb_w_in, sb_w_out, rw_mu, rw_w_rkv, rw_w0, rw_w1, rw_w2,
           rw_a0, rw_a1, rw_a2, rw_g1, rw_g2, rw_k_k, rw_k_a, rw_r_k, rw_gn_g, rw_gn_b, rw_w_out,
           router_w, router_b, moe_w_gate, moe_w_up, moe_w_down):
    B, S, D = x.shape
    h = x.reshape(B * S, D)
    hb = h.astype(BF16)
    for i in range(DEPTH):
        kind, j = i % 4, i // 4
        if kind == 0:
            h, hb = fox_layer(h, hb, fox_w_in[j], fox_b_f[j], fox_w_out[j], ln1_g[i], ln1_b[i], B, S)
        elif kind == 1:
            h, hb = gmlp_layer(h, hb, gm_w_in[j], gm_b_in[j], gm_ln_g[j], gm_ln_b[j], gm_w_s[j],
                               gm_b_s[j], gm_w_out[j], ln1_g[i], ln1_b[i])
        elif kind == 2:
            h, hb = sb_layer(h, hb, sb_w_in[j], sb_w_out[j], ln1_g[i], ln1_b[i], B, S)
        else:
            h, hb = rwkv_layer(h, rw_mu[j], rw_w_rkv[j], rw_w0[j], rw_w1[j], rw_w2[j], rw_a0[j],
                               rw_a1[j], rw_a2[j], rw_g1[j], rw_g2[j], rw_k_k[j], rw_k_a[j], rw_r_k[j],
                               rw_gn_g[j], rw_gn_b[j], rw_w_out[j], ln1_g[i], ln1_b[i], B, S)
        h, hb = grouped_moe_ln(h, router_w, router_b, moe_w_gate[i].astype(BF16),
                               moe_w_up[i].astype(BF16), moe_w_down[i].astype(BF16),
                               ln2_g[i], ln2_b[i])
    return h.reshape(B, S, D)
```

```python
import functools

import jax
import jax.numpy as jnp
from jax import lax
from jax.experimental import pallas as pl
from jax.experimental.pallas import tpu as pltpu

F32 = jnp.float32
BF16 = jnp.bfloat16
I32 = jnp.int32

N_HEADS = 16
HEAD_DIM = 64
CHUNK = 128
GMLP_GROUPS = 8
N_EXPERTS = 16
N_GROUPS = 4
EXPERTS_PER_GROUP = 4
PAIRS_PER_GROUP = 6
N_BUCKETS = N_GROUPS * PAIRS_PER_GROUP
N_BUCKETS_PAD = 32
GATE_LANES = 128
DEPTH = 4
ALPHA = (2 * DEPTH) ** 0.25
LN_EPS = 1e-5
GN_EPS = 64e-5
LOG2E = 1.4426950408889634

VMEM_LIMIT_BYTES = 48 * 1024 * 1024
MOE_TILE = 256
RW_CHUNK = 64
RW_CHUNKS_PER_STEP = 2
LANES = 128
ATT_ROW_PARTS = 2


def _cparams(*sem):
    return pltpu.CompilerParams(dimension_semantics=sem, vmem_limit_bytes=VMEM_LIMIT_BYTES)


def _layer_norm_rows(y, g, b):
    mu = jnp.mean(y, axis=-1, keepdims=True)
    yc = y - mu
    var = jnp.mean(yc * yc, axis=-1, keepdims=True)
    return yc * lax.rsqrt(var + LN_EPS) * g + b


def _split3(x):
    hi = x.astype(BF16)
    r1 = x - hi.astype(F32)
    mid = r1.astype(BF16)
    lo = (r1 - mid.astype(F32)).astype(BF16)
    return hi, mid, lo


def _dot(a, b):
    return jnp.dot(a, b, preferred_element_type=F32)


def _dot_nt(a, b):
    return lax.dot_general(a, b, (((1,), (1,)), ((), ())), preferred_element_type=F32)


def _linear_kernel(x_ref, w_ref, o_ref):
    o_ref[...] = _dot(x_ref[...], w_ref[...]).astype(o_ref.dtype)


def linear(x, w, out_dtype, tm=512, tn=512):
    M, K = x.shape
    N = w.shape[1]
    tn = min(tn, N)
    return pl.pallas_call(
        _linear_kernel,
        grid=(M // tm, N // tn),
        in_specs=[pl.BlockSpec((tm, K), lambda i, j: (i, 0)),
                  pl.BlockSpec((K, tn), lambda i, j: (0, j))],
        out_specs=pl.BlockSpec((tm, tn), lambda i, j: (i, j)),
        out_shape=jax.ShapeDtypeStruct((M, N), out_dtype),
        compiler_params=_cparams("parallel", "parallel"),
        name="linear",
    )(x, w)


def _proj_ln_kernel(a_ref, w_ref, h_ref, g_ref, b_ref, of_ref, ob_ref):
    mix = _dot(a_ref[...], w_ref[...])
    out = _layer_norm_rows(ALPHA * h_ref[...] + mix, g_ref[...], b_ref[...])
    of_ref[...] = out
    ob_ref[...] = out.astype(BF16)


def proj_ln(a, w, h, g, b, tm=512):
    M, K = a.shape
    D = w.shape[1]
    return pl.pallas_call(
        _proj_ln_kernel,
        grid=(M // tm,),
        in_specs=[pl.BlockSpec((tm, K), lambda i: (i, 0)),
                  pl.BlockSpec((K, D), lambda i: (0, 0)),
                  pl.BlockSpec((tm, D), lambda i: (i, 0)),
                  pl.BlockSpec((1, D), lambda i: (0, 0)),
                  pl.BlockSpec((1, D), lambda i: (0, 0))],
        out_specs=[pl.BlockSpec((tm, D), lambda i: (i, 0)),
                   pl.BlockSpec((tm, D), lambda i: (i, 0))],
        out_shape=[jax.ShapeDtypeStruct((M, D), F32), jax.ShapeDtypeStruct((M, D), BF16)],
        compiler_params=_cparams("parallel"),
        name="proj_ln",
    )(a, w, h, g.reshape(1, D), b.reshape(1, D))


def _router_kernel(h_ref, rwt_ref, rb_ref, su_ref, idx_ref, gate_ref, cnt_ref, base_ref):
    step = pl.program_id(0)
    tr = h_ref.shape[0]

    @pl.when(step == 0)
    def _():
        base_ref[...] = jnp.zeros_like(base_ref)

    logits = lax.dot_general(rwt_ref[...], h_ref[...], (((1,), (1,)), ((), ())),
                             precision=lax.Precision.HIGHEST, preferred_element_type=F32)
    scores = jax.nn.sigmoid(logits)
    sel = scores + rb_ref[...]
    s = [sel[e:e + 1, :] for e in range(N_EXPERTS)]
    sc = [scores[e:e + 1, :] for e in range(N_EXPERTS)]

    def top2sum(v):
        best = v[0] + v[1]
        for a in range(4):
            for b in range(a + 1, 4):
                if (a, b) != (0, 1):
                    best = jnp.maximum(best, v[a] + v[b])
        return best

    gs = [top2sum(s[4 * g:4 * g + 4]) for g in range(N_GROUPS)]
    best, gi = gs[0], jnp.zeros((1, tr), I32)
    for g in range(1, N_GROUPS):
        better = gs[g] > best
        gi = jnp.where(better, g, gi)
        best = jnp.where(better, gs[g], best)

    def pick_group(rows):
        out = []
        for j in range(EXPERTS_PER_GROUP):
            v = rows[j]
            for g in range(1, N_GROUPS):
                v = jnp.where(gi == g, rows[4 * g + j], v)
            out.append(v)
        return out

    v = pick_group(s)
    raw = pick_group(sc)
    m1, l1, r1 = v[0], jnp.zeros((1, tr), I32), raw[0]
    for j in range(1, 4):
        better = v[j] > m1
        l1 = jnp.where(better, j, l1)
        m1 = jnp.where(better, v[j], m1)
        r1 = jnp.where(better, raw[j], r1)
    m2 = jnp.full((1, tr), -jnp.inf, F32)
    l2 = jnp.zeros((1, tr), I32)
    r2 = jnp.zeros((1, tr), F32)
    for j in range(4):
        better = jnp.logical_and(l1 != j, v[j] > m2)
        l2 = jnp.where(better, j, l2)
        m2 = jnp.where(better, v[j], m2)
        r2 = jnp.where(better, raw[j], r2)
    den = r1 + r2
    first_low = l1 < l2
    lo = jnp.where(first_low, l1, l2)
    hi = jnp.where(first_low, l2, l1)
    g_lo = jnp.where(first_low, r1, r2) / den
    g_hi = jnp.where(first_low, r2, r1) / den
    gate_ref[...] = jnp.concatenate([g_lo, g_hi], axis=0)
    bucket = gi * PAIRS_PER_GROUP + 2 * lo + hi - 1 - jnp.where(lo == 2, 1, 0)

    bidx = lax.broadcasted_iota(I32, (N_BUCKETS_PAD, tr), 0)
    hit = bidx == bucket
    oh = jnp.where(hit, 1.0, 0.0)
    rank = _dot(oh.astype(BF16), su_ref[...]) + base_ref[...]
    rk = jnp.sum(jnp.where(hit, rank, 0.0), axis=0, keepdims=True).astype(I32)
    idx_ref[...] = jnp.concatenate([bucket, rk], axis=0)
    base_ref[...] = base_ref[...] + jnp.sum(oh, axis=1, keepdims=True)
    cnt_ref[...] = jnp.broadcast_to(base_ref[...], cnt_ref.shape)


def moe_router(h, router_w, router_b, tr=512):
    T, D = h.shape
    su = jnp.triu(jnp.ones((tr, tr), BF16), k=1)
    idx, gate, cnt = pl.pallas_call(
        _router_kernel,
        grid=(T // tr,),
        in_specs=[pl.BlockSpec((tr, D), lambda i: (i, 0)),
                  pl.BlockSpec((N_EXPERTS, D), lambda i: (0, 0)),
                  pl.BlockSpec((N_EXPERTS, 1), lambda i: (0, 0)),
                  pl.BlockSpec((tr, tr), lambda i: (0, 0))],
        out_specs=[pl.BlockSpec((2, tr), lambda i: (0, i)),
                   pl.BlockSpec((2, tr), lambda i: (0, i)),
                   pl.BlockSpec((N_BUCKETS_PAD, LANES), lambda i: (0, 0))],
        out_shape=[jax.ShapeDtypeStruct((2, T), I32), jax.ShapeDtypeStruct((2, T), F32),
                   jax.ShapeDtypeStruct((N_BUCKETS_PAD, LANES), F32)],
        scratch_shapes=[pltpu.VMEM((N_BUCKETS_PAD, 1), F32)],
        compiler_params=_cparams("arbitrary"),
        name="moe_router",
    )(h, router_w.T, router_b.reshape(N_EXPERTS, 1), su)
    return idx, gate, cnt[:N_BUCKETS, 0].astype(I32)


def _dispatch_kernel(pos_ref, h_ref, gate_ref, z_ref, xs_ref, aug, sems):
    del z_ref
    i, n = pl.program_id(0), pl.num_programs(0)
    td, D = h_ref.shape
    slot = i % 2
    stage = aug.at[slot]
    stage[:, :D] = h_ref[...]
    stage[:, D:] = gate_ref[...]

    def row_copy(s, r, dst_row):
        return pltpu.make_async_copy(aug.at[s, pl.ds(r, 1)], xs_ref.at[pl.ds(dst_row, 1)], sems.at[s])

    def issue(r, c):
        row_copy(slot, r, pos_ref[r]).start()
        return c

    lax.fori_loop(0, td, issue, 0, unroll=8)

    def drain(s):
        def body(r, c):
            row_copy(s, 0, 0).wait()
            return c
        lax.fori_loop(0, td, body, 0, unroll=8)

    pl.when(i > 0)(lambda: drain(1 - slot))
    pl.when(i == n - 1)(lambda: drain(slot))


def moe_dispatch(h, pos, gcols, n_rows, td=256):
    T, D = h.shape
    W = D + GATE_LANES
    zeros = jnp.zeros((n_rows, W), h.dtype)
    return pl.pallas_call(
        _dispatch_kernel,
        grid=(T // td,),
        in_specs=[pl.BlockSpec((td,), lambda i: (i,), memory_space=pltpu.SMEM),
                  pl.BlockSpec((td, D), lambda i: (i, 0)),
                  pl.BlockSpec((td, GATE_LANES), lambda i: (i, 0)),
                  pl.BlockSpec(memory_space=pl.ANY)],
        out_specs=pl.BlockSpec(memory_space=pl.ANY),
        out_shape=jax.ShapeDtypeStruct((n_rows, W), h.dtype),
        scratch_shapes=[pltpu.VMEM((2, td, W), h.dtype), pltpu.SemaphoreType.DMA((2,))],
        input_output_aliases={3: 0},
        compiler_params=_cparams("arbitrary"),
        name="moe_dispatch",
    )(pos, h, gcols, zeros)


def _expert_kernel(ta_ref, tb_ref, nu_ref, x_ref, wga, wua, wda, wgb, wub, wdb, o_ref):
    del ta_ref, tb_ref
    D = o_ref.shape[1]

    @pl.when(pl.program_id(0) < nu_ref[0])
    def _():
        xa = x_ref[...]
        x = xa[:, :D].astype(BF16)

        def ffn(wg, wu, wd):
            he = (jax.nn.silu(_dot(x, wg[0])) * _dot(x, wu[0])).astype(BF16)
            return _dot(he, wd[0])

        o_ref[...] = xa[:, D:D + 1] * ffn(wga, wua, wda) + xa[:, D + 1:D + 2] * ffn(wgb, wub, wdb)

    @pl.when(pl.program_id(0) >= nu_ref[0])
    def _():
        o_ref[...] = jnp.zeros_like(o_ref)


def moe_experts(xs, tile_ea, tile_eb, n_used, w_gate, w_up, w_down):
    n_rows, W = xs.shape
    E, D, De = w_gate.shape
    n_tiles = n_rows // MOE_TILE

    def row_map(i, ta, tb, nu):
        return (jnp.maximum(jnp.minimum(i, nu[0] - 1), 0), 0)

    wa = lambda i, ta, tb, nu: (ta[i], 0, 0)
    wb = lambda i, ta, tb, nu: (tb[i], 0, 0)
    grid_spec = pltpu.PrefetchScalarGridSpec(
        num_scalar_prefetch=3,
        grid=(n_tiles,),
        in_specs=[pl.BlockSpec((MOE_TILE, W), row_map),
                  pl.BlockSpec((1, D, De), wa), pl.BlockSpec((1, D, De), wa), pl.BlockSpec((1, De, D), wa),
                  pl.BlockSpec((1, D, De), wb), pl.BlockSpec((1, D, De), wb), pl.BlockSpec((1, De, D), wb)],
        out_specs=pl.BlockSpec((MOE_TILE, D), lambda i, ta, tb, nu: (i, 0)),
    )
    return pl.pallas_call(
        _expert_kernel,
        grid_spec=grid_spec,
        out_shape=jax.ShapeDtypeStruct((n_rows, D), F32),
        compiler_params=_cparams("arbitrary"),
        name="moe_experts",
    )(tile_ea, tile_eb, n_used, xs, w_gate, w_up, w_down, w_gate, w_up, w_down)


def _combine_kernel(pos_ref, posn_ref, h_ref, lg_ref, lb_ref, os_ref, of_ref, ob_ref, buf, sems):
    i, n = pl.program_id(0), pl.num_programs(0)
    tc = h_ref.shape[0]
    slot = i % 2

    def row_copy(s, src_row, r):
        return pltpu.make_async_copy(os_ref.at[pl.ds(src_row, 1)], buf.at[s, pl.ds(r, 1)], sems.at[s])

    def fetch(p_ref, s):
        def body(r, c):
            row_copy(s, p_ref[r], r).start()
            return c
        lax.fori_loop(0, tc, body, 0, unroll=8)

    pl.when(i == 0)(lambda: fetch(pos_ref, slot))
    pl.when(i + 1 < n)(lambda: fetch(posn_ref, 1 - slot))

    def drain(r, c):
        row_copy(slot, 0, 0).wait()
        return c

    lax.fori_loop(0, tc, drain, 0, unroll=8)
    out = _layer_norm_rows(ALPHA * h_ref[...] + buf[slot], lg_ref[...], lb_ref[...])
    of_ref[...] = out
    ob_ref[...] = out.astype(BF16)


def moe_combine(osorted, pos, h, ln_g, ln_b, tc=256):
    T, D = h.shape
    nt = T // tc
    return pl.pallas_call(
        _combine_kernel,
        grid=(nt,),
        in_specs=[pl.BlockSpec((tc,), lambda i: (i,), memory_space=pltpu.SMEM),
                  pl.BlockSpec((tc,), lambda i: (jnp.minimum(i + 1, nt - 1),), memory_space=pltpu.SMEM),
                  pl.BlockSpec((tc, D), lambda i: (i, 0)),
                  pl.BlockSpec((1, D), lambda i: (0, 0)),
                  pl.BlockSpec((1, D), lambda i: (0, 0)),
                  pl.BlockSpec(memory_space=pl.ANY)],
        out_specs=[pl.BlockSpec((tc, D), lambda i: (i, 0)),
                   pl.BlockSpec((tc, D), lambda i: (i, 0))],
        out_shape=[jax.ShapeDtypeStruct((T, D), F32), jax.ShapeDtypeStruct((T, D), BF16)],
        scratch_shapes=[pltpu.VMEM((2, tc, D), F32), pltpu.SemaphoreType.DMA((2,))],
        compiler_params=_cparams("arbitrary"),
        name="moe_combine",
    )(pos, pos, h, ln_g.reshape(1, D), ln_b.reshape(1, D), osorted)


_PAIRS = ((0, 1), (0, 2), (0, 3), (1, 2), (1, 3), (2, 3))


def grouped_moe_ln(h, router_w, router_b, w_gate, w_up, w_down, ln_g, ln_b):
    T, D = h.shape
    idx, gate, cnt = moe_router(h, router_w, router_b)
    bucket, rank = idx[0], idx[1]
    tiles = (cnt + MOE_TILE - 1) // MOE_TILE
    tile_end = jnp.cumsum(tiles)
    start = (tile_end - tiles) * MOE_TILE
    pos = start[bucket] + rank
    n_tiles = T // MOE_TILE + N_BUCKETS
    tile_ids = jnp.arange(n_tiles, dtype=I32)
    tile_bucket = jnp.minimum(
        jnp.sum((tile_ids[:, None] >= tile_end[None, :]).astype(I32), axis=1), N_BUCKETS - 1)
    ea = jnp.array([g * EXPERTS_PER_GROUP + a for g in range(N_GROUPS) for a, _ in _PAIRS], I32)
    eb = jnp.array([g * EXPERTS_PER_GROUP + b for g in range(N_GROUPS) for _, b in _PAIRS], I32)
    n_used = tile_end[-1:].astype(I32)
    gcols = jnp.pad(gate.T, ((0, 0), (0, GATE_LANES - 2)))
    xs = moe_dispatch(h, pos, gcols, n_tiles * MOE_TILE)
    osorted = moe_experts(xs, ea[tile_bucket], eb[tile_bucket], n_used, w_gate, w_up, w_down)
    return moe_combine(osorted, pos, h, ln_g, ln_b)


def _forget_kernel(h_ref, wft_ref, bf_ref, ui_ref, c_ref, carry_ref):
    @pl.when(pl.program_id(1) == 0)
    def _():
        carry_ref[...] = jnp.zeros_like(carry_ref)

    logits = lax.dot_general(wft_ref[...], h_ref[...], (((1,), (1,)), ((), ())),
                             precision=lax.Precision.HIGHEST, preferred_element_type=F32)
    log_f = jax.nn.log_sigmoid(logits + bf_ref[...])
    hi, mid, lo = _split3(log_f)
    ui = ui_ref[...]
    cum = _dot(hi, ui) + _dot(mid, ui) + _dot(lo, ui) + carry_ref[...]
    c_ref[0] = cum * LOG2E
    carry_ref[...] = cum[:, -1:]


def fox_forget_cumsum(h, w_f, b_f, B, S, ts=512):
    T, D = h.shape
    H = w_f.shape[1]
    ui = jnp.triu(jnp.ones((ts, ts), BF16))
    ns = S // ts
    return pl.pallas_call(
        _forget_kernel,
        grid=(B, ns),
        in_specs=[pl.BlockSpec((ts, D), lambda b, i: (b * ns + i, 0)),
                  pl.BlockSpec((H, D), lambda b, i: (0, 0)),
                  pl.BlockSpec((H, 1), lambda b, i: (0, 0)),
                  pl.BlockSpec((ts, ts), lambda b, i: (0, 0))],
        out_specs=pl.BlockSpec((1, H, ts), lambda b, i: (b, 0, i)),
        out_shape=jax.ShapeDtypeStruct((B, H, S), F32),
        scratch_shapes=[pltpu.VMEM((H, 1), F32)],
        compiler_params=_cparams("parallel", "arbitrary"),
        name="fox_forget",
    )(h, w_f.T, b_f.reshape(H, 1), ui)


def _head_pair_masks(q):
    lane = lax.broadcasted_iota(I32, q.shape, 1)
    zero = jnp.zeros_like(q)
    return jnp.where(lane < HEAD_DIM, q, zero), jnp.where(lane >= HEAD_DIM, q, zero)


def _lane_tile(x, n):
    return x if n == 1 else jnp.concatenate([x] * n, axis=1)


def _causal_steps(nq, ratio, descending):
    qi, kb = [], []
    for i in range(nq):
        ks = list(range(ratio * (i + 1)))
        for k in (reversed(ks) if descending else ks):
            qi.append(i)
            kb.append(k)
    return jnp.array(qi, I32), jnp.array(kb, I32)


def _fox_kernel(qi_ref, kb_ref, q_ref, k_ref, v_ref, crow_ref, ccol_ref, og_ref, o_ref,
                qa, qb, ct, m, acc):
    hp, step_id = pl.program_id(1), pl.program_id(2)
    i, j = qi_ref[step_id], kb_ref[step_id]
    tq, tk = q_ref.shape[1], k_ref.shape[1]

    @pl.when(j == 0)
    def _():
        qa[...], qb[...] = _head_pair_masks(q_ref[0])
        cc = ccol_ref[0]
        head = lax.broadcasted_iota(I32, cc.shape, 1)
        for x in range(2):
            col = jnp.sum(jnp.where(head == 2 * hp + x, cc, 0.0), axis=1, keepdims=True)
            ct[x] = jnp.broadcast_to(col, (tq, LANES))
        m[...] = jnp.full(m.shape, -jnp.inf, F32)
        acc[...] = jnp.zeros_like(acc)

    def step(masked):
        k = k_ref[0]
        v_ones = jnp.concatenate([v_ref[0], jnp.ones((tk, LANES), BF16)], axis=1)
        crow = crow_ref[0, 0]
        if masked:
            row = lax.broadcasted_iota(I32, (tq, tk), 0)
            col = lax.broadcasted_iota(I32, (tq, tk), 1)
            keep = col <= row
        rp = tq // ATT_ROW_PARTS
        chains = [(x, qx, slice(p * rp, (p + 1) * rp))
                  for x, qx in enumerate((qa, qb)) for p in range(ATT_ROW_PARTS)]
        st = [dict() for _ in chains]

        def s_scores(c):
            x, qx, rows = chains[c]
            st[c]["s"] = _dot_nt(qx[rows, :], k)

        def s_softmax(c):
            x, qx, rows = chains[c]
            s = st[c].pop("s") - crow[x:x + 1, :]
            if masked:
                s = jnp.where(keep[rows, :], s, -jnp.inf)
            c_t = ct[x, rows, :]
            m_prev = m[x, rows, :]
            m_new = jnp.maximum(m_prev, jnp.max(s, axis=1, keepdims=True) + c_t)
            st[c]["alpha"] = jnp.exp2(m_prev - m_new)
            st[c]["p"] = jnp.exp2(s - _lane_tile(m_new - c_t, tk // LANES)).astype(BF16)
            m[x, rows, :] = m_new

        def s_values(c):
            x, qx, rows = chains[c]
            acc[x, rows, :] = (_lane_tile(st[c].pop("alpha"), 2) * acc[x, rows, :]
                               + _dot(st[c].pop("p"), v_ones))

        stages = (s_scores, s_softmax, s_values)
        for t in range(len(chains) + len(stages) - 1):
            for s in reversed(range(len(stages))):
                if 0 <= t - s < len(chains):
                    stages[s](t - s)

    pl.when(j < i)(lambda: step(False))
    pl.when(j == i)(lambda: step(True))

    @pl.when(j == i)
    def _():
        lane = lax.broadcasted_iota(I32, (tq, LANES), 1)
        a0, a1 = acc[0], acc[1]
        o = jnp.where(lane < HEAD_DIM, a0[:, :LANES] / a0[:, LANES:], a1[:, :LANES] / a1[:, LANES:])
        o_ref[0] = (o * jax.nn.sigmoid(og_ref[0].astype(F32))).astype(BF16)


def fox_attention(qkv, og, c, B, S, tq=512):
    T = qkv.shape[0]
    D = N_HEADS * HEAD_DIM
    hp_n = N_HEADS // 2
    nq = S // tq
    qkv3 = qkv.reshape(B, S, 3 * D)
    crow = c.reshape(B, hp_n, 2, S)
    ccol = jnp.transpose(c, (0, 2, 1))
    qi, kb = _causal_steps(nq, 1, descending=False)
    q_blk = lambda b, hp, s, qi, kb: (b, qi[s], hp)
    kv_blk = lambda off: (lambda b, hp, s, qi, kb: (b, kb[s], off + hp))
    grid_spec = pltpu.PrefetchScalarGridSpec(
        num_scalar_prefetch=2,
        grid=(B, hp_n, qi.shape[0]),
        in_specs=[pl.BlockSpec((1, tq, 128), q_blk),
                  pl.BlockSpec((1, tq, 128), kv_blk(hp_n)),
                  pl.BlockSpec((1, tq, 128), kv_blk(2 * hp_n)),
                  pl.BlockSpec((1, 1, 2, tq), lambda b, hp, s, qi, kb: (b, hp, 0, kb[s])),
                  pl.BlockSpec((1, tq, N_HEADS), lambda b, hp, s, qi, kb: (b, qi[s], 0)),
                  pl.BlockSpec((1, tq, 128), q_blk)],
        out_specs=pl.BlockSpec((1, tq, 128), q_blk),
        scratch_shapes=[pltpu.VMEM((tq, 128), BF16), pltpu.VMEM((tq, 128), BF16),
                        pltpu.VMEM((2, tq, LANES), F32), pltpu.VMEM((2, tq, LANES), F32),
                        pltpu.VMEM((2, tq, 2 * LANES), F32)],
    )
    out = pl.pallas_call(
        _fox_kernel,
        grid_spec=grid_spec,
        out_shape=jax.ShapeDtypeStruct((B, S, D), BF16),
        compiler_params=_cparams("parallel", "parallel", "arbitrary"),
        name="fox_attention",
    )(qi, kb, qkv3, qkv3, qkv3, crow, ccol, og.reshape(B, S, D))
    return out.reshape(T, D)


def _sb_kernel(qi_ref, kb_ref, q_ref, k_ref, v_ref, u_ref, o_ref, qa, qb, rs, acc):
    step_id = pl.program_id(2)
    i, kb = qi_ref[step_id], kb_ref[step_id]
    tq, tk = q_ref.shape[1], k_ref.shape[1]
    ratio = tq // tk

    @pl.when(kb == ratio * (i + 1) - 1)
    def _():
        qa[...], qb[...] = _head_pair_masks(q_ref[0])
        rs[...] = jnp.zeros_like(rs)
        acc[...] = jnp.zeros_like(acc)

    def step(masked):
        k = k_ref[0]
        v = v_ref[0]
        u = u_ref[...]
        if masked:
            row = lax.broadcasted_iota(I32, (tq, tk), 0)
            col = lax.broadcasted_iota(I32, (tq, tk), 1)
            keep = col - row < i * tq - kb * tk
        rp = tq // ATT_ROW_PARTS
        chains = [(x, qx, slice(p * rp, (p + 1) * rp))
                  for x, qx in enumerate((qa, qb)) for p in range(ATT_ROW_PARTS)]
        st = [dict() for _ in chains]

        def s_scores(c):
            x, qx, rows = chains[c]
            st[c]["z"] = _dot_nt(qx[rows, :], k)

        def s_logs(c):
            x, qx, rows = chains[c]
            z = st[c]["z"]
            nl = jnp.maximum(z, 0.0) + jnp.log(1.0 + jnp.exp2(jnp.abs(z) * (-LOG2E)))
            if masked:
                nl = jnp.where(keep[rows, :], nl, 0.0)
            st[c].update(nl=nl.astype(BF16), total=jnp.sum(nl, axis=1, keepdims=True))

        def s_suffix(c):
            st[c]["ci"] = _dot(st[c].pop("nl"), u)

        def s_weights(c):
            x, qx, rows = chains[c]
            prev = rs[x, rows, :]
            a = jnp.exp2((st[c].pop("z") - st[c].pop("ci") - _lane_tile(prev, tk // LANES)) * LOG2E)
            if masked:
                a = jnp.where(keep[rows, :], a, 0.0)
            st[c]["a"] = a.astype(BF16)
            rs[x, rows, :] = prev + st[c].pop("total")

        def s_values(c):
            x, qx, rows = chains[c]
            acc[x, rows, :] = acc[x, rows, :] + _dot(st[c].pop("a"), v)

        stages = (s_scores, s_logs, s_suffix, s_weights, s_values)
        for t in range(len(chains) + len(stages) - 1):
            for s in reversed(range(len(stages))):
                if 0 <= t - s < len(chains):
                    stages[s](t - s)

    pl.when(kb >= ratio * i)(lambda: step(True))
    pl.when(kb < ratio * i)(lambda: step(False))

    @pl.when(kb == 0)
    def _():
        lane = lax.broadcasted_iota(I32, (tq, 2 * HEAD_DIM), 1)
        o_ref[0] = jnp.where(lane < HEAD_DIM, acc[0], acc[1]).astype(BF16)


def stick_breaking_attention(qkv, B, S, tq=512, tk=256):
    T = qkv.shape[0]
    D = N_HEADS * HEAD_DIM
    hp_n = N_HEADS // 2
    nq = S // tq
    qkv3 = qkv.reshape(B, S, 3 * D)
    u = jnp.tril(jnp.ones((tk, tk), BF16))
    qi, kb = _causal_steps(nq, tq // tk, descending=True)
    q_blk = lambda b, hp, s, qi, kb: (b, qi[s], hp)
    kv_blk = lambda off: (lambda b, hp, s, qi, kb: (b, kb[s], off + hp))
    grid_spec = pltpu.PrefetchScalarGridSpec(
        num_scalar_prefetch=2,
        grid=(B, hp_n, qi.shape[0]),
        in_specs=[pl.BlockSpec((1, tq, 128), q_blk),
                  pl.BlockSpec((1, tk, 128), kv_blk(hp_n)),
                  pl.BlockSpec((1, tk, 128), kv_blk(2 * hp_n)),
                  pl.BlockSpec((tk, tk), lambda b, hp, s, qi, kb: (0, 0))],
        out_specs=pl.BlockSpec((1, tq, 128), q_blk),
        scratch_shapes=[pltpu.VMEM((tq, 128), BF16), pltpu.VMEM((tq, 128), BF16),
                        pltpu.VMEM((2, tq, LANES), F32), pltpu.VMEM((2, tq, LANES), F32)],
    )
    out = pl.pallas_call(
        _sb_kernel,
        grid_spec=grid_spec,
        out_shape=jax.ShapeDtypeStruct((B, S, D), BF16),
        compiler_params=_cparams("parallel", "parallel", "arbitrary"),
        name="stick_breaking",
    )(qi, kb, qkv3, qkv3, qkv3, u)
    return out.reshape(T, D)


def _gmlp_kernel(h_ref, win_ref, bin_ref, lg_ref, lb_ref, ws_ref, bst_ref, o_ref):
    tm = h_ref.shape[0]
    dg = o_ref.shape[1]
    z = jax.nn.gelu(_dot(h_ref[...], win_ref[...]) + bin_ref[...])
    u = z[:, :dg]
    v = _layer_norm_rows(z[:, dg:], lg_ref[...], lb_ref[...]).astype(BF16)
    row = lax.broadcasted_iota(I32, (CHUNK, CHUNK), 0)
    col = lax.broadcasted_iota(I32, (CHUNK, CHUNK), 1)
    gw = dg // GMLP_GROUPS
    for g in range(GMLP_GROUPS):
        wc = jnp.where(col <= row, ws_ref[g], 0.0).astype(BF16)
        cs = slice(g * gw, (g + 1) * gw)
        for c in range(tm // CHUNK):
            rs = slice(c * CHUNK, (c + 1) * CHUNK)
            sv = _dot(wc, v[rs, cs]) + bst_ref[:, g:g + 1]
            o_ref[rs, cs] = (u[rs, cs] * sv).astype(BF16)


def gmlp_gate(hb, w_in, b_in, ln_g, ln_b, w_s, b_s, tm=256):
    T, D = hb.shape
    dg = w_in.shape[1] // 2
    return pl.pallas_call(
        _gmlp_kernel,
        grid=(T // tm,),
        in_specs=[pl.BlockSpec((tm, D), lambda i: (i, 0)),
                  pl.BlockSpec((D, 2 * dg), lambda i: (0, 0)),
                  pl.BlockSpec((1, 2 * dg), lambda i: (0, 0)),
                  pl.BlockSpec((1, dg), lambda i: (0, 0)),
                  pl.BlockSpec((1, dg), lambda i: (0, 0)),
                  pl.BlockSpec((GMLP_GROUPS, CHUNK, CHUNK), lambda i: (0, 0, 0)),
                  pl.BlockSpec((CHUNK, GMLP_GROUPS), lambda i: (0, 0))],
        out_specs=pl.BlockSpec((tm, dg), lambda i: (i, 0)),
        out_shape=jax.ShapeDtypeStruct((T, dg), BF16),
        compiler_params=_cparams("parallel"),
        name="gmlp_gate",
    )(hb, w_in, b_in.reshape(1, -1), ln_g.reshape(1, -1), ln_b.reshape(1, -1), w_s, b_s.T)


RW_LANES = 4 * HEAD_DIM


def _seg_sum(x, bones):
    hi = x.astype(BF16)
    lo = (x - hi.astype(F32)).astype(BF16)
    return _dot(hi, bones) + _dot(lo, bones)


def _rwkv_prep_kernel(seq_len, h_ref, hp_ref, mu_ref, wrkv_ref, w1_ref, w2_ref, a1_ref, a2_ref,
                      g1_ref, g2_ref, w0_ref, a0_ref, kk_ref, ka_ref, bones_ref, lchunk_ref,
                      r_ref, cl_ref, clx_ref, k_ref, v_ref, kkn_ref, b_ref, g_ref):
    i = pl.program_id(0)
    tm, D = h_ref.shape
    h = h_ref[...]
    at_start = (i * tm) % seq_len == 0
    prev_row = jnp.where(at_start, 0.0, hp_ref[7:8, :])
    row = lax.broadcasted_iota(I32, (tm, D), 0)
    h_prev = jnp.where(row == 0, prev_row, pltpu.roll(h, 1, axis=0))
    dx = h_prev - h

    def mixed(c):
        return (h + dx * mu_ref[c:c + 1, :]).astype(BF16)

    r = _dot(mixed(0), wrkv_ref[0])
    k = _dot(mixed(1), wrkv_ref[1])
    v = _dot(mixed(2), wrkv_ref[2])
    d = w0_ref[...] + _dot(jnp.tanh(_dot(mixed(3), w1_ref[...])).astype(BF16), w2_ref[...])
    lw = -jnp.exp(-jax.nn.softplus(-d) - 0.5)
    a = jax.nn.sigmoid(a0_ref[...] + _dot(_dot(mixed(4), a1_ref[...]).astype(BF16), a2_ref[...]))
    g = _dot(jax.nn.sigmoid(_dot(mixed(5), g1_ref[...])).astype(BF16), g2_ref[...])
    kk = k * kk_ref[...]
    bones = bones_ref[...]
    norm_sq = jnp.concatenate(
        [_seg_sum(jnp.square(kk[:, q * RW_LANES:(q + 1) * RW_LANES]), bones) for q in range(D // RW_LANES)],
        axis=1)
    kkn = kk / jnp.maximum(jnp.sqrt(norm_sq), 1e-12)
    hi, mid, lo = _split3(lw)
    lchunk = lchunk_ref[...]
    cl = _dot(lchunk, hi) + _dot(lchunk, mid) + _dot(lchunk, lo)
    r_ref[...] = r
    cl_ref[...] = cl
    clx_ref[...] = cl - lw
    k_ref[...] = k * (1.0 + (a - 1.0) * ka_ref[...])
    v_ref[...] = v
    kkn_ref[...] = kkn
    b_ref[...] = kkn * a
    g_ref[...] = g


def _block_ones():
    idx = jnp.arange(RW_LANES) // HEAD_DIM
    return (idx[:, None] == idx[None, :]).astype(BF16)


def rwkv_prep(h, mu, w_rkv, w0, w1, w2, a0, a1, a2, g1, g2, k_k, k_a, S, tm=256):
    T, D = h.shape
    row = lambda i: (i, 0)
    full2 = lambda i: (0, 0)
    vec = lambda x: x.reshape(1, D)
    bf = lambda x: x.astype(BF16)
    pos = jnp.arange(tm)
    lchunk = jnp.logical_and(pos[:, None] // RW_CHUNK == pos[None, :] // RW_CHUNK,
                             pos[None, :] <= pos[:, None]).astype(BF16)
    outs = pl.pallas_call(
        functools.partial(_rwkv_prep_kernel, S),
        grid=(T // tm,),
        in_specs=[pl.BlockSpec((tm, D), row),
                  pl.BlockSpec((8, D), lambda i: (jnp.maximum(i * (tm // 8) - 1, 0), 0)),
                  pl.BlockSpec((6, D), full2),
                  pl.BlockSpec((3, D, D), lambda i: (0, 0, 0)),
                  pl.BlockSpec(w1.shape, full2), pl.BlockSpec(w2.shape, full2),
                  pl.BlockSpec(a1.shape, full2), pl.BlockSpec(a2.shape, full2),
                  pl.BlockSpec(g1.shape, full2), pl.BlockSpec(g2.shape, full2),
                  pl.BlockSpec((1, D), full2), pl.BlockSpec((1, D), full2),
                  pl.BlockSpec((1, D), full2), pl.BlockSpec((1, D), full2),
                  pl.BlockSpec((RW_LANES, RW_LANES), full2),
                  pl.BlockSpec((tm, tm), full2)],
        out_specs=[pl.BlockSpec((tm, D), row)] * 8,
        out_shape=[jax.ShapeDtypeStruct((T, D), F32)] * 8,
        compiler_params=_cparams("parallel"),
        name="rwkv_prep",
    )(h, h, mu, bf(w_rkv), bf(w1), bf(w2), bf(a1), bf(a2), bf(g1), bf(g2),
      vec(w0), vec(a0), vec(k_k), vec(k_a), _block_ones(), lchunk)
    return outs


def _rwkv_local(r, cl, clx, k, v, kk, b, cst):
    C = RW_CHUNK
    mbd, eye, strict, incl, levels = cst
    eye_f = jnp.where(eye, 1.0, 0.0)
    eye_b = eye_f.astype(BF16)

    def each(f, *lists):
        return [f(*xs) for xs in zip(*lists)]

    def bd(y):
        return jnp.where(mbd, jnp.concatenate([y.astype(BF16)] * 4, axis=0), jnp.zeros((), BF16))

    def rows(*parts):
        return jnp.concatenate([p.astype(BF16) for p in parts], axis=0)

    cl_last = each(lambda c: c[C - 1:C, :], cl)
    kkt = each(lambda a, c: a * jnp.exp(c), kk, clx)
    rt = each(lambda a, c: a * jnp.exp(c), r, cl)
    g_inv = each(lambda c: jnp.exp(-c), cl)
    g_tail = each(lambda cl_, c: jnp.exp(cl_ - c), cl_last, cl)
    kk_r = each(rows, kkt, rt)
    ab_k = each(lambda l, a, g: _dot_nt(l, bd(a * g)), kk_r, k, g_inv)
    ab_b = each(lambda l, a, g: _dot_nt(l, bd(a * g)), kk_r, b, g_inv)
    a_k = each(lambda x: jnp.where(strict, x[:C], 0.0), ab_k)
    b_k = each(lambda x: jnp.where(incl, x[C:], 0.0), ab_k)
    a_b = each(lambda x: jnp.where(strict, x[:C], 0.0), ab_b)
    b_b = each(lambda x: jnp.where(incl, x[C:], 0.0), ab_b)
    kh_t = each(lambda a, g: _dot_nt(eye_b, bd(a * g)), k, g_tail)
    bh_t = each(lambda a, g: _dot_nt(eye_b, bd(a * g)), b, g_tail)

    dm = each(lambda a: eye_f - jnp.where(levels[0], a, 0.0), a_b)
    for lvl in levels[1:]:
        w = each(lambda d, a: _dot(d.astype(BF16), bd(jnp.where(lvl, a, 0.0))), dm, a_b)
        dm = each(lambda d, w_: d - _dot(w_.astype(BF16), bd(d)), dm, w)
    tm_b = each(lambda d: d.astype(BF16), dm)

    xv = each(lambda a, b_, c, v_: _dot(rows(a, b_, c), bd(v_)), a_k, b_k, kh_t, v)
    kp = each(lambda t, a: _dot(t, bd(a)), tm_b, kkt)
    vp = each(lambda t, x: _dot(t, bd(x[:C])), tm_b, xv)
    bb_bh = each(rows, b_b, bh_t)
    xkp = each(lambda l, a: _dot(l, bd(a)), bb_bh, kp)
    xvp = each(lambda l, a: _dot(l, bd(a)), bb_bh, vp)
    y_loc = each(lambda x, z: x[C:2 * C] - z[:C], xv, xvp)
    r_eff = each(lambda a, z: a - z[:C], rt, xkp)
    m = each(lambda c, z: eye_f * jnp.exp(c) - z[C:], cl_last, xkp)
    n = each(lambda x, z: x[2 * C:] - z[C:], xv, xvp)
    return r_eff, y_loc, m, n


def _rwkv_apply(r_eff, y_loc, m, n, P, mbd):
    C = RW_CHUNK

    def each(f, *lists):
        return [f(*xs) for xs in zip(*lists)]

    def bd(y):
        return jnp.where(mbd, jnp.concatenate([y.astype(BF16)] * 4, axis=0), jnp.zeros((), BF16))

    def rows(*parts):
        return jnp.concatenate([p.astype(BF16) for p in parts], axis=0)

    p_hi = each(lambda p: p.astype(BF16), P)
    m_hi = each(lambda a: a.astype(BF16), m)
    z_hi = each(lambda a, mh, m_, ph: _dot(rows(a, mh, m_ - mh.astype(F32)), bd(ph)), r_eff, m_hi, m, p_hi)
    z_lo = each(lambda a, mh, p, ph: _dot(rows(a, mh), bd(p - ph.astype(F32))), r_eff, m_hi, P, p_hi)
    y = each(lambda zh, zl, yl: zh[:C] + zl[:C] + yl, z_hi, z_lo, y_loc)
    p_new = each(lambda zh, zl, n_: zh[C:2 * C] + zh[2 * C:] + zl[C:] + n_, z_hi, z_lo, n)
    return y, p_new


def _rwkv_scan_kernel(r_ref, cl_ref, clx_ref, k_ref, v_ref, kk_ref, b_ref, g_ref, rk_ref, gg_ref, gb_ref,
                      bones_ref, o_ref, state):
    C = RW_CHUNK

    @pl.when(pl.program_id(1) == 0)
    def _():
        state[...] = jnp.zeros_like(state)

    t = lax.broadcasted_iota(I32, (C, RW_LANES), 0)
    i = lax.broadcasted_iota(I32, (C, RW_LANES), 1) % C
    levels = []
    for sh in range(6):
        tb, ib = t >> sh, i >> sh
        levels.append(jnp.logical_and(tb == ib + 1, (tb & 1) == 1))
    rr = lax.broadcasted_iota(I32, (RW_LANES, RW_LANES), 0) // HEAD_DIM
    cc = lax.broadcasted_iota(I32, (RW_LANES, RW_LANES), 1) // HEAD_DIM
    cst = (rr == cc, i == t, i < t, i <= t, levels)
    bones = bones_ref[...]

    n_groups = r_ref.shape[1] // RW_LANES
    n_chunks = r_ref.shape[0] // C
    tiles = [(slice(c * C, (c + 1) * C), slice(q * RW_LANES, (q + 1) * RW_LANES))
             for c in range(n_chunks) for q in range(n_groups)]
    cut = lambda ref: [ref[rs, ls] for rs, ls in tiles]
    r, k, v = cut(r_ref), cut(k_ref), cut(v_ref)
    r_eff, y_loc, m, n = _rwkv_local(r, cut(cl_ref), cut(clx_ref), k, v, cut(kk_ref), cut(b_ref), cst)
    p = [state[q] for q in range(n_groups)]
    ys = []
    for c in range(n_chunks):
        part = slice(c * n_groups, (c + 1) * n_groups)
        y_c, p = _rwkv_apply(r_eff[part], y_loc[part], m[part], n[part], p, cst[0])
        ys += y_c
    for q in range(n_groups):
        state[q] = p[q]
    stack = lambda xs: jnp.concatenate(xs, axis=0)
    y = stack(ys)
    mean = _seg_sum(y, bones) * (1.0 / HEAD_DIM)
    yc = y - mean
    var = _seg_sum(yc * yc, bones) * (1.0 / HEAD_DIM)
    yn = yc * lax.rsqrt(var + GN_EPS)
    rkr = stack([r[idx] * k[idx] * rk_ref[:, ls] for idx, (rs, ls) in enumerate(tiles)])
    dots = _seg_sum(rkr, bones)
    for idx, (rs, ls) in enumerate(tiles):
        part = slice(idx * C, (idx + 1) * C)
        out = yn[part] * gg_ref[:, ls] + gb_ref[:, ls] + dots[part] * v[idx]
        o_ref[rs, ls] = (out * g_ref[rs, ls]).astype(BF16)


def rwkv_scan(r, cl, clx, k, v, kkn, b, g, r_k, gn_g, gn_b, B, S):
    T, D = r.shape
    rows_per_step = RW_CHUNK * RW_CHUNKS_PER_STEP
    nc = S // rows_per_step
    row = lambda bi, ci: (bi * nc + ci, 0)
    full2 = lambda bi, ci: (0, 0)
    return pl.pallas_call(
        _rwkv_scan_kernel,
        grid=(B, nc),
        in_specs=[pl.BlockSpec((rows_per_step, D), row)] * 8 + [pl.BlockSpec((1, D), full2)] * 3
        + [pl.BlockSpec((RW_LANES, RW_LANES), full2)],
        out_specs=pl.BlockSpec((rows_per_step, D), row),
        out_shape=jax.ShapeDtypeStruct((T, D), BF16),
        scratch_shapes=[pltpu.VMEM((D // RW_LANES, HEAD_DIM, RW_LANES), F32)],
        compiler_params=_cparams("parallel", "arbitrary"),
        name="rwkv_scan",
    )(r, cl, clx, k, v, kkn, b, g, r_k.reshape(1, D), gn_g.reshape(1, D), gn_b.reshape(1, D),
      _block_ones())


def rwkv_layer(h, mu, w_rkv, w0, w1, w2, a0, a1, a2, g1, g2, k_k, k_a, r_k, gn_g, gn_b, w_out,
               ln_g, ln_b, B, S):
    r, cl, clx, k, v, kkn, b, g = rwkv_prep(h, mu, w_rkv, w0, w1, w2, a0, a1, a2, g1, g2, k_k, k_a, S)
    y = rwkv_scan(r, cl, clx, k, v, kkn, b, g, r_k, gn_g, gn_b, B, S)
    return proj_ln(y, w_out.astype(BF16), h, ln_g, ln_b)


def fox_layer(h, hb, w_in, b_f, w_out, ln_g, ln_b, B, S):
    D = h.shape[1]
    scale = HEAD_DIM ** -0.5 * LOG2E
    w_qkv = jnp.concatenate([w_in[:, :D] * scale, w_in[:, D:3 * D]], axis=1).astype(BF16)
    w_f = w_in[:, 3 * D:3 * D + N_HEADS]
    w_og = w_in[:, 3 * D + N_HEADS:].astype(BF16)
    qkv = linear(hb, w_qkv, BF16)
    og = linear(hb, w_og, BF16)
    c = fox_forget_cumsum(h, w_f, b_f, B, S)
    o = fox_attention(qkv, og, c, B, S)
    return proj_ln(o, w_out.astype(BF16), h, ln_g, ln_b)


def sb_layer(h, hb, w_in, w_out, ln_g, ln_b, B, S):
    D = h.shape[1]
    scale = HEAD_DIM ** -0.5
    w_qkv = jnp.concatenate([w_in[:, :D] * scale, w_in[:, D:]], axis=1).astype(BF16)
    qkv = linear(hb, w_qkv, BF16)
    o = stick_breaking_attention(qkv, B, S)
    return proj_ln(o, w_out.astype(BF16), h, ln_g, ln_b)


def gmlp_layer(h, hb, w_in, b_in, gm_ln_g, gm_ln_b, w_s, b_s, w_out, ln_g, ln_b):
    y = gmlp_gate(hb, w_in.astype(BF16), b_in, gm_ln_g, gm_ln_b, w_s, b_s)
    return proj_ln(y, w_out.astype(BF16), h, ln_g, ln_b)


def kernel(x, ln1_g, ln1_b, ln2_g, ln2_b, fox_w_in, fox_b_f, fox_w_out, gm_w_in, gm_b_in, gm_ln_g,
           gm_ln_b, gm_w_s, gm_b_s, gm_w_out, sb_w_in, sb_w_out, rw_mu, rw_w_rkv, rw_w0, rw_w1, rw_w2,
           rw_a0, rw_a1, rw_a2, rw_g1, rw_g2, rw_k_k, rw_k_a, rw_r_k, rw_gn_g, rw_gn_b, rw_w_out,
           router_w, router_b, moe_w_gate, moe_w_up, moe_w_down):
    B, S, D = x.shape
    h = x.reshape(B * S, D)
    hb = h.astype(BF16)
    for i in range(DEPTH):
        kind, j = i % 4, i // 4
        if kind == 0:
            h, hb = fox_layer(h, hb, fox_w_in[j], fox_b_f[j], fox_w_out[j], ln1_g[i], ln1_b[i], B, S)
        elif kind == 1:
            h, hb = gmlp_layer(h, hb, gm_w_in[j], gm_b_in[j], gm_ln_g[j], gm_ln_b[j], gm_w_s[j],
                               gm_b_s[j], gm_w_out[j], ln1_g[i], ln1_b[i])
        elif kind == 2:
            h, hb = sb_layer(h, hb, sb_w_in[j], sb_w_out[j], ln1_g[i], ln1_b[i], B, S)
        else:
            h, hb = rwkv_layer(h, rw_mu[j], rw_w_rkv[j], rw_w0[j], rw_w1[j], rw_w2[j], rw_a0[j],
                               rw_a1[j], rw_a2[j], rw_g1[j], rw_g2[j], rw_k_k[j], rw_k_a[j], rw_r_k[j],
                               rw_gn_g[j], rw_gn_b[j], rw_w_out[j], ln1_g[i], ln1_b[i], B, S)
        h, hb = grouped_moe_ln(h, router_w, router_b, moe_w_gate[i].astype(BF16),
                               moe_w_up[i].astype(BF16), moe_w_down[i].astype(BF16),
                               ln2_g[i], ln2_b[i])
    return h.reshape(B, S, D)
```

```python
import functools

import jax
import jax.numpy as jnp
from jax import lax
from jax.experimental import pallas as pl
from jax.experimental.pallas import tpu as pltpu

F32 = jnp.float32
BF16 = jnp.bfloat16
I32 = jnp.int32

N_HEADS = 16
HEAD_DIM = 64
CHUNK = 128
GMLP_GROUPS = 8
N_EXPERTS = 16
N_GROUPS = 4
EXPERTS_PER_GROUP = 4
PAIRS_PER_GROUP = 6
N_BUCKETS = N_GROUPS * PAIRS_PER_GROUP
N_BUCKETS_PAD = 32
DMA_UNROLL = 8
GATE_LANES = 128
DEPTH = 4
ALPHA = (2 * DEPTH) ** 0.25
LN_EPS = 1e-5
GN_EPS = 64e-5
LOG2E = 1.4426950408889634

VMEM_LIMIT_BYTES = 48 * 1024 * 1024
MOE_TILE = 256
RW_CHUNK = 64
RW_CHUNKS_PER_STEP = 2
LANES = 128
ATT_PAIRS = 4
ATT_ROW_PARTS = 1


def _cparams(*sem):
    return pltpu.CompilerParams(dimension_semantics=sem, vmem_limit_bytes=VMEM_LIMIT_BYTES)


def _layer_norm_rows(y, g, b):
    mu = jnp.mean(y, axis=-1, keepdims=True)
    yc = y - mu
    var = jnp.mean(yc * yc, axis=-1, keepdims=True)
    return yc * lax.rsqrt(var + LN_EPS) * g + b


def _split3(x):
    hi = x.astype(BF16)
    r1 = x - hi.astype(F32)
    mid = r1.astype(BF16)
    lo = (r1 - mid.astype(F32)).astype(BF16)
    return hi, mid, lo


def _dot(a, b):
    return jnp.dot(a, b, preferred_element_type=F32)


def _dot_nt(a, b):
    return lax.dot_general(a, b, (((1,), (1,)), ((), ())), preferred_element_type=F32)


def _dot_nt_split(w, x):
    n = w.shape[0]
    w_hi = w.astype(BF16)
    w_lo = (w - w_hi.astype(F32)).astype(BF16)
    x_hi = x.astype(BF16)
    x_lo = (x - x_hi.astype(F32)).astype(BF16)
    both = _dot_nt(jnp.concatenate([w_hi, w_lo], axis=0), x_hi)
    return both[:n] + both[n:] + _dot_nt(w_hi, x_lo)


def _linear_kernel(x_ref, w_ref, o_ref):
    o_ref[...] = _dot(x_ref[...], w_ref[...]).astype(o_ref.dtype)


def linear(x, w, out_dtype, tm=1024, tn=1024):
    M, K = x.shape
    N = w.shape[1]
    tn = min(tn, N)
    return pl.pallas_call(
        _linear_kernel,
        grid=(M // tm, N // tn),
        in_specs=[pl.BlockSpec((tm, K), lambda i, j: (i, 0)),
                  pl.BlockSpec((K, tn), lambda i, j: (0, j))],
        out_specs=pl.BlockSpec((tm, tn), lambda i, j: (i, j)),
        out_shape=jax.ShapeDtypeStruct((M, N), out_dtype),
        compiler_params=_cparams("parallel", "parallel"),
        name="linear",
    )(x, w)


def _proj_ln_kernel(a_ref, w_ref, h_ref, g_ref, b_ref, of_ref, ob_ref):
    mix = _dot(a_ref[...], w_ref[...])
    out = _layer_norm_rows(ALPHA * h_ref[...] + mix, g_ref[...], b_ref[...])
    of_ref[...] = out
    ob_ref[...] = out.astype(BF16)


def proj_ln(a, w, h, g, b, tm=512):
    M, K = a.shape
    D = w.shape[1]
    return pl.pallas_call(
        _proj_ln_kernel,
        grid=(M // tm,),
        in_specs=[pl.BlockSpec((tm, K), lambda i: (i, 0)),
                  pl.BlockSpec((K, D), lambda i: (0, 0)),
                  pl.BlockSpec((tm, D), lambda i: (i, 0)),
                  pl.BlockSpec((1, D), lambda i: (0, 0)),
                  pl.BlockSpec((1, D), lambda i: (0, 0))],
        out_specs=[pl.BlockSpec((tm, D), lambda i: (i, 0)),
                   pl.BlockSpec((tm, D), lambda i: (i, 0))],
        out_shape=[jax.ShapeDtypeStruct((M, D), F32), jax.ShapeDtypeStruct((M, D), BF16)],
        compiler_params=_cparams("parallel"),
        name="proj_ln",
    )(a, w, h, g.reshape(1, D), b.reshape(1, D))


def _router_kernel(h_ref, rwt_ref, rb_ref, su_ref, idx_ref, gate_ref, cnt_ref, base_ref):
    step = pl.program_id(0)
    tr = h_ref.shape[0]

    @pl.when(step == 0)
    def _():
        base_ref[...] = jnp.zeros_like(base_ref)

    logits = _dot_nt_split(rwt_ref[...], h_ref[...])
    scores = jax.nn.sigmoid(logits)
    sel = scores + rb_ref[...]
    s = [sel[e:e + 1, :] for e in range(N_EXPERTS)]
    sc = [scores[e:e + 1, :] for e in range(N_EXPERTS)]

    def top2sum(v):
        best = v[0] + v[1]
        for a in range(4):
            for b in range(a + 1, 4):
                if (a, b) != (0, 1):
                    best = jnp.maximum(best, v[a] + v[b])
        return best

    gs = [top2sum(s[4 * g:4 * g + 4]) for g in range(N_GROUPS)]
    best, gi = gs[0], jnp.zeros((1, tr), I32)
    for g in range(1, N_GROUPS):
        better = gs[g] > best
        gi = jnp.where(better, g, gi)
        best = jnp.where(better, gs[g], best)

    def pick_group(rows):
        out = []
        for j in range(EXPERTS_PER_GROUP):
            v = rows[j]
            for g in range(1, N_GROUPS):
                v = jnp.where(gi == g, rows[4 * g + j], v)
            out.append(v)
        return out

    v = pick_group(s)
    raw = pick_group(sc)
    m1, l1, r1 = v[0], jnp.zeros((1, tr), I32), raw[0]
    for j in range(1, 4):
        better = v[j] > m1
        l1 = jnp.where(better, j, l1)
        m1 = jnp.where(better, v[j], m1)
        r1 = jnp.where(better, raw[j], r1)
    m2 = jnp.full((1, tr), -jnp.inf, F32)
    l2 = jnp.zeros((1, tr), I32)
    r2 = jnp.zeros((1, tr), F32)
    for j in range(4):
        better = jnp.logical_and(l1 != j, v[j] > m2)
        l2 = jnp.where(better, j, l2)
        m2 = jnp.where(better, v[j], m2)
        r2 = jnp.where(better, raw[j], r2)
    den = r1 + r2
    first_low = l1 < l2
    lo = jnp.where(first_low, l1, l2)
    hi = jnp.where(first_low, l2, l1)
    g_lo = jnp.where(first_low, r1, r2) / den
    g_hi = jnp.where(first_low, r2, r1) / den
    gate_ref[...] = jnp.concatenate([g_lo, g_hi], axis=0)
    bucket = gi * PAIRS_PER_GROUP + 2 * lo + hi - 1 - jnp.where(lo == 2, 1, 0)

    bidx = lax.broadcasted_iota(I32, (N_BUCKETS_PAD, tr), 0)
    hit = bidx == bucket
    oh = jnp.where(hit, 1.0, 0.0)
    rank = _dot(oh.astype(BF16), su_ref[...]) + base_ref[...]
    rk = jnp.sum(jnp.where(hit, rank, 0.0), axis=0, keepdims=True).astype(I32)
    idx_ref[...] = jnp.concatenate([bucket, rk], axis=0)
    base_ref[...] = base_ref[...] + jnp.sum(oh, axis=1, keepdims=True)
    cnt_ref[...] = jnp.broadcast_to(base_ref[...], cnt_ref.shape)


def moe_router(h, router_w, router_b, tr=512):
    T, D = h.shape
    su = jnp.triu(jnp.ones((tr, tr), BF16), k=1)
    idx, gate, cnt = pl.pallas_call(
        _router_kernel,
        grid=(T // tr,),
        in_specs=[pl.BlockSpec((tr, D), lambda i: (i, 0)),
                  pl.BlockSpec((N_EXPERTS, D), lambda i: (0, 0)),
                  pl.BlockSpec((N_EXPERTS, 1), lambda i: (0, 0)),
                  pl.BlockSpec((tr, tr), lambda i: (0, 0))],
        out_specs=[pl.BlockSpec((2, tr), lambda i: (0, i)),
                   pl.BlockSpec((2, tr), lambda i: (0, i)),
                   pl.BlockSpec((N_BUCKETS_PAD, LANES), lambda i: (0, 0))],
        out_shape=[jax.ShapeDtypeStruct((2, T), I32), jax.ShapeDtypeStruct((2, T), F32),
                   jax.ShapeDtypeStruct((N_BUCKETS_PAD, LANES), F32)],
        scratch_shapes=[pltpu.VMEM((N_BUCKETS_PAD, 1), F32)],
        compiler_params=_cparams("arbitrary"),
        name="moe_router",
    )(h, router_w.T, router_b.reshape(N_EXPERTS, 1), su)
    return idx, gate, cnt[:N_BUCKETS, 0].astype(I32)


def _dispatch_kernel(pos_ref, h_ref, gate_ref, z_ref, xs_ref, aug, sems):
    del z_ref
    i, n = pl.program_id(0), pl.num_programs(0)
    td, D = h_ref.shape
    slot = i % 2
    stage = aug.at[slot]
    stage[:, :D] = h_ref[...]
    stage[:, D:] = gate_ref[...]

    def row_copy(s, r, dst_row):
        return pltpu.make_async_copy(aug.at[s, pl.ds(r, 1)], xs_ref.at[pl.ds(dst_row, 1)], sems.at[s])

    def issue(g, c):
        for j in range(DMA_UNROLL):
            r = g * DMA_UNROLL + j
            row_copy(slot, r, pos_ref[r]).start(priority=j % 2)
        return c

    lax.fori_loop(0, td // DMA_UNROLL, issue, 0)

    def drain(s):
        pltpu.make_async_copy(aug.at[s], xs_ref.at[pl.ds(0, td)], sems.at[s]).wait()

    pl.when(i > 0)(lambda: drain(1 - slot))
    pl.when(i == n - 1)(lambda: drain(slot))


def moe_dispatch(h, pos, gcols, n_rows, td=512):
    T, D = h.shape
    W = D + GATE_LANES
    zeros = jnp.zeros((n_rows, W), h.dtype)
    return pl.pallas_call(
        _dispatch_kernel,
        grid=(T // td,),
        in_specs=[pl.BlockSpec((td,), lambda i: (i,), memory_space=pltpu.SMEM),
                  pl.BlockSpec((td, D), lambda i: (i, 0)),
                  pl.BlockSpec((td, GATE_LANES), lambda i: (i, 0)),
                  pl.BlockSpec(memory_space=pl.ANY)],
        out_specs=pl.BlockSpec(memory_space=pl.ANY),
        out_shape=jax.ShapeDtypeStruct((n_rows, W), h.dtype),
        scratch_shapes=[pltpu.VMEM((2, td, W), h.dtype), pltpu.SemaphoreType.DMA((2,))],
        input_output_aliases={3: 0},
        compiler_params=_cparams("arbitrary"),
        name="moe_dispatch",
    )(pos, h, gcols, zeros)


def _expert_kernel(ta_ref, tb_ref, nu_ref, x_ref, wga, wua, wda, wgb, wub, wdb, o_ref):
    del ta_ref, tb_ref
    D = o_ref.shape[1]

    @pl.when(pl.program_id(0) < nu_ref[0])
    def _():
        xa = x_ref[...]
        x = xa[:, :D].astype(BF16)

        def ffn(wg, wu, wd):
            he = (jax.nn.silu(_dot(x, wg[0])) * _dot(x, wu[0])).astype(BF16)
            return _dot(he, wd[0])

        o_ref[...] = xa[:, D:D + 1] * ffn(wga, wua, wda) + xa[:, D + 1:D + 2] * ffn(wgb, wub, wdb)

    @pl.when(pl.program_id(0) >= nu_ref[0])
    def _():
        o_ref[...] = jnp.zeros_like(o_ref)


def moe_experts(xs, tile_ea, tile_eb, n_used, w_gate, w_up, w_down):
    n_rows, W = xs.shape
    E, D, De = w_gate.shape
    n_tiles = n_rows // MOE_TILE

    def row_map(i, ta, tb, nu):
        return (jnp.maximum(jnp.minimum(i, nu[0] - 1), 0), 0)

    wa = lambda i, ta, tb, nu: (ta[i], 0, 0)
    wb = lambda i, ta, tb, nu: (tb[i], 0, 0)
    grid_spec = pltpu.PrefetchScalarGridSpec(
        num_scalar_prefetch=3,
        grid=(n_tiles,),
        in_specs=[pl.BlockSpec((MOE_TILE, W), row_map),
                  pl.BlockSpec((1, D, De), wa), pl.BlockSpec((1, D, De), wa), pl.BlockSpec((1, De, D), wa),
                  pl.BlockSpec((1, D, De), wb), pl.BlockSpec((1, D, De), wb), pl.BlockSpec((1, De, D), wb)],
        out_specs=pl.BlockSpec((MOE_TILE, D), lambda i, ta, tb, nu: (i, 0)),
    )
    return pl.pallas_call(
        _expert_kernel,
        grid_spec=grid_spec,
        out_shape=jax.ShapeDtypeStruct((n_rows, D), F32),
        compiler_params=_cparams("arbitrary"),
        name="moe_experts",
    )(tile_ea, tile_eb, n_used, xs, w_gate, w_up, w_down, w_gate, w_up, w_down)


def _combine_kernel(pos_ref, posn_ref, h_ref, lg_ref, lb_ref, os_ref, of_ref, ob_ref, buf, sems):
    i, n = pl.program_id(0), pl.num_programs(0)
    tc = h_ref.shape[0]
    slot = i % 2

    def row_copy(s, src_row, r):
        return pltpu.make_async_copy(os_ref.at[pl.ds(src_row, 1)], buf.at[s, pl.ds(r, 1)], sems.at[s])

    def fetch(p_ref, s):
        def body(g, c):
            for j in range(DMA_UNROLL):
                r = g * DMA_UNROLL + j
                row_copy(s, p_ref[r], r).start(priority=j % 2)
            return c
        lax.fori_loop(0, tc // DMA_UNROLL, body, 0)

    pl.when(i == 0)(lambda: fetch(pos_ref, slot))
    pl.when(i + 1 < n)(lambda: fetch(posn_ref, 1 - slot))

    pltpu.make_async_copy(os_ref.at[pl.ds(0, tc)], buf.at[slot], sems.at[slot]).wait()
    out = _layer_norm_rows(ALPHA * h_ref[...] + buf[slot], lg_ref[...], lb_ref[...])
    of_ref[...] = out
    ob_ref[...] = out.astype(BF16)


def moe_combine(osorted, pos, h, ln_g, ln_b, tc=512):
    T, D = h.shape
    nt = T // tc
    return pl.pallas_call(
        _combine_kernel,
        grid=(nt,),
        in_specs=[pl.BlockSpec((tc,), lambda i: (i,), memory_space=pltpu.SMEM),
                  pl.BlockSpec((tc,), lambda i: (jnp.minimum(i + 1, nt - 1),), memory_space=pltpu.SMEM),
                  pl.BlockSpec((tc, D), lambda i: (i, 0)),
                  pl.BlockSpec((1, D), lambda i: (0, 0)),
                  pl.BlockSpec((1, D), lambda i: (0, 0)),
                  pl.BlockSpec(memory_space=pl.ANY)],
        out_specs=[pl.BlockSpec((tc, D), lambda i: (i, 0)),
                   pl.BlockSpec((tc, D), lambda i: (i, 0))],
        out_shape=[jax.ShapeDtypeStruct((T, D), F32), jax.ShapeDtypeStruct((T, D), BF16)],
        scratch_shapes=[pltpu.VMEM((2, tc, D), F32), pltpu.SemaphoreType.DMA((2,))],
        compiler_params=_cparams("arbitrary"),
        name="moe_combine",
    )(pos, pos, h, ln_g.reshape(1, D), ln_b.reshape(1, D), osorted)


_PAIRS = ((0, 1), (0, 2), (0, 3), (1, 2), (1, 3), (2, 3))


def grouped_moe_ln(h, router_w, router_b, w_gate, w_up, w_down, ln_g, ln_b):
    T, D = h.shape
    idx, gate, cnt = moe_router(h, router_w, router_b)
    bucket, rank = idx[0], idx[1]
    tiles = (cnt + MOE_TILE - 1) // MOE_TILE
    tile_end = jnp.cumsum(tiles)
    start = (tile_end - tiles) * MOE_TILE
    pos = start[bucket] + rank
    n_tiles = T // MOE_TILE + N_BUCKETS
    tile_ids = jnp.arange(n_tiles, dtype=I32)
    tile_bucket = jnp.minimum(
        jnp.sum((tile_ids[:, None] >= tile_end[None, :]).astype(I32), axis=1), N_BUCKETS - 1)
    ea = jnp.array([g * EXPERTS_PER_GROUP + a for g in range(N_GROUPS) for a, _ in _PAIRS], I32)
    eb = jnp.array([g * EXPERTS_PER_GROUP + b for g in range(N_GROUPS) for _, b in _PAIRS], I32)
    n_used = tile_end[-1:].astype(I32)
    gcols = jnp.pad(gate.T, ((0, 0), (0, GATE_LANES - 2)))
    xs = moe_dispatch(h, pos, gcols, n_tiles * MOE_TILE)
    osorted = moe_experts(xs, ea[tile_bucket], eb[tile_bucket], n_used, w_gate, w_up, w_down)
    return moe_combine(osorted, pos, h, ln_g, ln_b)


def _forget_kernel(h_ref, wft_ref, bf_ref, ui_ref, c_ref, carry_ref):
    @pl.when(pl.program_id(1) == 0)
    def _():
        carry_ref[...] = jnp.zeros_like(carry_ref)

    logits = _dot_nt_split(wft_ref[...], h_ref[...])
    log_f = jax.nn.log_sigmoid(logits + bf_ref[...])
    hi, mid, lo = _split3(log_f)
    ui = ui_ref[...]
    cum = _dot(hi, ui) + _dot(mid, ui) + _dot(lo, ui) + carry_ref[...]
    c_ref[0] = cum * LOG2E
    carry_ref[...] = cum[:, -1:]


def fox_forget_cumsum(h, w_f, b_f, B, S, ts=512):
    T, D = h.shape
    H = w_f.shape[1]
    ui = jnp.triu(jnp.ones((ts, ts), BF16))
    ns = S // ts
    return pl.pallas_call(
        _forget_kernel,
        grid=(B, ns),
        in_specs=[pl.BlockSpec((ts, D), lambda b, i: (b * ns + i, 0)),
                  pl.BlockSpec((H, D), lambda b, i: (0, 0)),
                  pl.BlockSpec((H, 1), lambda b, i: (0, 0)),
                  pl.BlockSpec((ts, ts), lambda b, i: (0, 0))],
        out_specs=pl.BlockSpec((1, H, ts), lambda b, i: (b, 0, i)),
        out_shape=jax.ShapeDtypeStruct((B, H, S), F32),
        scratch_shapes=[pltpu.VMEM((H, 1), F32)],
        compiler_params=_cparams("parallel", "arbitrary"),
        name="fox_forget",
    )(h, w_f.T, b_f.reshape(H, 1), ui)


def _head_pair_masks(q):
    lane = lax.broadcasted_iota(I32, q.shape, 1)
    zero = jnp.zeros_like(q)
    return jnp.where(lane < HEAD_DIM, q, zero), jnp.where(lane >= HEAD_DIM, q, zero)


def _lane_tile(x, n):
    return x if n == 1 else jnp.concatenate([x] * n, axis=1)


def _causal_steps(nq, ratio, descending):
    qi, kb = [], []
    for i in range(nq):
        ks = list(range(ratio * (i + 1)))
        for k in (reversed(ks) if descending else ks):
            qi.append(i)
            kb.append(k)
    return jnp.array(qi, I32), jnp.array(kb, I32)


def _emit_skewed(stages, n_chains):
    for t in range(n_chains + len(stages) - 1):
        for s in reversed(range(len(stages))):
            if 0 <= t - s < n_chains:
                stages[s](t - s)


def _pair(p):
    return slice(p * LANES, (p + 1) * LANES)


def _fox_kernel(qi_ref, kb_ref, q_ref, k_ref, v_ref, crow_ref, ccol_ref, og_ref, o_ref, qm, ct, m, acc):
    hg, step_id = pl.program_id(1), pl.program_id(2)
    i, j = qi_ref[step_id], kb_ref[step_id]
    tq, tk = q_ref.shape[1], k_ref.shape[1]
    n_heads = qm.shape[0]

    @pl.when(j == 0)
    def _():
        q = q_ref[0]
        cc = ccol_ref[0]
        head = lax.broadcasted_iota(I32, cc.shape, 1)
        for hh in range(n_heads):
            qm[hh] = _head_pair_masks(q[:, _pair(hh // 2)])[hh % 2]
            col = jnp.sum(jnp.where(head == n_heads * hg + hh, cc, 0.0), axis=1, keepdims=True)
            ct[hh] = jnp.broadcast_to(col, (tq, LANES))
        m[...] = jnp.full(m.shape, -jnp.inf, F32)
        acc[...] = jnp.zeros_like(acc)

    def step(masked):
        ks = [k_ref[0, :, _pair(p)] for p in range(n_heads // 2)]
        ones = jnp.ones((tk, LANES), BF16)
        v_ones = [jnp.concatenate([v_ref[0, :, _pair(p)], ones], axis=1) for p in range(n_heads // 2)]
        crow = crow_ref[0]
        if masked:
            row = lax.broadcasted_iota(I32, (tq, tk), 0)
            col = lax.broadcasted_iota(I32, (tq, tk), 1)
            keep = col <= row
        rp = tq // ATT_ROW_PARTS
        chains = [(hh, slice(p * rp, (p + 1) * rp)) for hh in range(n_heads) for p in range(ATT_ROW_PARTS)]
        st = [dict() for _ in chains]

        def s_scores(c):
            hh, rows = chains[c]
            st[c]["s"] = _dot_nt(qm[hh, rows, :], ks[hh // 2])

        def s_softmax(c):
            hh, rows = chains[c]
            s = st[c].pop("s") - crow[hh // 2, hh % 2:hh % 2 + 1, :]
            if masked:
                s = jnp.where(keep[rows, :], s, -jnp.inf)
            c_t = ct[hh, rows, :]
            m_prev = m[hh, rows, :]
            m_new = jnp.maximum(m_prev, jnp.max(s, axis=1, keepdims=True) + c_t)
            st[c]["alpha"] = jnp.exp2(m_prev - m_new)
            st[c]["p"] = jnp.exp2(s - _lane_tile(m_new - c_t, tk // LANES)).astype(BF16)
            m[hh, rows, :] = m_new

        def s_values(c):
            hh, rows = chains[c]
            acc[hh, rows, :] = (_lane_tile(st[c].pop("alpha"), 2) * acc[hh, rows, :]
                                + _dot(st[c].pop("p"), v_ones[hh // 2]))

        _emit_skewed((s_scores, s_softmax, s_values), len(chains))

    pl.when(j < i)(lambda: step(False))
    pl.when(j == i)(lambda: step(True))

    @pl.when(j == i)
    def _():
        lane = lax.broadcasted_iota(I32, (tq, LANES), 1)
        for p in range(n_heads // 2):
            a0, a1 = acc[2 * p], acc[2 * p + 1]
            o = jnp.where(lane < HEAD_DIM, a0[:, :LANES] / a0[:, LANES:], a1[:, :LANES] / a1[:, LANES:])
            o_ref[0, :, _pair(p)] = (o * jax.nn.sigmoid(og_ref[0, :, _pair(p)].astype(F32))).astype(BF16)


def fox_attention(qkv, og, c, B, S, tq=512):
    T = qkv.shape[0]
    D = N_HEADS * HEAD_DIM
    hp_n = N_HEADS // 2
    nq = S // tq
    qkv3 = qkv.reshape(B, S, 3 * D)
    crow = c.reshape(B, hp_n, 2, S)
    ccol = jnp.transpose(c, (0, 2, 1))
    qi, kb = _causal_steps(nq, 1, descending=False)
    w = ATT_PAIRS * LANES
    hg_n = hp_n // ATT_PAIRS
    nh = 2 * ATT_PAIRS
    q_blk = lambda b, hg, s, qi, kb: (b, qi[s], hg)
    kv_blk = lambda off: (lambda b, hg, s, qi, kb: (b, kb[s], off + hg))
    grid_spec = pltpu.PrefetchScalarGridSpec(
        num_scalar_prefetch=2,
        grid=(B, hg_n, qi.shape[0]),
        in_specs=[pl.BlockSpec((1, tq, w), q_blk),
                  pl.BlockSpec((1, tq, w), kv_blk(hg_n)),
                  pl.BlockSpec((1, tq, w), kv_blk(2 * hg_n)),
                  pl.BlockSpec((1, ATT_PAIRS, 2, tq), lambda b, hg, s, qi, kb: (b, hg, 0, kb[s])),
                  pl.BlockSpec((1, tq, N_HEADS), lambda b, hg, s, qi, kb: (b, qi[s], 0)),
                  pl.BlockSpec((1, tq, w), q_blk)],
        out_specs=pl.BlockSpec((1, tq, w), q_blk),
        scratch_shapes=[pltpu.VMEM((nh, tq, LANES), BF16),
                        pltpu.VMEM((nh, tq, LANES), F32), pltpu.VMEM((nh, tq, LANES), F32),
                        pltpu.VMEM((nh, tq, 2 * LANES), F32)],
    )
    out = pl.pallas_call(
        _fox_kernel,
        grid_spec=grid_spec,
        out_shape=jax.ShapeDtypeStruct((B, S, D), BF16),
        compiler_params=_cparams("parallel", "parallel", "arbitrary"),
        name="fox_attention",
    )(qi, kb, qkv3, qkv3, qkv3, crow, ccol, og.reshape(B, S, D))
    return out.reshape(T, D)


def _sb_kernel(qi_ref, kb_ref, q_ref, k_ref, v_ref, u_ref, o_ref, qm, rs, acc):
    step_id = pl.program_id(2)
    i, kb = qi_ref[step_id], kb_ref[step_id]
    tq, tk = q_ref.shape[1], k_ref.shape[1]
    ratio = tq // tk

    n_heads = qm.shape[0]

    @pl.when(kb == ratio * (i + 1) - 1)
    def _():
        q = q_ref[0]
        for hh in range(n_heads):
            qm[hh] = _head_pair_masks(q[:, _pair(hh // 2)])[hh % 2]
        rs[...] = jnp.zeros_like(rs)
        acc[...] = jnp.zeros_like(acc)

    def step(masked):
        ks = [k_ref[0, :, _pair(p)] for p in range(n_heads // 2)]
        vs = [v_ref[0, :, _pair(p)] for p in range(n_heads // 2)]
        u = u_ref[...]
        if masked:
            row = lax.broadcasted_iota(I32, (tq, tk), 0)
            col = lax.broadcasted_iota(I32, (tq, tk), 1)
            keep = col - row < i * tq - kb * tk
        rp = tq // ATT_ROW_PARTS
        chains = [(hh, slice(p * rp, (p + 1) * rp)) for hh in range(n_heads) for p in range(ATT_ROW_PARTS)]
        st = [dict() for _ in chains]

        def s_scores(c):
            x, rows = chains[c]
            st[c]["z"] = _dot_nt(qm[x, rows, :], ks[x // 2])

        def s_logs(c):
            x, rows = chains[c]
            z = st[c]["z"]
            nl = jnp.maximum(z, 0.0) + jnp.log(1.0 + jnp.exp2(jnp.abs(z) * (-LOG2E)))
            if masked:
                nl = jnp.where(keep[rows, :], nl, 0.0)
            st[c].update(nl=nl.astype(BF16), total=jnp.sum(nl, axis=1, keepdims=True))

        def s_suffix(c):
            st[c]["ci"] = _dot(st[c].pop("nl"), u)

        def s_weights(c):
            x, rows = chains[c]
            prev = rs[x, rows, :]
            a = jnp.exp2((st[c].pop("z") - st[c].pop("ci") - _lane_tile(prev, tk // LANES)) * LOG2E)
            if masked:
                a = jnp.where(keep[rows, :], a, 0.0)
            st[c]["a"] = a.astype(BF16)
            rs[x, rows, :] = prev + st[c].pop("total")

        def s_values(c):
            x, rows = chains[c]
            acc[x, rows, :] = acc[x, rows, :] + _dot(st[c].pop("a"), vs[x // 2])

        _emit_skewed((s_scores, s_logs, s_suffix, s_weights, s_values), len(chains))

    pl.when(kb >= ratio * i)(lambda: step(True))
    pl.when(kb < ratio * i)(lambda: step(False))

    @pl.when(kb == 0)
    def _():
        lane = lax.broadcasted_iota(I32, (tq, LANES), 1)
        for p in range(n_heads // 2):
            o_ref[0, :, _pair(p)] = jnp.where(lane < HEAD_DIM, acc[2 * p], acc[2 * p + 1]).astype(BF16)


def stick_breaking_attention(qkv, B, S, tq=512, tk=256):
    T = qkv.shape[0]
    D = N_HEADS * HEAD_DIM
    hp_n = N_HEADS // 2
    nq = S // tq
    qkv3 = qkv.reshape(B, S, 3 * D)
    u = jnp.tril(jnp.ones((tk, tk), BF16))
    qi, kb = _causal_steps(nq, tq // tk, descending=True)
    w = ATT_PAIRS * LANES
    hg_n = hp_n // ATT_PAIRS
    nh = 2 * ATT_PAIRS
    q_blk = lambda b, hg, s, qi, kb: (b, qi[s], hg)
    kv_blk = lambda off: (lambda b, hg, s, qi, kb: (b, kb[s], off + hg))
    grid_spec = pltpu.PrefetchScalarGridSpec(
        num_scalar_prefetch=2,
        grid=(B, hg_n, qi.shape[0]),
        in_specs=[pl.BlockSpec((1, tq, w), q_blk),
                  pl.BlockSpec((1, tk, w), kv_blk(hg_n)),
                  pl.BlockSpec((1, tk, w), kv_blk(2 * hg_n)),
                  pl.BlockSpec((tk, tk), lambda b, hg, s, qi, kb: (0, 0))],
        out_specs=pl.BlockSpec((1, tq, w), q_blk),
        scratch_shapes=[pltpu.VMEM((nh, tq, LANES), BF16),
                        pltpu.VMEM((nh, tq, LANES), F32), pltpu.VMEM((nh, tq, LANES), F32)],
    )
    out = pl.pallas_call(
        _sb_kernel,
        grid_spec=grid_spec,
        out_shape=jax.ShapeDtypeStruct((B, S, D), BF16),
        compiler_params=_cparams("parallel", "parallel", "arbitrary"),
        name="stick_breaking",
    )(qi, kb, qkv3, qkv3, qkv3, u)
    return out.reshape(T, D)


def _gmlp_kernel(h_ref, win_ref, bin_ref, lg_ref, lb_ref, ws_ref, bst_ref, o_ref):
    tm = h_ref.shape[0]
    dg = o_ref.shape[1]
    z = jax.nn.gelu(_dot(h_ref[...], win_ref[...]) + bin_ref[...])
    u = z[:, :dg]
    v = _layer_norm_rows(z[:, dg:], lg_ref[...], lb_ref[...]).astype(BF16)
    row = lax.broadcasted_iota(I32, (CHUNK, CHUNK), 0)
    col = lax.broadcasted_iota(I32, (CHUNK, CHUNK), 1)
    gw = dg // GMLP_GROUPS
    for g in range(GMLP_GROUPS):
        wc = jnp.where(col <= row, ws_ref[g], 0.0).astype(BF16)
        cs = slice(g * gw, (g + 1) * gw)
        for c in range(tm // CHUNK):
            rs = slice(c * CHUNK, (c + 1) * CHUNK)
            sv = _dot(wc, v[rs, cs]) + bst_ref[:, g:g + 1]
            o_ref[rs, cs] = (u[rs, cs] * sv).astype(BF16)


def gmlp_gate(hb, w_in, b_in, ln_g, ln_b, w_s, b_s, tm=256):
    T, D = hb.shape
    dg = w_in.shape[1] // 2
    return pl.pallas_call(
        _gmlp_kernel,
        grid=(T // tm,),
        in_specs=[pl.BlockSpec((tm, D), lambda i: (i, 0)),
                  pl.BlockSpec((D, 2 * dg), lambda i: (0, 0)),
                  pl.BlockSpec((1, 2 * dg), lambda i: (0, 0)),
                  pl.BlockSpec((1, dg), lambda i: (0, 0)),
                  pl.BlockSpec((1, dg), lambda i: (0, 0)),
                  pl.BlockSpec((GMLP_GROUPS, CHUNK, CHUNK), lambda i: (0, 0, 0)),
                  pl.BlockSpec((CHUNK, GMLP_GROUPS), lambda i: (0, 0))],
        out_specs=pl.BlockSpec((tm, dg), lambda i: (i, 0)),
        out_shape=jax.ShapeDtypeStruct((T, dg), BF16),
        compiler_params=_cparams("parallel"),
        name="gmlp_gate",
    )(hb, w_in, b_in.reshape(1, -1), ln_g.reshape(1, -1), ln_b.reshape(1, -1), w_s, b_s.T)


RW_LANES = 4 * HEAD_DIM


def _seg_sum(x, bones):
    hi = x.astype(BF16)
    lo = (x - hi.astype(F32)).astype(BF16)
    return _dot(hi, bones) + _dot(lo, bones)


def _rwkv_prep_kernel(seq_len, h_ref, hp_ref, mu_ref, wrkv_ref, w1_ref, w2_ref, a1_ref, a2_ref,
                      g1_ref, g2_ref, w0_ref, a0_ref, kk_ref, ka_ref, bones_ref, lchunk_ref,
                      r_ref, cl_ref, clx_ref, k_ref, v_ref, kkn_ref, b_ref, g_ref):
    i = pl.program_id(0)
    tm, D = h_ref.shape
    h = h_ref[...]
    at_start = (i * tm) % seq_len == 0
    prev_row = jnp.where(at_start, 0.0, hp_ref[7:8, :])
    row = lax.broadcasted_iota(I32, (tm, D), 0)
    h_prev = jnp.where(row == 0, prev_row, pltpu.roll(h, 1, axis=0))
    dx = h_prev - h

    def mixed(c):
        return (h + dx * mu_ref[c:c + 1, :]).astype(BF16)

    r = _dot(mixed(0), wrkv_ref[0])
    k = _dot(mixed(1), wrkv_ref[1])
    v = _dot(mixed(2), wrkv_ref[2])
    d = w0_ref[...] + _dot(jnp.tanh(_dot(mixed(3), w1_ref[...])).astype(BF16), w2_ref[...])
    lw = -jnp.exp(-jax.nn.softplus(-d) - 0.5)
    a = jax.nn.sigmoid(a0_ref[...] + _dot(_dot(mixed(4), a1_ref[...]).astype(BF16), a2_ref[...]))
    g = _dot(jax.nn.sigmoid(_dot(mixed(5), g1_ref[...])).astype(BF16), g2_ref[...])
    kk = k * kk_ref[...]
    bones = bones_ref[...]
    norm_sq = jnp.concatenate(
        [_seg_sum(jnp.square(kk[:, q * RW_LANES:(q + 1) * RW_LANES]), bones) for q in range(D // RW_LANES)],
        axis=1)
    kkn = kk / jnp.maximum(jnp.sqrt(norm_sq), 1e-12)
    hi, mid, lo = _split3(lw)
    lchunk = lchunk_ref[...]
    cl = _dot(lchunk, hi) + _dot(lchunk, mid) + _dot(lchunk, lo)
    r_ref[...] = r
    cl_ref[...] = cl
    clx_ref[...] = cl - lw
    k_ref[...] = k * (1.0 + (a - 1.0) * ka_ref[...])
    v_ref[...] = v
    kkn_ref[...] = kkn
    b_ref[...] = kkn * a
    g_ref[...] = g


def _block_ones():
    idx = jnp.arange(RW_LANES) // HEAD_DIM
    return (idx[:, None] == idx[None, :]).astype(BF16)


def rwkv_prep(h, mu, w_rkv, w0, w1, w2, a0, a1, a2, g1, g2, k_k, k_a, S, tm=256):
    T, D = h.shape
    row = lambda i: (i, 0)
    full2 = lambda i: (0, 0)
    vec = lambda x: x.reshape(1, D)
    bf = lambda x: x.astype(BF16)
    pos = jnp.arange(tm)
    lchunk = jnp.logical_and(pos[:, None] // RW_CHUNK == pos[None, :] // RW_CHUNK,
                             pos[None, :] <= pos[:, None]).astype(BF16)
    outs = pl.pallas_call(
        functools.partial(_rwkv_prep_kernel, S),
        grid=(T // tm,),
        in_specs=[pl.BlockSpec((tm, D), row),
                  pl.BlockSpec((8, D), lambda i: (jnp.maximum(i * (tm // 8) - 1, 0), 0)),
                  pl.BlockSpec((6, D), full2),
                  pl.BlockSpec((3, D, D), lambda i: (0, 0, 0)),
                  pl.BlockSpec(w1.shape, full2), pl.BlockSpec(w2.shape, full2),
                  pl.BlockSpec(a1.shape, full2), pl.BlockSpec(a2.shape, full2),
                  pl.BlockSpec(g1.shape, full2), pl.BlockSpec(g2.shape, full2),
                  pl.BlockSpec((1, D), full2), pl.BlockSpec((1, D), full2),
                  pl.BlockSpec((1, D), full2), pl.BlockSpec((1, D), full2),
                  pl.BlockSpec((RW_LANES, RW_LANES), full2),
                  pl.BlockSpec((tm, tm), full2)],
        out_specs=[pl.BlockSpec((tm, D), row)] * 8,
        out_shape=[jax.ShapeDtypeStruct((T, D), F32)] * 8,
        compiler_params=_cparams("parallel"),
        name="rwkv_prep",
    )(h, h, mu, bf(w_rkv), bf(w1), bf(w2), bf(a1), bf(a2), bf(g1), bf(g2),
      vec(w0), vec(a0), vec(k_k), vec(k_a), _block_ones(), lchunk)
    return outs


def _rwkv_local(r, cl, clx, k, v, kk, b, cst):
    C = RW_CHUNK
    mbd, eye, strict, incl, levels = cst
    eye_f = jnp.where(eye, 1.0, 0.0)
    eye_b = eye_f.astype(BF16)

    def each(f, *lists):
        return [f(*xs) for xs in zip(*lists)]

    def bd(y):
        return jnp.where(mbd, jnp.concatenate([y.astype(BF16)] * 4, axis=0), jnp.zeros((), BF16))

    def rows(*parts):
        return jnp.concatenate([p.astype(BF16) for p in parts], axis=0)

    cl_last = each(lambda c: c[C - 1:C, :], cl)
    kkt = each(lambda a, c: a * jnp.exp(c), kk, clx)
    rt = each(lambda a, c: a * jnp.exp(c), r, cl)
    g_inv = each(lambda c: jnp.exp(-c), cl)
    g_tail = each(lambda cl_, c: jnp.exp(cl_ - c), cl_last, cl)
    kk_r = each(rows, kkt, rt)
    ab_k = each(lambda l, a, g: _dot_nt(l, bd(a * g)), kk_r, k, g_inv)
    ab_b = each(lambda l, a, g: _dot_nt(l, bd(a * g)), kk_r, b, g_inv)
    a_k = each(lambda x: jnp.where(strict, x[:C], 0.0), ab_k)
    b_k = each(lambda x: jnp.where(incl, x[C:], 0.0), ab_k)
    a_b = each(lambda x: jnp.where(strict, x[:C], 0.0), ab_b)
    b_b = each(lambda x: jnp.where(incl, x[C:], 0.0), ab_b)
    kh_t = each(lambda a, g: _dot_nt(eye_b, bd(a * g)), k, g_tail)
    bh_t = each(lambda a, g: _dot_nt(eye_b, bd(a * g)), b, g_tail)

    dm = each(lambda a: eye_f - jnp.where(levels[0], a, 0.0), a_b)
    for lvl in levels[1:]:
        w = each(lambda d, a: _dot(d.astype(BF16), bd(jnp.where(lvl, a, 0.0))), dm, a_b)
        dm = each(lambda d, w_: d - _dot(w_.astype(BF16), bd(d)), dm, w)
    tm_b = each(lambda d: d.astype(BF16), dm)

    xv = each(lambda a, b_, c, v_: _dot(rows(a, b_, c), bd(v_)), a_k, b_k, kh_t, v)
    kp = each(lambda t, a: _dot(t, bd(a)), tm_b, kkt)
    vp = each(lambda t, x: _dot(t, bd(x[:C])), tm_b, xv)
    bb_bh = each(rows, b_b, bh_t)
    xkp = each(lambda l, a: _dot(l, bd(a)), bb_bh, kp)
    xvp = each(lambda l, a: _dot(l, bd(a)), bb_bh, vp)
    y_loc = each(lambda x, z: x[C:2 * C] - z[:C], xv, xvp)
    r_eff = each(lambda a, z: a - z[:C], rt, xkp)
    m = each(lambda c, z: eye_f * jnp.exp(c) - z[C:], cl_last, xkp)
    n = each(lambda x, z: x[2 * C:] - z[C:], xv, xvp)
    return r_eff, y_loc, m, n


def _rwkv_apply(r_eff, y_loc, m, n, P, mbd):
    C = RW_CHUNK

    def each(f, *lists):
        return [f(*xs) for xs in zip(*lists)]

    def bd(y):
        return jnp.where(mbd, jnp.concatenate([y.astype(BF16)] * 4, axis=0), jnp.zeros((), BF16))

    def rows(*parts):
        return jnp.concatenate([p.astype(BF16) for p in parts], axis=0)

    p_hi = each(lambda p: p.astype(BF16), P)
    m_hi = each(lambda a: a.astype(BF16), m)
    z_hi = each(lambda a, mh, m_, ph: _dot(rows(a, mh, m_ - mh.astype(F32)), bd(ph)), r_eff, m_hi, m, p_hi)
    z_lo = each(lambda a, mh, p, ph: _dot(rows(a, mh), bd(p - ph.astype(F32))), r_eff, m_hi, P, p_hi)
    y = each(lambda zh, zl, yl: zh[:C] + zl[:C] + yl, z_hi, z_lo, y_loc)
    p_new = each(lambda zh, zl, n_: zh[C:2 * C] + zh[2 * C:] + zl[C:] + n_, z_hi, z_lo, n)
    return y, p_new


def _rwkv_scan_kernel(r_ref, cl_ref, clx_ref, k_ref, v_ref, kk_ref, b_ref, g_ref, rk_ref, gg_ref, gb_ref,
                      bones_ref, o_ref, state):
    C = RW_CHUNK

    @pl.when(pl.program_id(1) == 0)
    def _():
        state[...] = jnp.zeros_like(state)

    t = lax.broadcasted_iota(I32, (C, RW_LANES), 0)
    i = lax.broadcasted_iota(I32, (C, RW_LANES), 1) % C
    levels = []
    for sh in range(6):
        tb, ib = t >> sh, i >> sh
        levels.append(jnp.logical_and(tb == ib + 1, (tb & 1) == 1))
    rr = lax.broadcasted_iota(I32, (RW_LANES, RW_LANES), 0) // HEAD_DIM
    cc = lax.broadcasted_iota(I32, (RW_LANES, RW_LANES), 1) // HEAD_DIM
    cst = (rr == cc, i == t, i < t, i <= t, levels)
    bones = bones_ref[...]

    n_groups = r_ref.shape[1] // RW_LANES
    n_chunks = r_ref.shape[0] // C
    tiles = [(slice(c * C, (c + 1) * C), slice(q * RW_LANES, (q + 1) * RW_LANES))
             for c in range(n_chunks) for q in range(n_groups)]
    cut = lambda ref: [ref[rs, ls] for rs, ls in tiles]
    r, k, v = cut(r_ref), cut(k_ref), cut(v_ref)
    r_eff, y_loc, m, n = _rwkv_local(r, cut(cl_ref), cut(clx_ref), k, v, cut(kk_ref), cut(b_ref), cst)
    p = [state[q] for q in range(n_groups)]
    ys = []
    for c in range(n_chunks):
        part = slice(c * n_groups, (c + 1) * n_groups)
        y_c, p = _rwkv_apply(r_eff[part], y_loc[part], m[part], n[part], p, cst[0])
        ys += y_c
    for q in range(n_groups):
        state[q] = p[q]
    stack = lambda xs: jnp.concatenate(xs, axis=0)
    y = stack(ys)
    mean = _seg_sum(y, bones) * (1.0 / HEAD_DIM)
    yc = y - mean
    var = _seg_sum(yc * yc, bones) * (1.0 / HEAD_DIM)
    yn = yc * lax.rsqrt(var + GN_EPS)
    rkr = stack([r[idx] * k[idx] * rk_ref[:, ls] for idx, (rs, ls) in enumerate(tiles)])
    dots = _seg_sum(rkr, bones)
    for idx, (rs, ls) in enumerate(tiles):
        part = slice(idx * C, (idx + 1) * C)
        out = yn[part] * gg_ref[:, ls] + gb_ref[:, ls] + dots[part] * v[idx]
        o_ref[rs, ls] = (out * g_ref[rs, ls]).astype(BF16)


def rwkv_scan(r, cl, clx, k, v, kkn, b, g, r_k, gn_g, gn_b, B, S):
    T, D = r.shape
    rows_per_step = RW_CHUNK * RW_CHUNKS_PER_STEP
    nc = S // rows_per_step
    row = lambda bi, ci: (bi * nc + ci, 0)
    full2 = lambda bi, ci: (0, 0)
    return pl.pallas_call(
        _rwkv_scan_kernel,
        grid=(B, nc),
        in_specs=[pl.BlockSpec((rows_per_step, D), row)] * 8 + [pl.BlockSpec((1, D), full2)] * 3
        + [pl.BlockSpec((RW_LANES, RW_LANES), full2)],
        out_specs=pl.BlockSpec((rows_per_step, D), row),
        out_shape=jax.ShapeDtypeStruct((T, D), BF16),
        scratch_shapes=[pltpu.VMEM((D // RW_LANES, HEAD_DIM, RW_LANES), F32)],
        compiler_params=_cparams("parallel", "arbitrary"),
        name="rwkv_scan",
    )(r, cl, clx, k, v, kkn, b, g, r_k.reshape(1, D), gn_g.reshape(1, D), gn_b.reshape(1, D),
      _block_ones())


def rwkv_layer(h, mu, w_rkv, w0, w1, w2, a0, a1, a2, g1, g2, k_k, k_a, r_k, gn_g, gn_b, w_out,
               ln_g, ln_b, B, S):
    r, cl, clx, k, v, kkn, b, g = rwkv_prep(h, mu, w_rkv, w0, w1, w2, a0, a1, a2, g1, g2, k_k, k_a, S)
    y = rwkv_scan(r, cl, clx, k, v, kkn, b, g, r_k, gn_g, gn_b, B, S)
    return proj_ln(y, w_out.astype(BF16), h, ln_g, ln_b)


def fox_layer(h, hb, w_in, b_f, w_out, ln_g, ln_b, B, S):
    D = h.shape[1]
    scale = HEAD_DIM ** -0.5 * LOG2E
    w_qkv = jnp.concatenate([w_in[:, :D] * scale, w_in[:, D:3 * D]], axis=1).astype(BF16)
    w_f = w_in[:, 3 * D:3 * D + N_HEADS]
    w_og = w_in[:, 3 * D + N_HEADS:].astype(BF16)
    qkv = linear(hb, w_qkv, BF16)
    og = linear(hb, w_og, BF16)
    c = fox_forget_cumsum(h, w_f, b_f, B, S)
    o = fox_attention(qkv, og, c, B, S)
    return proj_ln(o, w_out.astype(BF16), h, ln_g, ln_b)


def sb_layer(h, hb, w_in, w_out, ln_g, ln_b, B, S):
    D = h.shape[1]
    scale = HEAD_DIM ** -0.5
    w_qkv = jnp.concatenate([w_in[:, :D] * scale, w_in[:, D:]], axis=1).astype(BF16)
    qkv = linear(hb, w_qkv, BF16)
    o = stick_breaking_attention(qkv, B, S)
    return proj_ln(o, w_out.astype(BF16), h, ln_g, ln_b)


def gmlp_layer(h, hb, w_in, b_in, gm_ln_g, gm_ln_b, w_s, b_s, w_out, ln_g, ln_b):
    y = gmlp_gate(hb, w_in.astype(BF16), b_in, gm_ln_g, gm_ln_b, w_s, b_s)
    return proj_ln(y, w_out.astype(BF16), h, ln_g, ln_b)


def kernel(x, ln1_g, ln1_b, ln2_g, ln2_b, fox_w_in, fox_b_f, fox_w_out, gm_w_in, gm_b_in, gm_ln_g,
           gm_ln_b, gm_w_s, gm_b_s, gm_w_out, sb_w_in, sb_w_out, rw_mu, rw_w_rkv, rw_w0, rw_w1, rw_w2,
           rw_a0, rw_a1, rw_a2, rw_g1, rw_g2, rw_k_k, rw_k_a, rw_r_k, rw_gn_g, rw_gn_b, rw_w_out,
           router_w, router_b, moe_w_gate, moe_w_up, moe_w_down):
    B, S, D = x.shape
    h = x.reshape(B * S, D)
    hb = h.astype(BF16)
    for i in range(DEPTH):
        kind, j = i % 4, i // 4
        if kind == 0:
            h, hb = fox_layer(h, hb, fox_w_in[j], fox_b_f[j], fox_w_out[j], ln1_g[i], ln1_b[i], B, S)
        elif kind == 1:
            h, hb = gmlp_layer(h, hb, gm_w_in[j], gm_b_in[j], gm_ln_g[j], gm_ln_b[j], gm_w_s[j],
                               gm_b_s[j], gm_w_out[j], ln1_g[i], ln1_b[i])
        elif kind == 2:
            h, hb = sb_layer(h, hb, sb_w_in[j], sb_w_out[j], ln1_g[i], ln1_b[i], B, S)
        else:
            h, hb = rwkv_layer(h, rw_mu[j], rw_w_rkv[j], rw_w0[j], rw_w1[j], rw_w2[j], rw_a0[j],
                               rw_a1[j], rw_a2[j], rw_g1[j], rw_g2[j], rw_k_k[j], rw_k_a[j], rw_r_k[j],
                               rw_gn_g[j], rw_gn_b[j], rw_w_out[j], ln1_g[i], ln1_b[i], B, S)
        h, hb = grouped_moe_ln(h, router_w, router_b, moe_w_gate[i].astype(BF16),
                               moe_w_up[i].astype(BF16), moe_w_down[i].astype(BF16),
                               ln2_g[i], ln2_b[i])
    return h.reshape(B, S, D)
```

```python
import functools

import jax
import jax.numpy as jnp
from jax import lax
from jax.experimental import pallas as pl
from jax.experimental.pallas import tpu as pltpu

F32 = jnp.float32
BF16 = jnp.bfloat16
I32 = jnp.int32

N_HEADS = 16
HEAD_DIM = 64
CHUNK = 128
GMLP_GROUPS = 8
N_EXPERTS = 16
N_GROUPS = 4
EXPERTS_PER_GROUP = 4
PAIRS_PER_GROUP = 6
N_BUCKETS = N_GROUPS * PAIRS_PER_GROUP
N_BUCKETS_PAD = 32
PROJ_ROW_PARTS = 2
GMLP_ROW_PARTS = 2
DMA_UNROLL = 8
GATE_LANES = 128
DEPTH = 4
ALPHA = (2 * DEPTH) ** 0.25
LN_EPS = 1e-5
GN_EPS = 64e-5
LOG2E = 1.4426950408889634

VMEM_LIMIT_BYTES = 48 * 1024 * 1024
MOE_TILE = 256
RW_CHUNK = 64
RW_CHUNKS_PER_STEP = 2
LANES = 128
ATT_PAIRS = 4
ATT_ROW_PARTS = 1


def _cparams(*sem):
    return pltpu.CompilerParams(dimension_semantics=sem, vmem_limit_bytes=VMEM_LIMIT_BYTES)


def _layer_norm_rows(y, g, b):
    mu = jnp.mean(y, axis=-1, keepdims=True)
    yc = y - mu
    var = jnp.mean(yc * yc, axis=-1, keepdims=True)
    return yc * lax.rsqrt(var + LN_EPS) * g + b


def _split3(x):
    hi = x.astype(BF16)
    r1 = x - hi.astype(F32)
    mid = r1.astype(BF16)
    lo = (r1 - mid.astype(F32)).astype(BF16)
    return hi, mid, lo


def _dot(a, b):
    return jnp.dot(a, b, preferred_element_type=F32)


def _dot_nt(a, b):
    return lax.dot_general(a, b, (((1,), (1,)), ((), ())), preferred_element_type=F32)


def _dot_nt_split(w, x):
    n = w.shape[0]
    w_hi = w.astype(BF16)
    w_lo = (w - w_hi.astype(F32)).astype(BF16)
    x_hi = x.astype(BF16)
    x_lo = (x - x_hi.astype(F32)).astype(BF16)
    both = _dot_nt(jnp.concatenate([w_hi, w_lo], axis=0), x_hi)
    return both[:n] + both[n:] + _dot_nt(w_hi, x_lo)


def _linear_kernel(x_ref, w_ref, o_ref):
    o_ref[...] = _dot(x_ref[...], w_ref[...]).astype(o_ref.dtype)


def linear(x, w, out_dtype, tm=1024, tn=1024):
    M, K = x.shape
    N = w.shape[1]
    tn = min(tn, N)
    return pl.pallas_call(
        _linear_kernel,
        grid=(M // tm, N // tn),
        in_specs=[pl.BlockSpec((tm, K), lambda i, j: (i, 0)),
                  pl.BlockSpec((K, tn), lambda i, j: (0, j))],
        out_specs=pl.BlockSpec((tm, tn), lambda i, j: (i, j)),
        out_shape=jax.ShapeDtypeStruct((M, N), out_dtype),
        compiler_params=_cparams("parallel", "parallel"),
        name="linear",
    )(x, w)


def _proj_ln_kernel(a_ref, w_ref, h_ref, g_ref, b_ref, of_ref, ob_ref):
    tm = a_ref.shape[0]
    rp = tm // PROJ_ROW_PARTS
    parts = [slice(p * rp, (p + 1) * rp) for p in range(PROJ_ROW_PARTS)]
    mix = [None] * PROJ_ROW_PARTS

    def s_proj(c):
        mix[c] = _dot(a_ref[parts[c], :], w_ref[...])

    def s_norm(c):
        out = _layer_norm_rows(ALPHA * h_ref[parts[c], :] + mix[c], g_ref[...], b_ref[...])
        of_ref[parts[c], :] = out
        ob_ref[parts[c], :] = out.astype(BF16)

    _emit_skewed((s_proj, s_norm), PROJ_ROW_PARTS)


def proj_ln(a, w, h, g, b, tm=512):
    M, K = a.shape
    D = w.shape[1]
    return pl.pallas_call(
        _proj_ln_kernel,
        grid=(M // tm,),
        in_specs=[pl.BlockSpec((tm, K), lambda i: (i, 0)),
                  pl.BlockSpec((K, D), lambda i: (0, 0)),
                  pl.BlockSpec((tm, D), lambda i: (i, 0)),
                  pl.BlockSpec((1, D), lambda i: (0, 0)),
                  pl.BlockSpec((1, D), lambda i: (0, 0))],
        out_specs=[pl.BlockSpec((tm, D), lambda i: (i, 0)),
                   pl.BlockSpec((tm, D), lambda i: (i, 0))],
        out_shape=[jax.ShapeDtypeStruct((M, D), F32), jax.ShapeDtypeStruct((M, D), BF16)],
        compiler_params=_cparams("parallel"),
        name="proj_ln",
    )(a, w, h, g.reshape(1, D), b.reshape(1, D))


def _router_kernel(h_ref, rwt_ref, rb_ref, su_ref, idx_ref, gate_ref, cnt_ref, base_ref):
    step = pl.program_id(0)
    tr = h_ref.shape[0]

    @pl.when(step == 0)
    def _():
        base_ref[...] = jnp.zeros_like(base_ref)

    logits = _dot_nt_split(rwt_ref[...], h_ref[...])
    scores = jax.nn.sigmoid(logits)
    sel = scores + rb_ref[...]
    s = [sel[e:e + 1, :] for e in range(N_EXPERTS)]
    sc = [scores[e:e + 1, :] for e in range(N_EXPERTS)]

    def top2sum(v):
        best = v[0] + v[1]
        for a in range(4):
            for b in range(a + 1, 4):
                if (a, b) != (0, 1):
                    best = jnp.maximum(best, v[a] + v[b])
        return best

    gs = [top2sum(s[4 * g:4 * g + 4]) for g in range(N_GROUPS)]
    best, gi = gs[0], jnp.zeros((1, tr), I32)
    for g in range(1, N_GROUPS):
        better = gs[g] > best
        gi = jnp.where(better, g, gi)
        best = jnp.where(better, gs[g], best)

    def pick_group(rows):
        out = []
        for j in range(EXPERTS_PER_GROUP):
            v = rows[j]
            for g in range(1, N_GROUPS):
                v = jnp.where(gi == g, rows[4 * g + j], v)
            out.append(v)
        return out

    v = pick_group(s)
    raw = pick_group(sc)
    m1, l1, r1 = v[0], jnp.zeros((1, tr), I32), raw[0]
    for j in range(1, 4):
        better = v[j] > m1
        l1 = jnp.where(better, j, l1)
        m1 = jnp.where(better, v[j], m1)
        r1 = jnp.where(better, raw[j], r1)
    m2 = jnp.full((1, tr), -jnp.inf, F32)
    l2 = jnp.zeros((1, tr), I32)
    r2 = jnp.zeros((1, tr), F32)
    for j in range(4):
        better = jnp.logical_and(l1 != j, v[j] > m2)
        l2 = jnp.where(better, j, l2)
        m2 = jnp.where(better, v[j], m2)
        r2 = jnp.where(better, raw[j], r2)
    den = r1 + r2
    first_low = l1 < l2
    lo = jnp.where(first_low, l1, l2)
    hi = jnp.where(first_low, l2, l1)
    g_lo = jnp.where(first_low, r1, r2) / den
    g_hi = jnp.where(first_low, r2, r1) / den
    gate_ref[...] = jnp.concatenate([g_lo, g_hi], axis=0)
    bucket = gi * PAIRS_PER_GROUP + 2 * lo + hi - 1 - jnp.where(lo == 2, 1, 0)

    bidx = lax.broadcasted_iota(I32, (N_BUCKETS_PAD, tr), 0)
    hit = bidx == bucket
    oh = jnp.where(hit, 1.0, 0.0)
    rank = _dot(oh.astype(BF16), su_ref[...]) + base_ref[...]
    rk = jnp.sum(jnp.where(hit, rank, 0.0), axis=0, keepdims=True).astype(I32)
    idx_ref[...] = jnp.concatenate([bucket, rk], axis=0)
    base_ref[...] = base_ref[...] + jnp.sum(oh, axis=1, keepdims=True)
    cnt_ref[...] = jnp.broadcast_to(base_ref[...], cnt_ref.shape)


def moe_router(h, router_w, router_b, tr=512):
    T, D = h.shape
    su = jnp.triu(jnp.ones((tr, tr), BF16), k=1)
    idx, gate, cnt = pl.pallas_call(
        _router_kernel,
        grid=(T // tr,),
        in_specs=[pl.BlockSpec((tr, D), lambda i: (i, 0)),
                  pl.BlockSpec((N_EXPERTS, D), lambda i: (0, 0)),
                  pl.BlockSpec((N_EXPERTS, 1), lambda i: (0, 0)),
                  pl.BlockSpec((tr, tr), lambda i: (0, 0))],
        out_specs=[pl.BlockSpec((2, tr), lambda i: (0, i)),
                   pl.BlockSpec((2, tr), lambda i: (0, i)),
                   pl.BlockSpec((N_BUCKETS_PAD, LANES), lambda i: (0, 0))],
        out_shape=[jax.ShapeDtypeStruct((2, T), I32), jax.ShapeDtypeStruct((2, T), F32),
                   jax.ShapeDtypeStruct((N_BUCKETS_PAD, LANES), F32)],
        scratch_shapes=[pltpu.VMEM((N_BUCKETS_PAD, 1), F32)],
        compiler_params=_cparams("arbitrary"),
        name="moe_router",
    )(h, router_w.T, router_b.reshape(N_EXPERTS, 1), su)
    return idx, gate, cnt[:N_BUCKETS, 0].astype(I32)


def _dispatch_kernel(pos_ref, h_ref, gate_ref, z_ref, xs_ref, aug, sems):
    del z_ref
    i, n = pl.program_id(0), pl.num_programs(0)
    td, D = h_ref.shape
    slot = i % 2
    stage = aug.at[slot]
    stage[:, :D] = h_ref[...]
    stage[:, D:] = gate_ref[...]

    def row_copy(s, r, dst_row):
        return pltpu.make_async_copy(aug.at[s, pl.ds(r, 1)], xs_ref.at[pl.ds(dst_row, 1)], sems.at[s])

    def issue(g, c):
        for j in range(DMA_UNROLL):
            r = g * DMA_UNROLL + j
            row_copy(slot, r, pos_ref[r]).start(priority=j % 2)
        return c

    lax.fori_loop(0, td // DMA_UNROLL, issue, 0)

    def drain(s):
        pltpu.make_async_copy(aug.at[s], xs_ref.at[pl.ds(0, td)], sems.at[s]).wait()

    pl.when(i > 0)(lambda: drain(1 - slot))
    pl.when(i == n - 1)(lambda: drain(slot))


def moe_dispatch(h, pos, gcols, n_rows, td=512):
    T, D = h.shape
    W = D + GATE_LANES
    zeros = jnp.zeros((n_rows, W), h.dtype)
    return pl.pallas_call(
        _dispatch_kernel,
        grid=(T // td,),
        in_specs=[pl.BlockSpec((td,), lambda i: (i,), memory_space=pltpu.SMEM),
                  pl.BlockSpec((td, D), lambda i: (i, 0)),
                  pl.BlockSpec((td, GATE_LANES), lambda i: (i, 0)),
                  pl.BlockSpec(memory_space=pl.ANY)],
        out_specs=pl.BlockSpec(memory_space=pl.ANY),
        out_shape=jax.ShapeDtypeStruct((n_rows, W), h.dtype),
        scratch_shapes=[pltpu.VMEM((2, td, W), h.dtype), pltpu.SemaphoreType.DMA((2,))],
        input_output_aliases={3: 0},
        compiler_params=_cparams("arbitrary"),
        name="moe_dispatch",
    )(pos, h, gcols, zeros)


def _expert_kernel(ta_ref, tb_ref, nu_ref, x_ref, wga, wua, wda, wgb, wub, wdb, o_ref):
    del ta_ref, tb_ref
    D = o_ref.shape[1]

    @pl.when(pl.program_id(0) < nu_ref[0])
    def _():
        xa = x_ref[...]
        x = xa[:, :D].astype(BF16)

        weights = ((wga, wua, wda), (wgb, wub, wdb))
        st = [dict(), dict()]

        def s_in(c):
            st[c]["g"] = _dot(x, weights[c][0][0, 0])
            st[c]["u"] = _dot(x, weights[c][1][0, 0])

        def s_act(c):
            st[c]["he"] = (jax.nn.silu(st[c].pop("g")) * st[c].pop("u")).astype(BF16)

        def s_out(c):
            st[c]["y"] = _dot(st[c].pop("he"), weights[c][2][0, 0])

        _emit_skewed((s_in, s_act, s_out), 2)
        o_ref[...] = xa[:, D:D + 1] * st[0]["y"] + xa[:, D + 1:D + 2] * st[1]["y"]

    @pl.when(pl.program_id(0) >= nu_ref[0])
    def _():
        o_ref[...] = jnp.zeros_like(o_ref)


def moe_experts(xs, tile_ea, tile_eb, n_used, w_gate, w_up, w_down, layer):
    n_rows, W = xs.shape
    _, E, D, De = w_gate.shape
    n_tiles = n_rows // MOE_TILE

    def row_map(i, ta, tb, nu):
        return (jnp.maximum(jnp.minimum(i, nu[0] - 1), 0), 0)

    wa = lambda i, ta, tb, nu: (layer, ta[i], 0, 0)
    wb = lambda i, ta, tb, nu: (layer, tb[i], 0, 0)
    grid_spec = pltpu.PrefetchScalarGridSpec(
        num_scalar_prefetch=3,
        grid=(n_tiles,),
        in_specs=[pl.BlockSpec((MOE_TILE, W), row_map),
                  pl.BlockSpec((1, 1, D, De), wa), pl.BlockSpec((1, 1, D, De), wa),
                  pl.BlockSpec((1, 1, De, D), wa),
                  pl.BlockSpec((1, 1, D, De), wb), pl.BlockSpec((1, 1, D, De), wb),
                  pl.BlockSpec((1, 1, De, D), wb)],
        out_specs=pl.BlockSpec((MOE_TILE, D), lambda i, ta, tb, nu: (i, 0)),
    )
    return pl.pallas_call(
        _expert_kernel,
        grid_spec=grid_spec,
        out_shape=jax.ShapeDtypeStruct((n_rows, D), F32),
        compiler_params=_cparams("arbitrary"),
        name="moe_experts",
    )(tile_ea, tile_eb, n_used, xs, w_gate, w_up, w_down, w_gate, w_up, w_down)


def _combine_kernel(pos_ref, posn_ref, h_ref, lg_ref, lb_ref, os_ref, of_ref, ob_ref, buf, sems):
    i, n = pl.program_id(0), pl.num_programs(0)
    tc = h_ref.shape[0]
    slot = i % 2

    def row_copy(s, src_row, r):
        return pltpu.make_async_copy(os_ref.at[pl.ds(src_row, 1)], buf.at[s, pl.ds(r, 1)], sems.at[s])

    def fetch(p_ref, s):
        def body(g, c):
            for j in range(DMA_UNROLL):
                r = g * DMA_UNROLL + j
                row_copy(s, p_ref[r], r).start(priority=j % 2)
            return c
        lax.fori_loop(0, tc // DMA_UNROLL, body, 0)

    pl.when(i == 0)(lambda: fetch(pos_ref, slot))
    pl.when(i + 1 < n)(lambda: fetch(posn_ref, 1 - slot))

    pltpu.make_async_copy(os_ref.at[pl.ds(0, tc)], buf.at[slot], sems.at[slot]).wait()
    out = _layer_norm_rows(ALPHA * h_ref[...] + buf[slot], lg_ref[...], lb_ref[...])
    of_ref[...] = out
    ob_ref[...] = out.astype(BF16)


def moe_combine(osorted, pos, h, ln_g, ln_b, tc=512):
    T, D = h.shape
    nt = T // tc
    return pl.pallas_call(
        _combine_kernel,
        grid=(nt,),
        in_specs=[pl.BlockSpec((tc,), lambda i: (i,), memory_space=pltpu.SMEM),
                  pl.BlockSpec((tc,), lambda i: (jnp.minimum(i + 1, nt - 1),), memory_space=pltpu.SMEM),
                  pl.BlockSpec((tc, D), lambda i: (i, 0)),
                  pl.BlockSpec((1, D), lambda i: (0, 0)),
                  pl.BlockSpec((1, D), lambda i: (0, 0)),
                  pl.BlockSpec(memory_space=pl.ANY)],
        out_specs=[pl.BlockSpec((tc, D), lambda i: (i, 0)),
                   pl.BlockSpec((tc, D), lambda i: (i, 0))],
        out_shape=[jax.ShapeDtypeStruct((T, D), F32), jax.ShapeDtypeStruct((T, D), BF16)],
        scratch_shapes=[pltpu.VMEM((2, tc, D), F32), pltpu.SemaphoreType.DMA((2,))],
        compiler_params=_cparams("arbitrary"),
        name="moe_combine",
    )(pos, pos, h, ln_g.reshape(1, D), ln_b.reshape(1, D), osorted)


_PAIRS = ((0, 1), (0, 2), (0, 3), (1, 2), (1, 3), (2, 3))


def grouped_moe_ln(h, router_w, router_b, w_gate, w_up, w_down, ln_g, ln_b, layer):
    T, D = h.shape
    idx, gate, cnt = moe_router(h, router_w, router_b)
    bucket, rank = idx[0], idx[1]
    tiles = (cnt + MOE_TILE - 1) // MOE_TILE
    tile_end = jnp.cumsum(tiles)
    start = (tile_end - tiles) * MOE_TILE
    pos = start[bucket] + rank
    n_tiles = T // MOE_TILE + N_BUCKETS
    tile_ids = jnp.arange(n_tiles, dtype=I32)
    tile_bucket = jnp.minimum(
        jnp.sum((tile_ids[:, None] >= tile_end[None, :]).astype(I32), axis=1), N_BUCKETS - 1)
    ea = jnp.array([g * EXPERTS_PER_GROUP + a for g in range(N_GROUPS) for a, _ in _PAIRS], I32)
    eb = jnp.array([g * EXPERTS_PER_GROUP + b for g in range(N_GROUPS) for _, b in _PAIRS], I32)
    n_used = tile_end[-1:].astype(I32)
    gcols = jnp.pad(gate.T, ((0, 0), (0, GATE_LANES - 2)))
    xs = moe_dispatch(h, pos, gcols, n_tiles * MOE_TILE)
    osorted = moe_experts(xs, ea[tile_bucket], eb[tile_bucket], n_used, w_gate, w_up, w_down, layer)
    return moe_combine(osorted, pos, h, ln_g, ln_b)


def _forget_kernel(h_ref, wft_ref, bf_ref, ui_ref, c_ref, carry_ref):
    @pl.when(pl.program_id(1) == 0)
    def _():
        carry_ref[...] = jnp.zeros_like(carry_ref)

    logits = _dot_nt_split(wft_ref[...], h_ref[...])
    log_f = jax.nn.log_sigmoid(logits + bf_ref[...])
    hi, mid, lo = _split3(log_f)
    ui = ui_ref[...]
    cum = _dot(hi, ui) + _dot(mid, ui) + _dot(lo, ui) + carry_ref[...]
    c_ref[0] = cum * LOG2E
    carry_ref[...] = cum[:, -1:]


def fox_forget_cumsum(h, w_f, b_f, B, S, ts=512):
    T, D = h.shape
    H = w_f.shape[1]
    ui = jnp.triu(jnp.ones((ts, ts), BF16))
    ns = S // ts
    return pl.pallas_call(
        _forget_kernel,
        grid=(B, ns),
        in_specs=[pl.BlockSpec((ts, D), lambda b, i: (b * ns + i, 0)),
                  pl.BlockSpec((H, D), lambda b, i: (0, 0)),
                  pl.BlockSpec((H, 1), lambda b, i: (0, 0)),
                  pl.BlockSpec((ts, ts), lambda b, i: (0, 0))],
        out_specs=pl.BlockSpec((1, H, ts), lambda b, i: (b, 0, i)),
        out_shape=jax.ShapeDtypeStruct((B, H, S), F32),
        scratch_shapes=[pltpu.VMEM((H, 1), F32)],
        compiler_params=_cparams("parallel", "arbitrary"),
        name="fox_forget",
    )(h, w_f.T, b_f.reshape(H, 1), ui)


def _head_pair_masks(q):
    lane = lax.broadcasted_iota(I32, q.shape, 1)
    zero = jnp.zeros_like(q)
    return jnp.where(lane < HEAD_DIM, q, zero), jnp.where(lane >= HEAD_DIM, q, zero)


def _lane_tile(x, n):
    return x if n == 1 else jnp.concatenate([x] * n, axis=1)


def _causal_steps(nq, ratio, descending):
    qi, kb = [], []
    for i in range(nq):
        ks = list(range(ratio * (i + 1)))
        for k in (reversed(ks) if descending else ks):
            qi.append(i)
            kb.append(k)
    return jnp.array(qi, I32), jnp.array(kb, I32)


def _emit_skewed(stages, n_chains):
    for t in range(n_chains + len(stages) - 1):
        for s in reversed(range(len(stages))):
            if 0 <= t - s < n_chains:
                stages[s](t - s)


def _pair(p):
    return slice(p * LANES, (p + 1) * LANES)


def _fox_kernel(qi_ref, kb_ref, q_ref, k_ref, v_ref, crow_ref, ccol_ref, og_ref, o_ref, qm, ct, m, acc):
    hg, step_id = pl.program_id(1), pl.program_id(2)
    i, j = qi_ref[step_id], kb_ref[step_id]
    tq, tk = q_ref.shape[1], k_ref.shape[1]
    n_heads = qm.shape[0]

    @pl.when(j == 0)
    def _():
        q = q_ref[0]
        cc = ccol_ref[0]
        head = lax.broadcasted_iota(I32, cc.shape, 1)
        for hh in range(n_heads):
            qm[hh] = _head_pair_masks(q[:, _pair(hh // 2)])[hh % 2]
            col = jnp.sum(jnp.where(head == n_heads * hg + hh, cc, 0.0), axis=1, keepdims=True)
            ct[hh] = jnp.broadcast_to(col, (tq, LANES))
        m[...] = jnp.full(m.shape, -jnp.inf, F32)
        acc[...] = jnp.zeros_like(acc)

    def step(masked):
        ks = [k_ref[0, :, _pair(p)] for p in range(n_heads // 2)]
        ones = jnp.ones((tk, LANES), BF16)
        v_ones = [jnp.concatenate([v_ref[0, :, _pair(p)], ones], axis=1) for p in range(n_heads // 2)]
        crow = crow_ref[0]
        if masked:
            row = lax.broadcasted_iota(I32, (tq, tk), 0)
            col = lax.broadcasted_iota(I32, (tq, tk), 1)
            keep = col <= row
        rp = tq // ATT_ROW_PARTS
        chains = [(hh, slice(p * rp, (p + 1) * rp)) for hh in range(n_heads) for p in range(ATT_ROW_PARTS)]
        st = [dict() for _ in chains]

        def s_scores(c):
            hh, rows = chains[c]
            st[c]["s"] = _dot_nt(qm[hh, rows, :], ks[hh // 2])

        def s_softmax(c):
            hh, rows = chains[c]
            s = st[c].pop("s") - crow[hh // 2, hh % 2:hh % 2 + 1, :]
            if masked:
                s = jnp.where(keep[rows, :], s, -jnp.inf)
            c_t = ct[hh, rows, :]
            m_prev = m[hh, rows, :]
            m_new = jnp.maximum(m_prev, jnp.max(s, axis=1, keepdims=True) + c_t)
            st[c]["alpha"] = jnp.exp2(m_prev - m_new)
            st[c]["p"] = jnp.exp2(s - _lane_tile(m_new - c_t, tk // LANES)).astype(BF16)
            m[hh, rows, :] = m_new

        def s_values(c):
            hh, rows = chains[c]
            acc[hh, rows, :] = (_lane_tile(st[c].pop("alpha"), 2) * acc[hh, rows, :]
                                + _dot(st[c].pop("p"), v_ones[hh // 2]))

        _emit_skewed((s_scores, s_softmax, s_values), len(chains))

    pl.when(j < i)(lambda: step(False))
    pl.when(j == i)(lambda: step(True))

    @pl.when(j == i)
    def _():
        lane = lax.broadcasted_iota(I32, (tq, LANES), 1)
        for p in range(n_heads // 2):
            a0, a1 = acc[2 * p], acc[2 * p + 1]
            o = jnp.where(lane < HEAD_DIM, a0[:, :LANES] / a0[:, LANES:], a1[:, :LANES] / a1[:, LANES:])
            o_ref[0, :, _pair(p)] = (o * jax.nn.sigmoid(og_ref[0, :, _pair(p)].astype(F32))).astype(BF16)


def fox_attention(qkv, og, c, B, S, tq=512):
    T = qkv.shape[0]
    D = N_HEADS * HEAD_DIM
    hp_n = N_HEADS // 2
    nq = S // tq
    qkv3 = qkv.reshape(B, S, 3 * D)
    crow = c.reshape(B, hp_n, 2, S)
    ccol = jnp.transpose(c, (0, 2, 1))
    qi, kb = _causal_steps(nq, 1, descending=False)
    w = ATT_PAIRS * LANES
    hg_n = hp_n // ATT_PAIRS
    nh = 2 * ATT_PAIRS
    q_blk = lambda b, hg, s, qi, kb: (b, qi[s], hg)
    kv_blk = lambda off: (lambda b, hg, s, qi, kb: (b, kb[s], off + hg))
    grid_spec = pltpu.PrefetchScalarGridSpec(
        num_scalar_prefetch=2,
        grid=(B, hg_n, qi.shape[0]),
        in_specs=[pl.BlockSpec((1, tq, w), q_blk),
                  pl.BlockSpec((1, tq, w), kv_blk(hg_n)),
                  pl.BlockSpec((1, tq, w), kv_blk(2 * hg_n)),
                  pl.BlockSpec((1, ATT_PAIRS, 2, tq), lambda b, hg, s, qi, kb: (b, hg, 0, kb[s])),
                  pl.BlockSpec((1, tq, N_HEADS), lambda b, hg, s, qi, kb: (b, qi[s], 0)),
                  pl.BlockSpec((1, tq, w), q_blk)],
        out_specs=pl.BlockSpec((1, tq, w), q_blk),
        scratch_shapes=[pltpu.VMEM((nh, tq, LANES), BF16),
                        pltpu.VMEM((nh, tq, LANES), F32), pltpu.VMEM((nh, tq, LANES), F32),
                        pltpu.VMEM((nh, tq, 2 * LANES), F32)],
    )
    out = pl.pallas_call(
        _fox_kernel,
        grid_spec=grid_spec,
        out_shape=jax.ShapeDtypeStruct((B, S, D), BF16),
        compiler_params=_cparams("parallel", "parallel", "arbitrary"),
        name="fox_attention",
    )(qi, kb, qkv3, qkv3, qkv3, crow, ccol, og.reshape(B, S, D))
    return out.reshape(T, D)


def _sb_kernel(qi_ref, kb_ref, q_ref, k_ref, v_ref, u_ref, o_ref, qm, rs, acc):
    step_id = pl.program_id(2)
    i, kb = qi_ref[step_id], kb_ref[step_id]
    tq, tk = q_ref.shape[1], k_ref.shape[1]
    ratio = tq // tk

    n_heads = qm.shape[0]

    @pl.when(kb == ratio * (i + 1) - 1)
    def _():
        q = q_ref[0]
        for hh in range(n_heads):
            qm[hh] = _head_pair_masks(q[:, _pair(hh // 2)])[hh % 2]
        rs[...] = jnp.zeros_like(rs)
        acc[...] = jnp.zeros_like(acc)

    def step(masked):
        ks = [k_ref[0, :, _pair(p)] for p in range(n_heads // 2)]
        vs = [v_ref[0, :, _pair(p)] for p in range(n_heads // 2)]
        u = u_ref[...]
        if masked:
            row = lax.broadcasted_iota(I32, (tq, tk), 0)
            col = lax.broadcasted_iota(I32, (tq, tk), 1)
            keep = col - row < i * tq - kb * tk
        rp = tq // ATT_ROW_PARTS
        chains = [(hh, slice(p * rp, (p + 1) * rp)) for hh in range(n_heads) for p in range(ATT_ROW_PARTS)]
        st = [dict() for _ in chains]

        def s_scores(c):
            x, rows = chains[c]
            st[c]["z"] = _dot_nt(qm[x, rows, :], ks[x // 2])

        def s_logs(c):
            x, rows = chains[c]
            z = st[c]["z"]
            nl = jnp.maximum(z, 0.0) + jnp.log(1.0 + jnp.exp2(jnp.abs(z) * (-LOG2E)))
            if masked:
                nl = jnp.where(keep[rows, :], nl, 0.0)
            st[c].update(nl=nl.astype(BF16), total=jnp.sum(nl, axis=1, keepdims=True))

        def s_suffix(c):
            st[c]["ci"] = _dot(st[c].pop("nl"), u)

        def s_weights(c):
            x, rows = chains[c]
            prev = rs[x, rows, :]
            a = jnp.exp2((st[c].pop("z") - st[c].pop("ci") - _lane_tile(prev, tk // LANES)) * LOG2E)
            if masked:
                a = jnp.where(keep[rows, :], a, 0.0)
            st[c]["a"] = a.astype(BF16)
            rs[x, rows, :] = prev + st[c].pop("total")

        def s_values(c):
            x, rows = chains[c]
            acc[x, rows, :] = acc[x, rows, :] + _dot(st[c].pop("a"), vs[x // 2])

        _emit_skewed((s_scores, s_logs, s_suffix, s_weights, s_values), len(chains))

    pl.when(kb >= ratio * i)(lambda: step(True))
    pl.when(kb < ratio * i)(lambda: step(False))

    @pl.when(kb == 0)
    def _():
        lane = lax.broadcasted_iota(I32, (tq, LANES), 1)
        for p in range(n_heads // 2):
            o_ref[0, :, _pair(p)] = jnp.where(lane < HEAD_DIM, acc[2 * p], acc[2 * p + 1]).astype(BF16)


def stick_breaking_attention(qkv, B, S, tq=512, tk=256):
    T = qkv.shape[0]
    D = N_HEADS * HEAD_DIM
    hp_n = N_HEADS // 2
    nq = S // tq
    qkv3 = qkv.reshape(B, S, 3 * D)
    u = jnp.tril(jnp.ones((tk, tk), BF16))
    qi, kb = _causal_steps(nq, tq // tk, descending=True)
    w = ATT_PAIRS * LANES
    hg_n = hp_n // ATT_PAIRS
    nh = 2 * ATT_PAIRS
    q_blk = lambda b, hg, s, qi, kb: (b, qi[s], hg)
    kv_blk = lambda off: (lambda b, hg, s, qi, kb: (b, kb[s], off + hg))
    grid_spec = pltpu.PrefetchScalarGridSpec(
        num_scalar_prefetch=2,
        grid=(B, hg_n, qi.shape[0]),
        in_specs=[pl.BlockSpec((1, tq, w), q_blk),
                  pl.BlockSpec((1, tk, w), kv_blk(hg_n)),
                  pl.BlockSpec((1, tk, w), kv_blk(2 * hg_n)),
                  pl.BlockSpec((tk, tk), lambda b, hg, s, qi, kb: (0, 0))],
        out_specs=pl.BlockSpec((1, tq, w), q_blk),
        scratch_shapes=[pltpu.VMEM((nh, tq, LANES), BF16),
                        pltpu.VMEM((nh, tq, LANES), F32), pltpu.VMEM((nh, tq, LANES), F32)],
    )
    out = pl.pallas_call(
        _sb_kernel,
        grid_spec=grid_spec,
        out_shape=jax.ShapeDtypeStruct((B, S, D), BF16),
        compiler_params=_cparams("parallel", "parallel", "arbitrary"),
        name="stick_breaking",
    )(qi, kb, qkv3, qkv3, qkv3, u)
    return out.reshape(T, D)


def _gmlp_kernel(h_ref, win_ref, bin_ref, lg_ref, lb_ref, ws_ref, bst_ref, o_ref):
    tm = h_ref.shape[0]
    dg = o_ref.shape[1]
    row = lax.broadcasted_iota(I32, (CHUNK, CHUNK), 0)
    col = lax.broadcasted_iota(I32, (CHUNK, CHUNK), 1)
    gw = dg // GMLP_GROUPS
    wcs = [jnp.where(col <= row, ws_ref[g], 0.0).astype(BF16) for g in range(GMLP_GROUPS)]
    rp = tm // GMLP_ROW_PARTS
    parts = [slice(p * rp, (p + 1) * rp) for p in range(GMLP_ROW_PARTS)]
    zs = [None] * GMLP_ROW_PARTS

    def s_proj(p):
        zs[p] = _dot(h_ref[parts[p], :], win_ref[...])

    def s_gate(p):
        z = jax.nn.gelu(zs[p] + bin_ref[...])
        u = z[:, :dg]
        v = _layer_norm_rows(z[:, dg:], lg_ref[...], lb_ref[...]).astype(BF16)
        for g in range(GMLP_GROUPS):
            cs = slice(g * gw, (g + 1) * gw)
            for c in range(rp // CHUNK):
                rs = slice(c * CHUNK, (c + 1) * CHUNK)
                sv = _dot(wcs[g], v[rs, cs]) + bst_ref[:, g:g + 1]
                o_ref[pl.ds(p * rp + c * CHUNK, CHUNK), cs] = (u[rs, cs] * sv).astype(BF16)

    _emit_skewed((s_proj, s_gate), GMLP_ROW_PARTS)


def gmlp_gate(hb, w_in, b_in, ln_g, ln_b, w_s, b_s, tm=512):
    T, D = hb.shape
    dg = w_in.shape[1] // 2
    return pl.pallas_call(
        _gmlp_kernel,
        grid=(T // tm,),
        in_specs=[pl.BlockSpec((tm, D), lambda i: (i, 0)),
                  pl.BlockSpec((D, 2 * dg), lambda i: (0, 0)),
                  pl.BlockSpec((1, 2 * dg), lambda i: (0, 0)),
                  pl.BlockSpec((1, dg), lambda i: (0, 0)),
                  pl.BlockSpec((1, dg), lambda i: (0, 0)),
                  pl.BlockSpec((GMLP_GROUPS, CHUNK, CHUNK), lambda i: (0, 0, 0)),
                  pl.BlockSpec((CHUNK, GMLP_GROUPS), lambda i: (0, 0))],
        out_specs=pl.BlockSpec((tm, dg), lambda i: (i, 0)),
        out_shape=jax.ShapeDtypeStruct((T, dg), BF16),
        compiler_params=_cparams("parallel"),
        name="gmlp_gate",
    )(hb, w_in, b_in.reshape(1, -1), ln_g.reshape(1, -1), ln_b.reshape(1, -1), w_s, b_s.T)


RW_LANES = 4 * HEAD_DIM


def _seg_sum(x, bones):
    hi = x.astype(BF16)
    lo = (x - hi.astype(F32)).astype(BF16)
    return _dot(hi, bones) + _dot(lo, bones)


def _rwkv_prep_kernel(seq_len, h_ref, hp_ref, mu_ref, wrkv_ref, w1_ref, w2_ref, a1_ref, a2_ref,
                      g1_ref, g2_ref, w0_ref, a0_ref, kk_ref, ka_ref, bones_ref, lchunk_ref,
                      r_ref, cl_ref, clx_ref, k_ref, v_ref, kkn_ref, b_ref, g_ref):
    i = pl.program_id(0)
    tm, D = h_ref.shape
    h = h_ref[...]
    at_start = (i * tm) % seq_len == 0
    prev_row = jnp.where(at_start, 0.0, hp_ref[7:8, :])
    row = lax.broadcasted_iota(I32, (tm, D), 0)
    h_prev = jnp.where(row == 0, prev_row, pltpu.roll(h, 1, axis=0))
    dx = h_prev - h

    def mixed(c):
        return (h + dx * mu_ref[c:c + 1, :]).astype(BF16)

    r = _dot(mixed(0), wrkv_ref[0])
    k = _dot(mixed(1), wrkv_ref[1])
    v = _dot(mixed(2), wrkv_ref[2])
    d = w0_ref[...] + _dot(jnp.tanh(_dot(mixed(3), w1_ref[...])).astype(BF16), w2_ref[...])
    lw = -jnp.exp(-jax.nn.softplus(-d) - 0.5)
    a = jax.nn.sigmoid(a0_ref[...] + _dot(_dot(mixed(4), a1_ref[...]).astype(BF16), a2_ref[...]))
    g = _dot(jax.nn.sigmoid(_dot(mixed(5), g1_ref[...])).astype(BF16), g2_ref[...])
    kk = k * kk_ref[...]
    bones = bones_ref[...]
    norm_sq = jnp.concatenate(
        [_seg_sum(jnp.square(kk[:, q * RW_LANES:(q + 1) * RW_LANES]), bones) for q in range(D // RW_LANES)],
        axis=1)
    kkn = kk / jnp.maximum(jnp.sqrt(norm_sq), 1e-12)
    hi, mid, lo = _split3(lw)
    lchunk = lchunk_ref[...]
    cl = _dot(lchunk, hi) + _dot(lchunk, mid) + _dot(lchunk, lo)
    r_ref[...] = r.astype(BF16)
    cl_ref[...] = cl
    clx_ref[...] = cl - lw
    k_ref[...] = (k * (1.0 + (a - 1.0) * ka_ref[...])).astype(BF16)
    v_ref[...] = v.astype(BF16)
    kkn_ref[...] = kkn.astype(BF16)
    b_ref[...] = (kkn * a).astype(BF16)
    g_ref[...] = g.astype(BF16)


def _block_ones():
    idx = jnp.arange(RW_LANES) // HEAD_DIM
    return (idx[:, None] == idx[None, :]).astype(BF16)


def rwkv_prep(h, mu, w_rkv, w0, w1, w2, a0, a1, a2, g1, g2, k_k, k_a, S, tm=256):
    T, D = h.shape
    row = lambda i: (i, 0)
    full2 = lambda i: (0, 0)
    vec = lambda x: x.reshape(1, D)
    bf = lambda x: x.astype(BF16)
    pos = jnp.arange(tm)
    lchunk = jnp.logical_and(pos[:, None] // RW_CHUNK == pos[None, :] // RW_CHUNK,
                             pos[None, :] <= pos[:, None]).astype(BF16)
    outs = pl.pallas_call(
        functools.partial(_rwkv_prep_kernel, S),
        grid=(T // tm,),
        in_specs=[pl.BlockSpec((tm, D), row),
                  pl.BlockSpec((8, D), lambda i: (jnp.maximum(i * (tm // 8) - 1, 0), 0)),
                  pl.BlockSpec((6, D), full2),
                  pl.BlockSpec((3, D, D), lambda i: (0, 0, 0)),
                  pl.BlockSpec(w1.shape, full2), pl.BlockSpec(w2.shape, full2),
                  pl.BlockSpec(a1.shape, full2), pl.BlockSpec(a2.shape, full2),
                  pl.BlockSpec(g1.shape, full2), pl.BlockSpec(g2.shape, full2),
                  pl.BlockSpec((1, D), full2), pl.BlockSpec((1, D), full2),
                  pl.BlockSpec((1, D), full2), pl.BlockSpec((1, D), full2),
                  pl.BlockSpec((RW_LANES, RW_LANES), full2),
                  pl.BlockSpec((tm, tm), full2)],
        out_specs=[pl.BlockSpec((tm, D), row)] * 8,
        out_shape=[jax.ShapeDtypeStruct((T, D), dt) for dt in (BF16, F32, F32, BF16, BF16, BF16, BF16, BF16)],
        compiler_params=_cparams("parallel"),
        name="rwkv_prep",
    )(h, h, mu, bf(w_rkv), bf(w1), bf(w2), bf(a1), bf(a2), bf(g1), bf(g2),
      vec(w0), vec(a0), vec(k_k), vec(k_a), _block_ones(), lchunk)
    return outs


def _rwkv_local(r, cl, clx, k, v, kk, b, cst):
    C = RW_CHUNK
    mbd, eye, strict, incl, levels = cst
    eye_f = jnp.where(eye, 1.0, 0.0)
    eye_b = eye_f.astype(BF16)

    def each(f, *lists):
        return [f(*xs) for xs in zip(*lists)]

    def bd(y):
        return jnp.where(mbd, jnp.concatenate([y.astype(BF16)] * 4, axis=0), jnp.zeros((), BF16))

    def rows(*parts):
        return jnp.concatenate([p.astype(BF16) for p in parts], axis=0)

    cl_last = each(lambda c: c[C - 1:C, :], cl)
    kkt = each(lambda a, c: a * jnp.exp(c), kk, clx)
    rt = each(lambda a, c: a * jnp.exp(c), r, cl)
    g_inv = each(lambda c: jnp.exp(-c), cl)
    g_tail = each(lambda cl_, c: jnp.exp(cl_ - c), cl_last, cl)
    kk_r = each(rows, kkt, rt)
    ab_k = each(lambda l, a, g: _dot_nt(l, bd(a * g)), kk_r, k, g_inv)
    ab_b = each(lambda l, a, g: _dot_nt(l, bd(a * g)), kk_r, b, g_inv)
    a_k = each(lambda x: jnp.where(strict, x[:C], 0.0), ab_k)
    b_k = each(lambda x: jnp.where(incl, x[C:], 0.0), ab_k)
    a_b = each(lambda x: jnp.where(strict, x[:C], 0.0), ab_b)
    b_b = each(lambda x: jnp.where(incl, x[C:], 0.0), ab_b)
    kh_t = each(lambda a, g: _dot_nt(eye_b, bd(a * g)), k, g_tail)
    bh_t = each(lambda a, g: _dot_nt(eye_b, bd(a * g)), b, g_tail)

    dm = each(lambda a: eye_f - jnp.where(levels[0], a, 0.0), a_b)
    for lvl in levels[1:]:
        w = each(lambda d, a: _dot(d.astype(BF16), bd(jnp.where(lvl, a, 0.0))), dm, a_b)
        dm = each(lambda d, w_: d - _dot(w_.astype(BF16), bd(d)), dm, w)
    tm_b = each(lambda d: d.astype(BF16), dm)

    xv = each(lambda a, b_, c, v_: _dot(rows(a, b_, c), bd(v_)), a_k, b_k, kh_t, v)
    kp = each(lambda t, a: _dot(t, bd(a)), tm_b, kkt)
    vp = each(lambda t, x: _dot(t, bd(x[:C])), tm_b, xv)
    bb_bh = each(rows, b_b, bh_t)
    xkp = each(lambda l, a: _dot(l, bd(a)), bb_bh, kp)
    xvp = each(lambda l, a: _dot(l, bd(a)), bb_bh, vp)
    y_loc = each(lambda x, z: x[C:2 * C] - z[:C], xv, xvp)
    r_eff = each(lambda a, z: a - z[:C], rt, xkp)
    m = each(lambda c, z: eye_f * jnp.exp(c) - z[C:], cl_last, xkp)
    n = each(lambda x, z: x[2 * C:] - z[C:], xv, xvp)
    return r_eff, y_loc, m, n


def _rwkv_apply(r_eff, y_loc, m, n, P, mbd):
    C = RW_CHUNK

    def each(f, *lists):
        return [f(*xs) for xs in zip(*lists)]

    def bd(y):
        return jnp.where(mbd, jnp.concatenate([y.astype(BF16)] * 4, axis=0), jnp.zeros((), BF16))

    def rows(*parts):
        return jnp.concatenate([p.astype(BF16) for p in parts], axis=0)

    p_hi = each(lambda p: p.astype(BF16), P)
    m_hi = each(lambda a: a.astype(BF16), m)
    z_hi = each(lambda a, mh, m_, ph: _dot(rows(a, mh, m_ - mh.astype(F32)), bd(ph)), r_eff, m_hi, m, p_hi)
    z_lo = each(lambda a, mh, p, ph: _dot(rows(a, mh), bd(p - ph.astype(F32))), r_eff, m_hi, P, p_hi)
    y = each(lambda zh, zl, yl: zh[:C] + zl[:C] + yl, z_hi, z_lo, y_loc)
    p_new = each(lambda zh, zl, n_: zh[C:2 * C] + zh[2 * C:] + zl[C:] + n_, z_hi, z_lo, n)
    return y, p_new


def _rwkv_scan_kernel(r_ref, cl_ref, clx_ref, k_ref, v_ref, kk_ref, b_ref, g_ref, rk_ref, gg_ref, gb_ref,
                      bones_ref, o_ref, state):
    C = RW_CHUNK

    @pl.when(pl.program_id(1) == 0)
    def _():
        state[...] = jnp.zeros_like(state)

    t = lax.broadcasted_iota(I32, (C, RW_LANES), 0)
    i = lax.broadcasted_iota(I32, (C, RW_LANES), 1) % C
    levels = []
    for sh in range(6):
        tb, ib = t >> sh, i >> sh
        levels.append(jnp.logical_and(tb == ib + 1, (tb & 1) == 1))
    rr = lax.broadcasted_iota(I32, (RW_LANES, RW_LANES), 0) // HEAD_DIM
    cc = lax.broadcasted_iota(I32, (RW_LANES, RW_LANES), 1) // HEAD_DIM
    cst = (rr == cc, i == t, i < t, i <= t, levels)
    bones = bones_ref[...]

    n_groups = r_ref.shape[1] // RW_LANES
    n_chunks = r_ref.shape[0] // C
    tiles = [(slice(c * C, (c + 1) * C), slice(q * RW_LANES, (q + 1) * RW_LANES))
             for c in range(n_chunks) for q in range(n_groups)]
    cut = lambda ref: [ref[rs, ls] for rs, ls in tiles]
    r, k, v = cut(r_ref), cut(k_ref), cut(v_ref)
    r_eff, y_loc, m, n = _rwkv_local(r, cut(cl_ref), cut(clx_ref), k, v, cut(kk_ref), cut(b_ref), cst)
    p = [state[q] for q in range(n_groups)]
    ys = []
    for c in range(n_chunks):
        part = slice(c * n_groups, (c + 1) * n_groups)
        y_c, p = _rwkv_apply(r_eff[part], y_loc[part], m[part], n[part], p, cst[0])
        ys += y_c
    for q in range(n_groups):
        state[q] = p[q]
    stack = lambda xs: jnp.concatenate(xs, axis=0)
    y = stack(ys)
    mean = _seg_sum(y, bones) * (1.0 / HEAD_DIM)
    yc = y - mean
    var = _seg_sum(yc * yc, bones) * (1.0 / HEAD_DIM)
    yn = yc * lax.rsqrt(var + GN_EPS)
    rkr = stack([r[idx].astype(F32) * k[idx].astype(F32) * rk_ref[:, ls]
                 for idx, (rs, ls) in enumerate(tiles)])
    dots = _seg_sum(rkr, bones)
    for idx, (rs, ls) in enumerate(tiles):
        part = slice(idx * C, (idx + 1) * C)
        out = yn[part] * gg_ref[:, ls] + gb_ref[:, ls] + dots[part] * v[idx]
        o_ref[rs, ls] = (out * g_ref[rs, ls]).astype(BF16)


def rwkv_scan(r, cl, clx, k, v, kkn, b, g, r_k, gn_g, gn_b, B, S):
    T, D = r.shape
    rows_per_step = RW_CHUNK * RW_CHUNKS_PER_STEP
    nc = S // rows_per_step
    row = lambda bi, ci: (bi * nc + ci, 0)
    full2 = lambda bi, ci: (0, 0)
    return pl.pallas_call(
        _rwkv_scan_kernel,
        grid=(B, nc),
        in_specs=[pl.BlockSpec((rows_per_step, D), row)] * 8 + [pl.BlockSpec((1, D), full2)] * 3
        + [pl.BlockSpec((RW_LANES, RW_LANES), full2)],
        out_specs=pl.BlockSpec((rows_per_step, D), row),
        out_shape=jax.ShapeDtypeStruct((T, D), BF16),
        scratch_shapes=[pltpu.VMEM((D // RW_LANES, HEAD_DIM, RW_LANES), F32)],
        compiler_params=_cparams("parallel", "arbitrary"),
        name="rwkv_scan",
    )(r, cl, clx, k, v, kkn, b, g, r_k.reshape(1, D), gn_g.reshape(1, D), gn_b.reshape(1, D),
      _block_ones())


def rwkv_layer(h, mu, w_rkv, w0, w1, w2, a0, a1, a2, g1, g2, k_k, k_a, r_k, gn_g, gn_b, w_out,
               ln_g, ln_b, B, S):
    r, cl, clx, k, v, kkn, b, g = rwkv_prep(h, mu, w_rkv, w0, w1, w2, a0, a1, a2, g1, g2, k_k, k_a, S)
    y = rwkv_scan(r, cl, clx, k, v, kkn, b, g, r_k, gn_g, gn_b, B, S)
    return proj_ln(y, w_out.astype(BF16), h, ln_g, ln_b)


def fox_layer(h, hb, w_in, b_f, w_out, ln_g, ln_b, B, S):
    D = h.shape[1]
    scale = HEAD_DIM ** -0.5 * LOG2E
    w_qkv = jnp.concatenate([w_in[:, :D] * scale, w_in[:, D:3 * D]], axis=1).astype(BF16)
    w_f = w_in[:, 3 * D:3 * D + N_HEADS]
    w_og = w_in[:, 3 * D + N_HEADS:].astype(BF16)
    qkv = linear(hb, w_qkv, BF16)
    og = linear(hb, w_og, BF16)
    c = fox_forget_cumsum(h, w_f, b_f, B, S)
    o = fox_attention(qkv, og, c, B, S)
    return proj_ln(o, w_out.astype(BF16), h, ln_g, ln_b)


def sb_layer(h, hb, w_in, w_out, ln_g, ln_b, B, S):
    D = h.shape[1]
    scale = HEAD_DIM ** -0.5
    w_qkv = jnp.concatenate([w_in[:, :D] * scale, w_in[:, D:]], axis=1).astype(BF16)
    qkv = linear(hb, w_qkv, BF16)
    o = stick_breaking_attention(qkv, B, S)
    return proj_ln(o, w_out.astype(BF16), h, ln_g, ln_b)


def gmlp_layer(h, hb, w_in, b_in, gm_ln_g, gm_ln_b, w_s, b_s, w_out, ln_g, ln_b):
    y = gmlp_gate(hb, w_in.astype(BF16), b_in, gm_ln_g, gm_ln_b, w_s, b_s)
    return proj_ln(y, w_out.astype(BF16), h, ln_g, ln_b)


def kernel(x, ln1_g, ln1_b, ln2_g, ln2_b, fox_w_in, fox_b_f, fox_w_out, gm_w_in, gm_b_in, gm_ln_g,
           gm_ln_b, gm_w_s, gm_b_s, gm_w_out, sb_w_in, sb_w_out, rw_mu, rw_w_rkv, rw_w0, rw_w1, rw_w2,
           rw_a0, rw_a1, rw_a2, rw_g1, rw_g2, rw_k_k, rw_k_a, rw_r_k, rw_gn_g, rw_gn_b, rw_w_out,
           router_w, router_b, moe_w_gate, moe_w_up, moe_w_down):
    B, S, D = x.shape
    h = x.reshape(B * S, D)
    hb = h.astype(BF16)
    w_gate_b, w_up_b, w_down_b = (w.astype(BF16) for w in (moe_w_gate, moe_w_up, moe_w_down))
    for i in range(DEPTH):
        kind, j = i % 4, i // 4
        if kind == 0:
            h, hb = fox_layer(h, hb, fox_w_in[j], fox_b_f[j], fox_w_out[j], ln1_g[i], ln1_b[i], B, S)
        elif kind == 1:
            h, hb = gmlp_layer(h, hb, gm_w_in[j], gm_b_in[j], gm_ln_g[j], gm_ln_b[j], gm_w_s[j],
                               gm_b_s[j], gm_w_out[j], ln1_g[i], ln1_b[i])
        elif kind == 2:
            h, hb = sb_layer(h, hb, sb_w_in[j], sb_w_out[j], ln1_g[i], ln1_b[i], B, S)
        else:
            h, hb = rwkv_layer(h, rw_mu[j], rw_w_rkv[j], rw_w0[j], rw_w1[j], rw_w2[j], rw_a0[j],
                               rw_a1[j], rw_a2[j], rw_g1[j], rw_g2[j], rw_k_k[j], rw_k_a[j], rw_r_k[j],
                               rw_gn_g[j], rw_gn_b[j], rw_w_out[j], ln1_g[i], ln1_b[i], B, S)
        h, hb = grouped_moe_ln(h, router_w, router_b, w_gate_b, w_up_b, w_down_b, ln2_g[i], ln2_b[i], i)
    return h.reshape(B, S, D)
```

```python
import functools

import jax
import jax.numpy as jnp
from jax import lax
from jax.experimental import pallas as pl
from jax.experimental.pallas import tpu as pltpu

F32 = jnp.float32
BF16 = jnp.bfloat16
I32 = jnp.int32

N_HEADS = 16
HEAD_DIM = 64
CHUNK = 128
GMLP_GROUPS = 8
N_EXPERTS = 16
N_GROUPS = 4
EXPERTS_PER_GROUP = 4
PAIRS_PER_GROUP = 6
N_BUCKETS = N_GROUPS * PAIRS_PER_GROUP
N_BUCKETS_PAD = 32
PROJ_ROW_PARTS = 2
GMLP_ROW_PARTS = 2
DMA_UNROLL = 8
GATE_LANES = 128
DEPTH = 4
ALPHA = (2 * DEPTH) ** 0.25
LN_EPS = 1e-5
GN_EPS = 64e-5
LOG2E = 1.4426950408889634

VMEM_LIMIT_BYTES = 48 * 1024 * 1024
MOE_TILE = 256
RW_CHUNK = 64
RW_CHUNKS_PER_STEP = 4
LANES = 128
ATT_PAIRS = 4
ATT_ROW_PARTS = 1


def _cparams(*sem):
    return pltpu.CompilerParams(dimension_semantics=sem, vmem_limit_bytes=VMEM_LIMIT_BYTES)


def _layer_norm_rows(y, g, b):
    mu = jnp.mean(y, axis=-1, keepdims=True)
    yc = y - mu
    var = jnp.mean(yc * yc, axis=-1, keepdims=True)
    return yc * lax.rsqrt(var + LN_EPS) * g + b


def _split3(x):
    hi = x.astype(BF16)
    r1 = x - hi.astype(F32)
    mid = r1.astype(BF16)
    lo = (r1 - mid.astype(F32)).astype(BF16)
    return hi, mid, lo


def _dot(a, b):
    return jnp.dot(a, b, preferred_element_type=F32)


def _dot_nt(a, b):
    return lax.dot_general(a, b, (((1,), (1,)), ((), ())), preferred_element_type=F32)


def _dot_nt_split(w, x):
    n = w.shape[0]
    w_hi = w.astype(BF16)
    w_lo = (w - w_hi.astype(F32)).astype(BF16)
    x_hi = x.astype(BF16)
    x_lo = (x - x_hi.astype(F32)).astype(BF16)
    both = _dot_nt(jnp.concatenate([w_hi, w_lo], axis=0), x_hi)
    return both[:n] + both[n:] + _dot_nt(w_hi, x_lo)


def _linear_kernel(x_ref, w_ref, o_ref):
    o_ref[...] = _dot(x_ref[...], w_ref[...]).astype(o_ref.dtype)


def linear(x, w, out_dtype, tm=1024, tn=1024):
    M, K = x.shape
    N = w.shape[1]
    tn = min(tn, N)
    return pl.pallas_call(
        _linear_kernel,
        grid=(M // tm, N // tn),
        in_specs=[pl.BlockSpec((tm, K), lambda i, j: (i, 0)),
                  pl.BlockSpec((K, tn), lambda i, j: (0, j))],
        out_specs=pl.BlockSpec((tm, tn), lambda i, j: (i, j)),
        out_shape=jax.ShapeDtypeStruct((M, N), out_dtype),
        compiler_params=_cparams("parallel", "parallel"),
        name="linear",
    )(x, w)


def _proj_ln_kernel(a_ref, w_ref, h_ref, g_ref, b_ref, of_ref, ob_ref):
    tm = a_ref.shape[0]
    rp = tm // PROJ_ROW_PARTS
    parts = [slice(p * rp, (p + 1) * rp) for p in range(PROJ_ROW_PARTS)]
    mix = [None] * PROJ_ROW_PARTS

    def s_proj(c):
        mix[c] = _dot(a_ref[parts[c], :], w_ref[...])

    def s_norm(c):
        out = _layer_norm_rows(ALPHA * h_ref[parts[c], :] + mix[c], g_ref[...], b_ref[...])
        of_ref[parts[c], :] = out
        ob_ref[parts[c], :] = out.astype(BF16)

    _emit_skewed((s_proj, s_norm), PROJ_ROW_PARTS)


def proj_ln(a, w, h, g, b, tm=512):
    M, K = a.shape
    D = w.shape[1]
    return pl.pallas_call(
        _proj_ln_kernel,
        grid=(M // tm,),
        in_specs=[pl.BlockSpec((tm, K), lambda i: (i, 0)),
                  pl.BlockSpec((K, D), lambda i: (0, 0)),
                  pl.BlockSpec((tm, D), lambda i: (i, 0)),
                  pl.BlockSpec((1, D), lambda i: (0, 0)),
                  pl.BlockSpec((1, D), lambda i: (0, 0))],
        out_specs=[pl.BlockSpec((tm, D), lambda i: (i, 0)),
                   pl.BlockSpec((tm, D), lambda i: (i, 0))],
        out_shape=[jax.ShapeDtypeStruct((M, D), F32), jax.ShapeDtypeStruct((M, D), BF16)],
        compiler_params=_cparams("parallel"),
        name="proj_ln",
    )(a, w, h, g.reshape(1, D), b.reshape(1, D))


def _router_kernel(h_ref, rwt_ref, rb_ref, su_ref, idx_ref, gate_ref, cnt_ref, base_ref):
    step = pl.program_id(0)
    tr = h_ref.shape[0]

    @pl.when(step == 0)
    def _():
        base_ref[...] = jnp.zeros_like(base_ref)

    logits = _dot_nt_split(rwt_ref[...], h_ref[...])
    scores = jax.nn.sigmoid(logits)
    sel = scores + rb_ref[...]
    s = [sel[e:e + 1, :] for e in range(N_EXPERTS)]
    sc = [scores[e:e + 1, :] for e in range(N_EXPERTS)]

    def top2sum(v):
        best = v[0] + v[1]
        for a in range(4):
            for b in range(a + 1, 4):
                if (a, b) != (0, 1):
                    best = jnp.maximum(best, v[a] + v[b])
        return best

    gs = [top2sum(s[4 * g:4 * g + 4]) for g in range(N_GROUPS)]
    best, gi = gs[0], jnp.zeros((1, tr), I32)
    for g in range(1, N_GROUPS):
        better = gs[g] > best
        gi = jnp.where(better, g, gi)
        best = jnp.where(better, gs[g], best)

    def pick_group(rows):
        out = []
        for j in range(EXPERTS_PER_GROUP):
            v = rows[j]
            for g in range(1, N_GROUPS):
                v = jnp.where(gi == g, rows[4 * g + j], v)
            out.append(v)
        return out

    v = pick_group(s)
    raw = pick_group(sc)
    m1, l1, r1 = v[0], jnp.zeros((1, tr), I32), raw[0]
    for j in range(1, 4):
        better = v[j] > m1
        l1 = jnp.where(better, j, l1)
        m1 = jnp.where(better, v[j], m1)
        r1 = jnp.where(better, raw[j], r1)
    m2 = jnp.full((1, tr), -jnp.inf, F32)
    l2 = jnp.zeros((1, tr), I32)
    r2 = jnp.zeros((1, tr), F32)
    for j in range(4):
        better = jnp.logical_and(l1 != j, v[j] > m2)
        l2 = jnp.where(better, j, l2)
        m2 = jnp.where(better, v[j], m2)
        r2 = jnp.where(better, raw[j], r2)
    den = r1 + r2
    first_low = l1 < l2
    lo = jnp.where(first_low, l1, l2)
    hi = jnp.where(first_low, l2, l1)
    g_lo = jnp.where(first_low, r1, r2) / den
    g_hi = jnp.where(first_low, r2, r1) / den
    gate_ref[...] = jnp.concatenate([g_lo, g_hi], axis=0)
    bucket = gi * PAIRS_PER_GROUP + 2 * lo + hi - 1 - jnp.where(lo == 2, 1, 0)

    bidx = lax.broadcasted_iota(I32, (N_BUCKETS_PAD, tr), 0)
    hit = bidx == bucket
    oh = jnp.where(hit, 1.0, 0.0)
    rank = _dot(oh.astype(BF16), su_ref[...]) + base_ref[...]
    rk = jnp.sum(jnp.where(hit, rank, 0.0), axis=0, keepdims=True).astype(I32)
    idx_ref[...] = jnp.concatenate([bucket, rk], axis=0)
    base_ref[...] = base_ref[...] + jnp.sum(oh, axis=1, keepdims=True)
    cnt_ref[...] = jnp.broadcast_to(base_ref[...], cnt_ref.shape)


def moe_router(h, router_w, router_b, tr=512):
    T, D = h.shape
    su = jnp.triu(jnp.ones((tr, tr), BF16), k=1)
    idx, gate, cnt = pl.pallas_call(
        _router_kernel,
        grid=(T // tr,),
        in_specs=[pl.BlockSpec((tr, D), lambda i: (i, 0)),
                  pl.BlockSpec((N_EXPERTS, D), lambda i: (0, 0)),
                  pl.BlockSpec((N_EXPERTS, 1), lambda i: (0, 0)),
                  pl.BlockSpec((tr, tr), lambda i: (0, 0))],
        out_specs=[pl.BlockSpec((2, tr), lambda i: (0, i)),
                   pl.BlockSpec((2, tr), lambda i: (0, i)),
                   pl.BlockSpec((N_BUCKETS_PAD, LANES), lambda i: (0, 0))],
        out_shape=[jax.ShapeDtypeStruct((2, T), I32), jax.ShapeDtypeStruct((2, T), F32),
                   jax.ShapeDtypeStruct((N_BUCKETS_PAD, LANES), F32)],
        scratch_shapes=[pltpu.VMEM((N_BUCKETS_PAD, 1), F32)],
        compiler_params=_cparams("arbitrary"),
        name="moe_router",
    )(h, router_w.T, router_b.reshape(N_EXPERTS, 1), su)
    return idx, gate, cnt[:N_BUCKETS, 0].astype(I32)


def _dispatch_kernel(pos_ref, h_ref, gate_ref, z_ref, xs_ref, aug, sems):
    del z_ref
    i, n = pl.program_id(0), pl.num_programs(0)
    td, D = h_ref.shape
    slot = i % 2
    stage = aug.at[slot]
    stage[:, :D] = h_ref[...]
    stage[:, D:] = gate_ref[...]

    def row_copy(s, r, dst_row):
        return pltpu.make_async_copy(aug.at[s, pl.ds(r, 1)], xs_ref.at[pl.ds(dst_row, 1)], sems.at[s])

    def issue(g, c):
        for j in range(DMA_UNROLL):
            r = g * DMA_UNROLL + j
            row_copy(slot, r, pos_ref[r]).start(priority=j % 2)
        return c

    lax.fori_loop(0, td // DMA_UNROLL, issue, 0)

    def drain(s):
        pltpu.make_async_copy(aug.at[s], xs_ref.at[pl.ds(0, td)], sems.at[s]).wait()

    pl.when(i > 0)(lambda: drain(1 - slot))
    pl.when(i == n - 1)(lambda: drain(slot))


def moe_dispatch(h, pos, gcols, n_rows, td=1024):
    T, D = h.shape
    W = D + GATE_LANES
    zeros = jnp.zeros((n_rows, W), h.dtype)
    return pl.pallas_call(
        _dispatch_kernel,
        grid=(T // td,),
        in_specs=[pl.BlockSpec((td,), lambda i: (i,), memory_space=pltpu.SMEM),
                  pl.BlockSpec((td, D), lambda i: (i, 0)),
                  pl.BlockSpec((td, GATE_LANES), lambda i: (i, 0)),
                  pl.BlockSpec(memory_space=pl.ANY)],
        out_specs=pl.BlockSpec(memory_space=pl.ANY),
        out_shape=jax.ShapeDtypeStruct((n_rows, W), h.dtype),
        scratch_shapes=[pltpu.VMEM((2, td, W), h.dtype), pltpu.SemaphoreType.DMA((2,))],
        input_output_aliases={3: 0},
        compiler_params=_cparams("arbitrary"),
        name="moe_dispatch",
    )(pos, h, gcols, zeros)


def _expert_kernel(ta_ref, tb_ref, nu_ref, x_ref, wga, wua, wda, wgb, wub, wdb, o_ref):
    del ta_ref, tb_ref
    D = o_ref.shape[1]

    @pl.when(pl.program_id(0) < nu_ref[0])
    def _():
        xa = x_ref[...]
        x = xa[:, :D].astype(BF16)

        weights = ((wga, wua, wda), (wgb, wub, wdb))
        st = [dict(), dict()]

        def s_in(c):
            st[c]["g"] = _dot(x, weights[c][0][0, 0])
            st[c]["u"] = _dot(x, weights[c][1][0, 0])

        def s_act(c):
            st[c]["he"] = (jax.nn.silu(st[c].pop("g")) * st[c].pop("u")).astype(BF16)

        def s_out(c):
            st[c]["y"] = _dot(st[c].pop("he"), weights[c][2][0, 0])

        _emit_skewed((s_in, s_act, s_out), 2)
        o_ref[...] = xa[:, D:D + 1] * st[0]["y"] + xa[:, D + 1:D + 2] * st[1]["y"]

    @pl.when(pl.program_id(0) >= nu_ref[0])
    def _():
        o_ref[...] = jnp.zeros_like(o_ref)


def moe_experts(xs, tile_ea, tile_eb, n_used, w_gate, w_up, w_down, layer):
    n_rows, W = xs.shape
    _, E, D, De = w_gate.shape
    n_tiles = n_rows // MOE_TILE

    def row_map(i, ta, tb, nu):
        return (jnp.maximum(jnp.minimum(i, nu[0] - 1), 0), 0)

    wa = lambda i, ta, tb, nu: (layer, ta[i], 0, 0)
    wb = lambda i, ta, tb, nu: (layer, tb[i], 0, 0)
    grid_spec = pltpu.PrefetchScalarGridSpec(
        num_scalar_prefetch=3,
        grid=(n_tiles,),
        in_specs=[pl.BlockSpec((MOE_TILE, W), row_map),
                  pl.BlockSpec((1, 1, D, De), wa), pl.BlockSpec((1, 1, D, De), wa),
                  pl.BlockSpec((1, 1, De, D), wa),
                  pl.BlockSpec((1, 1, D, De), wb), pl.BlockSpec((1, 1, D, De), wb),
                  pl.BlockSpec((1, 1, De, D), wb)],
        out_specs=pl.BlockSpec((MOE_TILE, D), lambda i, ta, tb, nu: (i, 0)),
    )
    return pl.pallas_call(
        _expert_kernel,
        grid_spec=grid_spec,
        out_shape=jax.ShapeDtypeStruct((n_rows, D), F32),
        compiler_params=_cparams("arbitrary"),
        name="moe_experts",
    )(tile_ea, tile_eb, n_used, xs, w_gate, w_up, w_down, w_gate, w_up, w_down)


def _combine_kernel(pos_ref, posn_ref, h_ref, lg_ref, lb_ref, os_ref, of_ref, ob_ref, buf, sems):
    i, n = pl.program_id(0), pl.num_programs(0)
    tc = h_ref.shape[0]
    slot = i % 2

    def row_copy(s, src_row, r):
        return pltpu.make_async_copy(os_ref.at[pl.ds(src_row, 1)], buf.at[s, pl.ds(r, 1)], sems.at[s])

    def fetch(p_ref, s):
        def body(g, c):
            for j in range(DMA_UNROLL):
                r = g * DMA_UNROLL + j
                row_copy(s, p_ref[r], r).start(priority=j % 2)
            return c
        lax.fori_loop(0, tc // DMA_UNROLL, body, 0)

    pl.when(i == 0)(lambda: fetch(pos_ref, slot))
    pl.when(i + 1 < n)(lambda: fetch(posn_ref, 1 - slot))

    pltpu.make_async_copy(os_ref.at[pl.ds(0, tc)], buf.at[slot], sems.at[slot]).wait()
    out = _layer_norm_rows(ALPHA * h_ref[...] + buf[slot], lg_ref[...], lb_ref[...])
    of_ref[...] = out
    ob_ref[...] = out.astype(BF16)


def moe_combine(osorted, pos, h, ln_g, ln_b, tc=1024):
    T, D = h.shape
    nt = T // tc
    return pl.pallas_call(
        _combine_kernel,
        grid=(nt,),
        in_specs=[pl.BlockSpec((tc,), lambda i: (i,), memory_space=pltpu.SMEM),
                  pl.BlockSpec((tc,), lambda i: (jnp.minimum(i + 1, nt - 1),), memory_space=pltpu.SMEM),
                  pl.BlockSpec((tc, D), lambda i: (i, 0)),
                  pl.BlockSpec((1, D), lambda i: (0, 0)),
                  pl.BlockSpec((1, D), lambda i: (0, 0)),
                  pl.BlockSpec(memory_space=pl.ANY)],
        out_specs=[pl.BlockSpec((tc, D), lambda i: (i, 0)),
                   pl.BlockSpec((tc, D), lambda i: (i, 0))],
        out_shape=[jax.ShapeDtypeStruct((T, D), F32), jax.ShapeDtypeStruct((T, D), BF16)],
        scratch_shapes=[pltpu.VMEM((2, tc, D), F32), pltpu.SemaphoreType.DMA((2,))],
        compiler_params=_cparams("arbitrary"),
        name="moe_combine",
    )(pos, pos, h, ln_g.reshape(1, D), ln_b.reshape(1, D), osorted)


_PAIRS = ((0, 1), (0, 2), (0, 3), (1, 2), (1, 3), (2, 3))


def grouped_moe_ln(h, router_w, router_b, w_gate, w_up, w_down, ln_g, ln_b, layer):
    T, D = h.shape
    idx, gate, cnt = moe_router(h, router_w, router_b)
    bucket, rank = idx[0], idx[1]
    tiles = (cnt + MOE_TILE - 1) // MOE_TILE
    tile_end = jnp.cumsum(tiles)
    start = (tile_end - tiles) * MOE_TILE
    pos = start[bucket] + rank
    n_tiles = T // MOE_TILE + N_BUCKETS
    tile_ids = jnp.arange(n_tiles, dtype=I32)
    tile_bucket = jnp.minimum(
        jnp.sum((tile_ids[:, None] >= tile_end[None, :]).astype(I32), axis=1), N_BUCKETS - 1)
    ea = jnp.array([g * EXPERTS_PER_GROUP + a for g in range(N_GROUPS) for a, _ in _PAIRS], I32)
    eb = jnp.array([g * EXPERTS_PER_GROUP + b for g in range(N_GROUPS) for _, b in _PAIRS], I32)
    n_used = tile_end[-1:].astype(I32)
    gcols = jnp.pad(gate.T, ((0, 0), (0, GATE_LANES - 2)))
    xs = moe_dispatch(h, pos, gcols, n_tiles * MOE_TILE)
    osorted = moe_experts(xs, ea[tile_bucket], eb[tile_bucket], n_used, w_gate, w_up, w_down, layer)
    return moe_combine(osorted, pos, h, ln_g, ln_b)


def _forget_kernel(h_ref, wft_ref, bf_ref, ui_ref, c_ref, carry_ref):
    @pl.when(pl.program_id(1) == 0)
    def _():
        carry_ref[...] = jnp.zeros_like(carry_ref)

    logits = _dot_nt_split(wft_ref[...], h_ref[...])
    log_f = jax.nn.log_sigmoid(logits + bf_ref[...])
    hi, mid, lo = _split3(log_f)
    ui = ui_ref[...]
    cum = _dot(hi, ui) + _dot(mid, ui) + _dot(lo, ui) + carry_ref[...]
    c_ref[0] = cum * LOG2E
    carry_ref[...] = cum[:, -1:]


def fox_forget_cumsum(h, w_f, b_f, B, S, ts=512):
    T, D = h.shape
    H = w_f.shape[1]
    ui = jnp.triu(jnp.ones((ts, ts), BF16))
    ns = S // ts
    return pl.pallas_call(
        _forget_kernel,
        grid=(B, ns),
        in_specs=[pl.BlockSpec((ts, D), lambda b, i: (b * ns + i, 0)),
                  pl.BlockSpec((H, D), lambda b, i: (0, 0)),
                  pl.BlockSpec((H, 1), lambda b, i: (0, 0)),
                  pl.BlockSpec((ts, ts), lambda b, i: (0, 0))],
        out_specs=pl.BlockSpec((1, H, ts), lambda b, i: (b, 0, i)),
        out_shape=jax.ShapeDtypeStruct((B, H, S), F32),
        scratch_shapes=[pltpu.VMEM((H, 1), F32)],
        compiler_params=_cparams("parallel", "arbitrary"),
        name="fox_forget",
    )(h, w_f.T, b_f.reshape(H, 1), ui)


def _head_pair_masks(q):
    lane = lax.broadcasted_iota(I32, q.shape, 1)
    zero = jnp.zeros_like(q)
    return jnp.where(lane < HEAD_DIM, q, zero), jnp.where(lane >= HEAD_DIM, q, zero)


def _lane_tile(x, n):
    return x if n == 1 else jnp.concatenate([x] * n, axis=1)


def _causal_steps(nq, ratio, descending):
    qi, kb = [], []
    for i in range(nq):
        ks = list(range(ratio * (i + 1)))
        for k in (reversed(ks) if descending else ks):
            qi.append(i)
            kb.append(k)
    return jnp.array(qi, I32), jnp.array(kb, I32)


def _emit_skewed(stages, n_chains):
    for t in range(n_chains + len(stages) - 1):
        for s in reversed(range(len(stages))):
            if 0 <= t - s < n_chains:
                stages[s](t - s)


def _pair(p):
    return slice(p * LANES, (p + 1) * LANES)


def _fox_kernel(qi_ref, kb_ref, q_ref, k_ref, v_ref, crow_ref, ccol_ref, og_ref, o_ref, qm, ct, m, acc):
    hg, step_id = pl.program_id(1), pl.program_id(2)
    i, j = qi_ref[step_id], kb_ref[step_id]
    tq, tk = q_ref.shape[1], k_ref.shape[1]
    n_heads = qm.shape[0]

    @pl.when(j == 0)
    def _():
        q = q_ref[0]
        cc = ccol_ref[0]
        head = lax.broadcasted_iota(I32, cc.shape, 1)
        for hh in range(n_heads):
            qm[hh] = _head_pair_masks(q[:, _pair(hh // 2)])[hh % 2]
            col = jnp.sum(jnp.where(head == n_heads * hg + hh, cc, 0.0), axis=1, keepdims=True)
            ct[hh] = jnp.broadcast_to(col, (tq, LANES))
        m[...] = jnp.full(m.shape, -jnp.inf, F32)
        acc[...] = jnp.zeros_like(acc)

    def step(masked):
        ks = [k_ref[0, :, _pair(p)] for p in range(n_heads // 2)]
        ones = jnp.ones((tk, LANES), BF16)
        v_ones = [jnp.concatenate([v_ref[0, :, _pair(p)], ones], axis=1) for p in range(n_heads // 2)]
        crow = crow_ref[0]
        if masked:
            row = lax.broadcasted_iota(I32, (tq, tk), 0)
            col = lax.broadcasted_iota(I32, (tq, tk), 1)
            keep = col <= row
        rp = tq // ATT_ROW_PARTS
        chains = [(hh, slice(p * rp, (p + 1) * rp)) for hh in range(n_heads) for p in range(ATT_ROW_PARTS)]
        st = [dict() for _ in chains]

        def s_scores(c):
            hh, rows = chains[c]
            st[c]["s"] = _dot_nt(qm[hh, rows, :], ks[hh // 2])

        def s_softmax(c):
            hh, rows = chains[c]
            s = st[c].pop("s") - crow[hh // 2, hh % 2:hh % 2 + 1, :]
            if masked:
                s = jnp.where(keep[rows, :], s, -jnp.inf)
            c_t = ct[hh, rows, :]
            m_prev = m[hh, rows, :]
            m_new = jnp.maximum(m_prev, jnp.max(s, axis=1, keepdims=True) + c_t)
            st[c]["alpha"] = jnp.exp2(m_prev - m_new)
            st[c]["p"] = jnp.exp2(s - _lane_tile(m_new - c_t, tk // LANES)).astype(BF16)
            m[hh, rows, :] = m_new

        def s_values(c):
            hh, rows = chains[c]
            acc[hh, rows, :] = (_lane_tile(st[c].pop("alpha"), 2) * acc[hh, rows, :]
                                + _dot(st[c].pop("p"), v_ones[hh // 2]))

        _emit_skewed((s_scores, s_softmax, s_values), len(chains))

    pl.when(j < i)(lambda: step(False))
    pl.when(j == i)(lambda: step(True))

    @pl.when(j == i)
    def _():
        lane = lax.broadcasted_iota(I32, (tq, LANES), 1)
        for p in range(n_heads // 2):
            a0, a1 = acc[2 * p], acc[2 * p + 1]
            o = jnp.where(lane < HEAD_DIM, a0[:, :LANES] / a0[:, LANES:], a1[:, :LANES] / a1[:, LANES:])
            o_ref[0, :, _pair(p)] = (o * jax.nn.sigmoid(og_ref[0, :, _pair(p)].astype(F32))).astype(BF16)


def fox_attention(qkv, og, c, B, S, tq=512):
    T = qkv.shape[0]
    D = N_HEADS * HEAD_DIM
    hp_n = N_HEADS // 2
    nq = S // tq
    qkv3 = qkv.reshape(B, S, 3 * D)
    crow = c.reshape(B, hp_n, 2, S)
    ccol = jnp.transpose(c, (0, 2, 1))
    qi, kb = _causal_steps(nq, 1, descending=False)
    w = ATT_PAIRS * LANES
    hg_n = hp_n // ATT_PAIRS
    nh = 2 * ATT_PAIRS
    q_blk = lambda b, hg, s, qi, kb: (b, qi[s], hg)
    kv_blk = lambda off: (lambda b, hg, s, qi, kb: (b, kb[s], off + hg))
    grid_spec = pltpu.PrefetchScalarGridSpec(
        num_scalar_prefetch=2,
        grid=(B, hg_n, qi.shape[0]),
        in_specs=[pl.BlockSpec((1, tq, w), q_blk),
                  pl.BlockSpec((1, tq, w), kv_blk(hg_n)),
                  pl.BlockSpec((1, tq, w), kv_blk(2 * hg_n)),
                  pl.BlockSpec((1, ATT_PAIRS, 2, tq), lambda b, hg, s, qi, kb: (b, hg, 0, kb[s])),
                  pl.BlockSpec((1, tq, N_HEADS), lambda b, hg, s, qi, kb: (b, qi[s], 0)),
                  pl.BlockSpec((1, tq, w), q_blk)],
        out_specs=pl.BlockSpec((1, tq, w), q_blk),
        scratch_shapes=[pltpu.VMEM((nh, tq, LANES), BF16),
                        pltpu.VMEM((nh, tq, LANES), F32), pltpu.VMEM((nh, tq, LANES), F32),
                        pltpu.VMEM((nh, tq, 2 * LANES), F32)],
    )
    out = pl.pallas_call(
        _fox_kernel,
        grid_spec=grid_spec,
        out_shape=jax.ShapeDtypeStruct((B, S, D), BF16),
        compiler_params=_cparams("parallel", "parallel", "arbitrary"),
        name="fox_attention",
    )(qi, kb, qkv3, qkv3, qkv3, crow, ccol, og.reshape(B, S, D))
    return out.reshape(T, D)


def _sb_kernel(qi_ref, kb_ref, q_ref, k_ref, v_ref, u_ref, o_ref, qm, rs, acc):
    step_id = pl.program_id(2)
    i, kb = qi_ref[step_id], kb_ref[step_id]
    tq, tk = q_ref.shape[1], k_ref.shape[1]
    ratio = tq // tk

    n_heads = qm.shape[0]

    @pl.when(kb == ratio * (i + 1) - 1)
    def _():
        q = q_ref[0]
        for hh in range(n_heads):
            qm[hh] = _head_pair_masks(q[:, _pair(hh // 2)])[hh % 2]
        rs[...] = jnp.zeros_like(rs)
        acc[...] = jnp.zeros_like(acc)

    def step(masked):
        ks = [k_ref[0, :, _pair(p)] for p in range(n_heads // 2)]
        vs = [v_ref[0, :, _pair(p)] for p in range(n_heads // 2)]
        u = u_ref[...]
        if masked:
            row = lax.broadcasted_iota(I32, (tq, tk), 0)
            col = lax.broadcasted_iota(I32, (tq, tk), 1)
            keep = col - row < i * tq - kb * tk
        rp = tq // ATT_ROW_PARTS
        chains = [(hh, slice(p * rp, (p + 1) * rp)) for hh in range(n_heads) for p in range(ATT_ROW_PARTS)]
        st = [dict() for _ in chains]

        def s_scores(c):
            x, rows = chains[c]
            st[c]["z"] = _dot_nt(qm[x, rows, :], ks[x // 2])

        def s_logs(c):
            x, rows = chains[c]
            z = st[c]["z"]
            zb = z.astype(BF16)
            e = jnp.exp2((jnp.abs(z) * (-LOG2E)).astype(BF16))
            nl = jnp.maximum(zb, 0) + jnp.log(jnp.ones((), BF16) + e)
            if masked:
                nl = jnp.where(keep[rows, :], nl, jnp.zeros((), BF16))
            st[c]["nl"] = nl

        def s_suffix(c):
            st[c]["ci"] = _dot(st[c].pop("nl"), u)

        def s_weights(c):
            x, rows = chains[c]
            prev = rs[x, rows, :]
            ci = st[c].pop("ci")
            a = jnp.exp2((st[c].pop("z") - ci - _lane_tile(prev, tk // LANES)) * LOG2E)
            if masked:
                a = jnp.where(keep[rows, :], a, 0.0)
            st[c]["a"] = a.astype(BF16)
            rs[x, rows, :] = prev + ci[:, 0:1]

        def s_values(c):
            x, rows = chains[c]
            acc[x, rows, :] = acc[x, rows, :] + _dot(st[c].pop("a"), vs[x // 2])

        _emit_skewed((s_scores, s_logs, s_suffix, s_weights, s_values), len(chains))

    pl.when(kb >= ratio * i)(lambda: step(True))
    pl.when(kb < ratio * i)(lambda: step(False))

    @pl.when(kb == 0)
    def _():
        lane = lax.broadcasted_iota(I32, (tq, LANES), 1)
        for p in range(n_heads // 2):
            o_ref[0, :, _pair(p)] = jnp.where(lane < HEAD_DIM, acc[2 * p], acc[2 * p + 1]).astype(BF16)


def stick_breaking_attention(qkv, B, S, tq=512, tk=256):
    T = qkv.shape[0]
    D = N_HEADS * HEAD_DIM
    hp_n = N_HEADS // 2
    nq = S // tq
    qkv3 = qkv.reshape(B, S, 3 * D)
    u = jnp.tril(jnp.ones((tk, tk), BF16))
    qi, kb = _causal_steps(nq, tq // tk, descending=True)
    w = ATT_PAIRS * LANES
    hg_n = hp_n // ATT_PAIRS
    nh = 2 * ATT_PAIRS
    q_blk = lambda b, hg, s, qi, kb: (b, qi[s], hg)
    kv_blk = lambda off: (lambda b, hg, s, qi, kb: (b, kb[s], off + hg))
    grid_spec = pltpu.PrefetchScalarGridSpec(
        num_scalar_prefetch=2,
        grid=(B, hg_n, qi.shape[0]),
        in_specs=[pl.BlockSpec((1, tq, w), q_blk),
                  pl.BlockSpec((1, tk, w), kv_blk(hg_n)),
                  pl.BlockSpec((1, tk, w), kv_blk(2 * hg_n)),
                  pl.BlockSpec((tk, tk), lambda b, hg, s, qi, kb: (0, 0))],
        out_specs=pl.BlockSpec((1, tq, w), q_blk),
        scratch_shapes=[pltpu.VMEM((nh, tq, LANES), BF16),
                        pltpu.VMEM((nh, tq, LANES), F32), pltpu.VMEM((nh, tq, LANES), F32)],
    )
    out = pl.pallas_call(
        _sb_kernel,
        grid_spec=grid_spec,
        out_shape=jax.ShapeDtypeStruct((B, S, D), BF16),
        compiler_params=_cparams("parallel", "parallel", "arbitrary"),
        name="stick_breaking",
    )(qi, kb, qkv3, qkv3, qkv3, u)
    return out.reshape(T, D)


def _gmlp_kernel(h_ref, win_ref, bin_ref, lg_ref, lb_ref, ws_ref, bst_ref, o_ref):
    tm = h_ref.shape[0]
    dg = o_ref.shape[1]
    row = lax.broadcasted_iota(I32, (CHUNK, CHUNK), 0)
    col = lax.broadcasted_iota(I32, (CHUNK, CHUNK), 1)
    gw = dg // GMLP_GROUPS
    wcs = [jnp.where(col <= row, ws_ref[g], 0.0).astype(BF16) for g in range(GMLP_GROUPS)]
    rp = tm // GMLP_ROW_PARTS
    parts = [slice(p * rp, (p + 1) * rp) for p in range(GMLP_ROW_PARTS)]
    zs = [None] * GMLP_ROW_PARTS

    def s_proj(p):
        zs[p] = _dot(h_ref[parts[p], :], win_ref[...])

    def s_gate(p):
        z = jax.nn.gelu(zs[p] + bin_ref[...])
        u = z[:, :dg]
        v = _layer_norm_rows(z[:, dg:], lg_ref[...], lb_ref[...]).astype(BF16)
        for g in range(GMLP_GROUPS):
            cs = slice(g * gw, (g + 1) * gw)
            for c in range(rp // CHUNK):
                rs = slice(c * CHUNK, (c + 1) * CHUNK)
                sv = _dot(wcs[g], v[rs, cs]) + bst_ref[:, g:g + 1]
                o_ref[pl.ds(p * rp + c * CHUNK, CHUNK), cs] = (u[rs, cs] * sv).astype(BF16)

    _emit_skewed((s_proj, s_gate), GMLP_ROW_PARTS)


def gmlp_gate(hb, w_in, b_in, ln_g, ln_b, w_s, b_s, tm=512):
    T, D = hb.shape
    dg = w_in.shape[1] // 2
    return pl.pallas_call(
        _gmlp_kernel,
        grid=(T // tm,),
        in_specs=[pl.BlockSpec((tm, D), lambda i: (i, 0)),
                  pl.BlockSpec((D, 2 * dg), lambda i: (0, 0)),
                  pl.BlockSpec((1, 2 * dg), lambda i: (0, 0)),
                  pl.BlockSpec((1, dg), lambda i: (0, 0)),
                  pl.BlockSpec((1, dg), lambda i: (0, 0)),
                  pl.BlockSpec((GMLP_GROUPS, CHUNK, CHUNK), lambda i: (0, 0, 0)),
                  pl.BlockSpec((CHUNK, GMLP_GROUPS), lambda i: (0, 0))],
        out_specs=pl.BlockSpec((tm, dg), lambda i: (i, 0)),
        out_shape=jax.ShapeDtypeStruct((T, dg), BF16),
        compiler_params=_cparams("parallel"),
        name="gmlp_gate",
    )(hb, w_in, b_in.reshape(1, -1), ln_g.reshape(1, -1), ln_b.reshape(1, -1), w_s, b_s.T)


RW_LANES = 4 * HEAD_DIM


def _seg_sum(x, bones):
    hi = x.astype(BF16)
    lo = (x - hi.astype(F32)).astype(BF16)
    return _dot(hi, bones) + _dot(lo, bones)


def _rwkv_prep_kernel(seq_len, h_ref, hp_ref, mu_ref, wrkv_ref, w1_ref, w2_ref, a1_ref, a2_ref,
                      g1_ref, g2_ref, w0_ref, a0_ref, kk_ref, ka_ref, bones_ref, lchunk_ref,
                      r_ref, cl_ref, clx_ref, k_ref, v_ref, kkn_ref, b_ref, g_ref):
    i = pl.program_id(0)
    tm, D = h_ref.shape
    h = h_ref[...]
    at_start = (i * tm) % seq_len == 0
    prev_row = jnp.where(at_start, 0.0, hp_ref[7:8, :])
    row = lax.broadcasted_iota(I32, (tm, D), 0)
    h_prev = jnp.where(row == 0, prev_row, pltpu.roll(h, 1, axis=0))
    dx = h_prev - h

    def mixed(c):
        return (h + dx * mu_ref[c:c + 1, :]).astype(BF16)

    r = _dot(mixed(0), wrkv_ref[0])
    k = _dot(mixed(1), wrkv_ref[1])
    v = _dot(mixed(2), wrkv_ref[2])
    d = w0_ref[...] + _dot(jnp.tanh(_dot(mixed(3), w1_ref[...])).astype(BF16), w2_ref[...])
    lw = -jnp.exp(-jax.nn.softplus(-d) - 0.5)
    a = jax.nn.sigmoid(a0_ref[...] + _dot(_dot(mixed(4), a1_ref[...]).astype(BF16), a2_ref[...]))
    g = _dot(jax.nn.sigmoid(_dot(mixed(5), g1_ref[...])).astype(BF16), g2_ref[...])
    kk = k * kk_ref[...]
    bones = bones_ref[...]
    norm_sq = jnp.concatenate(
        [_seg_sum(jnp.square(kk[:, q * RW_LANES:(q + 1) * RW_LANES]), bones) for q in range(D // RW_LANES)],
        axis=1)
    kkn = kk / jnp.maximum(jnp.sqrt(norm_sq), 1e-12)
    hi, mid, lo = _split3(lw)
    lchunk = lchunk_ref[...]
    cl = _dot(lchunk, hi) + _dot(lchunk, mid) + _dot(lchunk, lo)
    r_ref[...] = r.astype(BF16)
    cl_ref[...] = cl
    clx_ref[...] = cl - lw
    k_ref[...] = (k * (1.0 + (a - 1.0) * ka_ref[...])).astype(BF16)
    v_ref[...] = v.astype(BF16)
    kkn_ref[...] = kkn.astype(BF16)
    b_ref[...] = (kkn * a).astype(BF16)
    g_ref[...] = g.astype(BF16)


def _block_ones():
    idx = jnp.arange(RW_LANES) // HEAD_DIM
    return (idx[:, None] == idx[None, :]).astype(BF16)


def rwkv_prep(h, mu, w_rkv, w0, w1, w2, a0, a1, a2, g1, g2, k_k, k_a, S, tm=256):
    T, D = h.shape
    row = lambda i: (i, 0)
    full2 = lambda i: (0, 0)
    vec = lambda x: x.reshape(1, D)
    bf = lambda x: x.astype(BF16)
    pos = jnp.arange(tm)
    lchunk = jnp.logical_and(pos[:, None] // RW_CHUNK == pos[None, :] // RW_CHUNK,
                             pos[None, :] <= pos[:, None]).astype(BF16)
    outs = pl.pallas_call(
        functools.partial(_rwkv_prep_kernel, S),
        grid=(T // tm,),
        in_specs=[pl.BlockSpec((tm, D), row),
                  pl.BlockSpec((8, D), lambda i: (jnp.maximum(i * (tm // 8) - 1, 0), 0)),
                  pl.BlockSpec((6, D), full2),
                  pl.BlockSpec((3, D, D), lambda i: (0, 0, 0)),
                  pl.BlockSpec(w1.shape, full2), pl.BlockSpec(w2.shape, full2),
                  pl.BlockSpec(a1.shape, full2), pl.BlockSpec(a2.shape, full2),
                  pl.BlockSpec(g1.shape, full2), pl.BlockSpec(g2.shape, full2),
                  pl.BlockSpec((1, D), full2), pl.BlockSpec((1, D), full2),
                  pl.BlockSpec((1, D), full2), pl.BlockSpec((1, D), full2),
                  pl.BlockSpec((RW_LANES, RW_LANES), full2),
                  pl.BlockSpec((tm, tm), full2)],
        out_specs=[pl.BlockSpec((tm, D), row)] * 8,
        out_shape=[jax.ShapeDtypeStruct((T, D), dt) for dt in (BF16, F32, F32, BF16, BF16, BF16, BF16, BF16)],
        compiler_params=_cparams("parallel"),
        name="rwkv_prep",
    )(h, h, mu, bf(w_rkv), bf(w1), bf(w2), bf(a1), bf(a2), bf(g1), bf(g2),
      vec(w0), vec(a0), vec(k_k), vec(k_a), _block_ones(), lchunk)
    return outs


def _rwkv_local(r, cl, clx, k, v, kk, b, cst):
    C = RW_CHUNK
    mbd, eye, strict, incl, levels = cst
    eye_f = jnp.where(eye, 1.0, 0.0)
    eye_b = eye_f.astype(BF16)

    def each(f, *lists):
        return [f(*xs) for xs in zip(*lists)]

    def bd(y):
        return jnp.where(mbd, jnp.concatenate([y.astype(BF16)] * 4, axis=0), jnp.zeros((), BF16))

    def rows(*parts):
        return jnp.concatenate([p.astype(BF16) for p in parts], axis=0)

    cl_last = each(lambda c: c[C - 1:C, :], cl)
    kkt = each(lambda a, c: a * jnp.exp(c), kk, clx)
    rt = each(lambda a, c: a * jnp.exp(c), r, cl)
    g_inv = each(lambda c: jnp.exp(-c), cl)
    g_tail = each(lambda cl_, c: jnp.exp(cl_ - c), cl_last, cl)
    kk_r = each(rows, kkt, rt)
    ab_k = each(lambda l, a, g: _dot_nt(l, bd(a * g)), kk_r, k, g_inv)
    ab_b = each(lambda l, a, g: _dot_nt(l, bd(a * g)), kk_r, b, g_inv)
    a_k = each(lambda x: jnp.where(strict, x[:C], 0.0), ab_k)
    b_k = each(lambda x: jnp.where(incl, x[C:], 0.0), ab_k)
    a_b = each(lambda x: jnp.where(strict, x[:C], 0.0), ab_b)
    b_b = each(lambda x: jnp.where(incl, x[C:], 0.0), ab_b)
    kh_t = each(lambda a, g: _dot_nt(eye_b, bd(a * g)), k, g_tail)
    bh_t = each(lambda a, g: _dot_nt(eye_b, bd(a * g)), b, g_tail)

    dm = each(lambda a: eye_f - jnp.where(levels[0], a, 0.0), a_b)
    for lvl in levels[1:]:
        w = each(lambda d, a: _dot(d.astype(BF16), bd(jnp.where(lvl, a, 0.0))), dm, a_b)
        dm = each(lambda d, w_: d - _dot(w_.astype(BF16), bd(d)), dm, w)
    tm_b = each(lambda d: d.astype(BF16), dm)

    xv = each(lambda a, b_, c, v_: _dot(rows(a, b_, c), bd(v_)), a_k, b_k, kh_t, v)
    kp = each(lambda t, a: _dot(t, bd(a)), tm_b, kkt)
    vp = each(lambda t, x: _dot(t, bd(x[:C])), tm_b, xv)
    bb_bh = each(rows, b_b, bh_t)
    xkp = each(lambda l, a: _dot(l, bd(a)), bb_bh, kp)
    xvp = each(lambda l, a: _dot(l, bd(a)), bb_bh, vp)
    y_loc = each(lambda x, z: x[C:2 * C] - z[:C], xv, xvp)
    r_eff = each(lambda a, z: a - z[:C], rt, xkp)
    m = each(lambda c, z: eye_f * jnp.exp(c) - z[C:], cl_last, xkp)
    n = each(lambda x, z: x[2 * C:] - z[C:], xv, xvp)
    return r_eff, y_loc, m, n


def _rwkv_apply(r_eff, y_loc, m, n, P, mbd):
    C = RW_CHUNK

    def each(f, *lists):
        return [f(*xs) for xs in zip(*lists)]

    def bd(y):
        return jnp.where(mbd, jnp.concatenate([y.astype(BF16)] * 4, axis=0), jnp.zeros((), BF16))

    def rows(*parts):
        return jnp.concatenate([p.astype(BF16) for p in parts], axis=0)

    p_hi = each(lambda p: p.astype(BF16), P)
    m_hi = each(lambda a: a.astype(BF16), m)
    z_hi = each(lambda a, mh, m_, ph: _dot(rows(a, mh, m_ - mh.astype(F32)), bd(ph)), r_eff, m_hi, m, p_hi)
    z_lo = each(lambda a, mh, p, ph: _dot(rows(a, mh), bd(p - ph.astype(F32))), r_eff, m_hi, P, p_hi)
    y = each(lambda zh, zl, yl: zh[:C] + zl[:C] + yl, z_hi, z_lo, y_loc)
    p_new = each(lambda zh, zl, n_: zh[C:2 * C] + zh[2 * C:] + zl[C:] + n_, z_hi, z_lo, n)
    return y, p_new


def _rwkv_scan_kernel(r_ref, cl_ref, clx_ref, k_ref, v_ref, kk_ref, b_ref, g_ref, rk_ref, gg_ref, gb_ref,
                      bones_ref, o_ref, state):
    C = RW_CHUNK

    @pl.when(pl.program_id(1) == 0)
    def _():
        state[...] = jnp.zeros_like(state)

    t = lax.broadcasted_iota(I32, (C, RW_LANES), 0)
    i = lax.broadcasted_iota(I32, (C, RW_LANES), 1) % C
    levels = []
    for sh in range(6):
        tb, ib = t >> sh, i >> sh
        levels.append(jnp.logical_and(tb == ib + 1, (tb & 1) == 1))
    rr = lax.broadcasted_iota(I32, (RW_LANES, RW_LANES), 0) // HEAD_DIM
    cc = lax.broadcasted_iota(I32, (RW_LANES, RW_LANES), 1) // HEAD_DIM
    cst = (rr == cc, i == t, i < t, i <= t, levels)
    bones = bones_ref[...]

    n_groups = r_ref.shape[1] // RW_LANES
    n_chunks = r_ref.shape[0] // C
    tiles = [(slice(c * C, (c + 1) * C), slice(q * RW_LANES, (q + 1) * RW_LANES))
             for c in range(n_chunks) for q in range(n_groups)]
    cut = lambda ref: [ref[rs, ls] for rs, ls in tiles]
    r, k, v = cut(r_ref), cut(k_ref), cut(v_ref)
    r_eff, y_loc, m, n = _rwkv_local(r, cut(cl_ref), cut(clx_ref), k, v, cut(kk_ref), cut(b_ref), cst)
    p = [state[q] for q in range(n_groups)]
    ys = []
    for c in range(n_chunks):
        part = slice(c * n_groups, (c + 1) * n_groups)
        y_c, p = _rwkv_apply(r_eff[part], y_loc[part], m[part], n[part], p, cst[0])
        ys += y_c
    for q in range(n_groups):
        state[q] = p[q]
    stack = lambda xs: jnp.concatenate(xs, axis=0)
    y = stack(ys)
    mean = _seg_sum(y, bones) * (1.0 / HEAD_DIM)
    yc = y - mean
    var = _seg_sum(yc * yc, bones) * (1.0 / HEAD_DIM)
    yn = yc * lax.rsqrt(var + GN_EPS)
    rkr = stack([r[idx].astype(F32) * k[idx].astype(F32) * rk_ref[:, ls]
                 for idx, (rs, ls) in enumerate(tiles)])
    dots = _seg_sum(rkr, bones)
    for idx, (rs, ls) in enumerate(tiles):
        part = slice(idx * C, (idx + 1) * C)
        out = yn[part] * gg_ref[:, ls] + gb_ref[:, ls] + dots[part] * v[idx]
        o_ref[rs, ls] = (out * g_ref[rs, ls]).astype(BF16)


def rwkv_scan(r, cl, clx, k, v, kkn, b, g, r_k, gn_g, gn_b, B, S):
    T, D = r.shape
    rows_per_step = RW_CHUNK * RW_CHUNKS_PER_STEP
    nc = S // rows_per_step
    row = lambda bi, ci: (bi * nc + ci, 0)
    full2 = lambda bi, ci: (0, 0)
    return pl.pallas_call(
        _rwkv_scan_kernel,
        grid=(B, nc),
        in_specs=[pl.BlockSpec((rows_per_step, D), row)] * 8 + [pl.BlockSpec((1, D), full2)] * 3
        + [pl.BlockSpec((RW_LANES, RW_LANES), full2)],
        out_specs=pl.BlockSpec((rows_per_step, D), row),
        out_shape=jax.ShapeDtypeStruct((T, D), BF16),
        scratch_shapes=[pltpu.VMEM((D // RW_LANES, HEAD_DIM, RW_LANES), F32)],
        compiler_params=_cparams("parallel", "arbitrary"),
        name="rwkv_scan",
    )(r, cl, clx, k, v, kkn, b, g, r_k.reshape(1, D), gn_g.reshape(1, D), gn_b.reshape(1, D),
      _block_ones())


def rwkv_layer(h, mu, w_rkv, w0, w1, w2, a0, a1, a2, g1, g2, k_k, k_a, r_k, gn_g, gn_b, w_out,
               ln_g, ln_b, B, S):
    r, cl, clx, k, v, kkn, b, g = rwkv_prep(h, mu, w_rkv, w0, w1, w2, a0, a1, a2, g1, g2, k_k, k_a, S)
    y = rwkv_scan(r, cl, clx, k, v, kkn, b, g, r_k, gn_g, gn_b, B, S)
    return proj_ln(y, w_out.astype(BF16), h, ln_g, ln_b)


def fox_layer(h, hb, w_in, b_f, w_out, ln_g, ln_b, B, S):
    D = h.shape[1]
    scale = HEAD_DIM ** -0.5 * LOG2E
    w_qkv = jnp.concatenate([w_in[:, :D] * scale, w_in[:, D:3 * D]], axis=1).astype(BF16)
    w_f = w_in[:, 3 * D:3 * D + N_HEADS]
    w_og = w_in[:, 3 * D + N_HEADS:].astype(BF16)
    qkv = linear(hb, w_qkv, BF16)
    og = linear(hb, w_og, BF16)
    c = fox_forget_cumsum(h, w_f, b_f, B, S)
    o = fox_attention(qkv, og, c, B, S)
    return proj_ln(o, w_out.astype(BF16), h, ln_g, ln_b)


def sb_layer(h, hb, w_in, w_out, ln_g, ln_b, B, S):
    D = h.shape[1]
    scale = HEAD_DIM ** -0.5
    w_qkv = jnp.concatenate([w_in[:, :D] * scale, w_in[:, D:]], axis=1).astype(BF16)
    qkv = linear(hb, w_qkv, BF16)
    o = stick_breaking_attention(qkv, B, S)
    return proj_ln(o, w_out.astype(BF16), h, ln_g, ln_b)


def gmlp_layer(h, hb, w_in, b_in, gm_ln_g, gm_ln_b, w_s, b_s, w_out, ln_g, ln_b):
    y = gmlp_gate(hb, w_in.astype(BF16), b_in, gm_ln_g, gm_ln_b, w_s, b_s)
    return proj_ln(y, w_out.astype(BF16), h, ln_g, ln_b)


def kernel(x, ln1_g, ln1_b, ln2_g, ln2_b, fox_w_in, fox_b_f, fox_w_out, gm_w_in, gm_b_in, gm_ln_g,
           gm_ln_b, gm_w_s, gm_b_s, gm_w_out, sb_w_in, sb_w_out, rw_mu, rw_w_rkv, rw_w0, rw_w1, rw_w2,
           rw_a0, rw_a1, rw_a2, rw_g1, rw_g2, rw_k_k, rw_k_a, rw_r_k, rw_gn_g, rw_gn_b, rw_w_out,
           router_w, router_b, moe_w_gate, moe_w_up, moe_w_down):
    B, S, D = x.shape
    h = x.reshape(B * S, D)
    hb = h.astype(BF16)
    w_gate_b, w_up_b, w_down_b = (w.astype(BF16) for w in (moe_w_gate, moe_w_up, moe_w_down))
    for i in range(DEPTH):
        kind, j = i % 4, i // 4
        if kind == 0:
            h, hb = fox_layer(h, hb, fox_w_in[j], fox_b_f[j], fox_w_out[j], ln1_g[i], ln1_b[i], B, S)
        elif kind == 1:
            h, hb = gmlp_layer(h, hb, gm_w_in[j], gm_b_in[j], gm_ln_g[j], gm_ln_b[j], gm_w_s[j],
                               gm_b_s[j], gm_w_out[j], ln1_g[i], ln1_b[i])
        elif kind == 2:
            h, hb = sb_layer(h, hb, sb_w_in[j], sb_w_out[j], ln1_g[i], ln1_b[i], B, S)
        else:
            h, hb = rwkv_layer(h, rw_mu[j], rw_w_rkv[j], rw_w0[j], rw_w1[j], rw_w2[j], rw_a0[j],
                               rw_a1[j], rw_a2[j], rw_g1[j], rw_g2[j], rw_k_k[j], rw_k_a[j], rw_r_k[j],
                               rw_gn_g[j], rw_gn_b[j], rw_w_out[j], ln1_g[i], ln1_b[i], B, S)
        h, hb = grouped_moe_ln(h, router_w, router_b, w_gate_b, w_up_b, w_down_b, ln2_g[i], ln2_b[i], i)
    return h.reshape(B, S, D)
```

```python
import functools

import jax
import jax.numpy as jnp
from jax import lax
from jax.experimental import pallas as pl
from jax.experimental.pallas import tpu as pltpu

F32 = jnp.float32
BF16 = jnp.bfloat16
I32 = jnp.int32

N_HEADS = 16
HEAD_DIM = 64
CHUNK = 128
GMLP_GROUPS = 8
N_EXPERTS = 16
N_GROUPS = 4
EXPERTS_PER_GROUP = 4
PAIRS_PER_GROUP = 6
N_BUCKETS = N_GROUPS * PAIRS_PER_GROUP
N_BUCKETS_PAD = 32
PROJ_ROW_PARTS = 2
GMLP_ROW_PARTS = 2
DMA_UNROLL = 8
GATE_LANES = 128
DEPTH = 4
ALPHA = (2 * DEPTH) ** 0.25
LN_EPS = 1e-5
GN_EPS = 64e-5
LOG2E = 1.4426950408889634

VMEM_LIMIT_BYTES = 48 * 1024 * 1024
MOE_TILE = 256
RW_CHUNK = 64
RW_CHUNKS_PER_STEP = 4
LANES = 128
ATT_PAIRS = 4
ATT_ROW_PARTS = 1


def _cparams(*sem):
    return pltpu.CompilerParams(dimension_semantics=sem, vmem_limit_bytes=VMEM_LIMIT_BYTES)


def _layer_norm_rows(y, g, b):
    mu = jnp.mean(y, axis=-1, keepdims=True)
    yc = y - mu
    var = jnp.mean(yc * yc, axis=-1, keepdims=True)
    return yc * lax.rsqrt(var + LN_EPS) * g + b


def _split3(x):
    hi = x.astype(BF16)
    r1 = x - hi.astype(F32)
    mid = r1.astype(BF16)
    lo = (r1 - mid.astype(F32)).astype(BF16)
    return hi, mid, lo


def _dot(a, b):
    return jnp.dot(a, b, preferred_element_type=F32)


def _dot_nt(a, b):
    return lax.dot_general(a, b, (((1,), (1,)), ((), ())), preferred_element_type=F32)


def _dot_nt_split(w, x):
    n = w.shape[0]
    w_hi = w.astype(BF16)
    w_lo = (w - w_hi.astype(F32)).astype(BF16)
    x_hi = x.astype(BF16)
    x_lo = (x - x_hi.astype(F32)).astype(BF16)
    both = _dot_nt(jnp.concatenate([w_hi, w_lo], axis=0), x_hi)
    return both[:n] + both[n:] + _dot_nt(w_hi, x_lo)


def _linear_kernel(x_ref, w_ref, o_ref):
    o_ref[...] = _dot(x_ref[...], w_ref[...]).astype(o_ref.dtype)


def linear(x, w, out_dtype, tm=1024, tn=1024):
    M, K = x.shape
    N = w.shape[1]
    tn = min(tn, N)
    return pl.pallas_call(
        _linear_kernel,
        grid=(M // tm, N // tn),
        in_specs=[pl.BlockSpec((tm, K), lambda i, j: (i, 0)),
                  pl.BlockSpec((K, tn), lambda i, j: (0, j))],
        out_specs=pl.BlockSpec((tm, tn), lambda i, j: (i, j)),
        out_shape=jax.ShapeDtypeStruct((M, N), out_dtype),
        compiler_params=_cparams("parallel", "parallel"),
        name="linear",
    )(x, w)


def _proj_ln_kernel(a_ref, w_ref, h_ref, g_ref, b_ref, of_ref, ob_ref):
    tm = a_ref.shape[0]
    rp = tm // PROJ_ROW_PARTS
    parts = [slice(p * rp, (p + 1) * rp) for p in range(PROJ_ROW_PARTS)]
    mix = [None] * PROJ_ROW_PARTS

    def s_proj(c):
        mix[c] = _dot(a_ref[parts[c], :], w_ref[...])

    def s_norm(c):
        out = _layer_norm_rows(ALPHA * h_ref[parts[c], :] + mix[c], g_ref[...], b_ref[...])
        of_ref[parts[c], :] = out
        ob_ref[parts[c], :] = out.astype(BF16)

    _emit_skewed((s_proj, s_norm), PROJ_ROW_PARTS)


def proj_ln(a, w, h, g, b, tm=512):
    M, K = a.shape
    D = w.shape[1]
    return pl.pallas_call(
        _proj_ln_kernel,
        grid=(M // tm,),
        in_specs=[pl.BlockSpec((tm, K), lambda i: (i, 0)),
                  pl.BlockSpec((K, D), lambda i: (0, 0)),
                  pl.BlockSpec((tm, D), lambda i: (i, 0)),
                  pl.BlockSpec((1, D), lambda i: (0, 0)),
                  pl.BlockSpec((1, D), lambda i: (0, 0))],
        out_specs=[pl.BlockSpec((tm, D), lambda i: (i, 0)),
                   pl.BlockSpec((tm, D), lambda i: (i, 0))],
        out_shape=[jax.ShapeDtypeStruct((M, D), F32), jax.ShapeDtypeStruct((M, D), BF16)],
        compiler_params=_cparams("parallel"),
        name="proj_ln",
    )(a, w, h, g.reshape(1, D), b.reshape(1, D))


def _router_kernel(h_ref, rwt_ref, rb_ref, su_ref, idx_ref, gate_ref, cnt_ref, base_ref):
    step = pl.program_id(0)
    tr = h_ref.shape[0]

    @pl.when(step == 0)
    def _():
        base_ref[...] = jnp.zeros_like(base_ref)

    logits = _dot_nt_split(rwt_ref[...], h_ref[...])
    scores = jax.nn.sigmoid(logits)
    sel = scores + rb_ref[...]
    s = [sel[e:e + 1, :] for e in range(N_EXPERTS)]
    sc = [scores[e:e + 1, :] for e in range(N_EXPERTS)]

    def top2sum(v):
        best = v[0] + v[1]
        for a in range(4):
            for b in range(a + 1, 4):
                if (a, b) != (0, 1):
                    best = jnp.maximum(best, v[a] + v[b])
        return best

    gs = [top2sum(s[4 * g:4 * g + 4]) for g in range(N_GROUPS)]
    best, gi = gs[0], jnp.zeros((1, tr), I32)
    for g in range(1, N_GROUPS):
        better = gs[g] > best
        gi = jnp.where(better, g, gi)
        best = jnp.where(better, gs[g], best)

    def pick_group(rows):
        out = []
        for j in range(EXPERTS_PER_GROUP):
            v = rows[j]
            for g in range(1, N_GROUPS):
                v = jnp.where(gi == g, rows[4 * g + j], v)
            out.append(v)
        return out

    v = pick_group(s)
    raw = pick_group(sc)
    m1, l1, r1 = v[0], jnp.zeros((1, tr), I32), raw[0]
    for j in range(1, 4):
        better = v[j] > m1
        l1 = jnp.where(better, j, l1)
        m1 = jnp.where(better, v[j], m1)
        r1 = jnp.where(better, raw[j], r1)
    m2 = jnp.full((1, tr), -jnp.inf, F32)
    l2 = jnp.zeros((1, tr), I32)
    r2 = jnp.zeros((1, tr), F32)
    for j in range(4):
        better = jnp.logical_and(l1 != j, v[j] > m2)
        l2 = jnp.where(better, j, l2)
        m2 = jnp.where(better, v[j], m2)
        r2 = jnp.where(better, raw[j], r2)
    den = r1 + r2
    first_low = l1 < l2
    lo = jnp.where(first_low, l1, l2)
    hi = jnp.where(first_low, l2, l1)
    g_lo = jnp.where(first_low, r1, r2) / den
    g_hi = jnp.where(first_low, r2, r1) / den
    gate_ref[...] = jnp.concatenate([g_lo, g_hi], axis=0)
    bucket = gi * PAIRS_PER_GROUP + 2 * lo + hi - 1 - jnp.where(lo == 2, 1, 0)

    bidx = lax.broadcasted_iota(I32, (N_BUCKETS_PAD, tr), 0)
    hit = bidx == bucket
    oh = jnp.where(hit, 1.0, 0.0)
    rank = _dot(oh.astype(BF16), su_ref[...]) + base_ref[...]
    rk = jnp.sum(jnp.where(hit, rank, 0.0), axis=0, keepdims=True).astype(I32)
    idx_ref[...] = jnp.concatenate([bucket, rk], axis=0)
    base_ref[...] = base_ref[...] + jnp.sum(oh, axis=1, keepdims=True)
    cnt_ref[...] = jnp.broadcast_to(base_ref[...], cnt_ref.shape)


def moe_router(h, router_w, router_b, tr=512):
    T, D = h.shape
    su = jnp.triu(jnp.ones((tr, tr), BF16), k=1)
    idx, gate, cnt = pl.pallas_call(
        _router_kernel,
        grid=(T // tr,),
        in_specs=[pl.BlockSpec((tr, D), lambda i: (i, 0)),
                  pl.BlockSpec((N_EXPERTS, D), lambda i: (0, 0)),
                  pl.BlockSpec((N_EXPERTS, 1), lambda i: (0, 0)),
                  pl.BlockSpec((tr, tr), lambda i: (0, 0))],
        out_specs=[pl.BlockSpec((2, tr), lambda i: (0, i)),
                   pl.BlockSpec((2, tr), lambda i: (0, i)),
                   pl.BlockSpec((N_BUCKETS_PAD, LANES), lambda i: (0, 0))],
        out_shape=[jax.ShapeDtypeStruct((2, T), I32), jax.ShapeDtypeStruct((2, T), F32),
                   jax.ShapeDtypeStruct((N_BUCKETS_PAD, LANES), F32)],
        scratch_shapes=[pltpu.VMEM((N_BUCKETS_PAD, 1), F32)],
        compiler_params=_cparams("arbitrary"),
        name="moe_router",
    )(h, router_w.T, router_b.reshape(N_EXPERTS, 1), su)
    return idx, gate, cnt[:N_BUCKETS, 0].astype(I32)


def _dispatch_kernel(pos_ref, h_ref, gate_ref, z_ref, xs_ref, aug, sems):
    del z_ref
    i, n = pl.program_id(0), pl.num_programs(0)
    td, D = h_ref.shape
    slot = i % 2
    stage = aug.at[slot]
    stage[:, :D] = h_ref[...]
    stage[:, D:] = gate_ref[...]

    def row_copy(s, r, dst_row):
        return pltpu.make_async_copy(aug.at[s, pl.ds(r, 1)], xs_ref.at[pl.ds(dst_row, 1)], sems.at[s])

    def issue(g, c):
        for j in range(DMA_UNROLL):
            r = g * DMA_UNROLL + j
            row_copy(slot, r, pos_ref[r]).start(priority=j % 2)
        return c

    lax.fori_loop(0, td // DMA_UNROLL, issue, 0)

    def drain(s):
        pltpu.make_async_copy(aug.at[s], xs_ref.at[pl.ds(0, td)], sems.at[s]).wait()

    pl.when(i > 0)(lambda: drain(1 - slot))
    pl.when(i == n - 1)(lambda: drain(slot))


def moe_dispatch(h, pos, gcols, n_rows, td=1024):
    T, D = h.shape
    W = D + GATE_LANES
    zeros = jnp.zeros((n_rows, W), h.dtype)
    return pl.pallas_call(
        _dispatch_kernel,
        grid=(T // td,),
        in_specs=[pl.BlockSpec((td,), lambda i: (i,), memory_space=pltpu.SMEM),
                  pl.BlockSpec((td, D), lambda i: (i, 0)),
                  pl.BlockSpec((td, GATE_LANES), lambda i: (i, 0)),
                  pl.BlockSpec(memory_space=pl.ANY)],
        out_specs=pl.BlockSpec(memory_space=pl.ANY),
        out_shape=jax.ShapeDtypeStruct((n_rows, W), h.dtype),
        scratch_shapes=[pltpu.VMEM((2, td, W), h.dtype), pltpu.SemaphoreType.DMA((2,))],
        input_output_aliases={3: 0},
        compiler_params=_cparams("arbitrary"),
        name="moe_dispatch",
    )(pos, h, gcols, zeros)


def _expert_kernel(ta_ref, tb_ref, nu_ref, x_ref, wgua, wda, wgub, wdb, o_ref):
    del ta_ref, tb_ref
    D = o_ref.shape[1]
    De = wda.shape[2]

    @pl.when(pl.program_id(0) < nu_ref[0])
    def _():
        xa = x_ref[...]
        x = xa[:, :D].astype(BF16)

        weights = ((wgua, wda), (wgub, wdb))
        st = [dict(), dict()]

        def s_in(c):
            st[c]["gu"] = _dot(x, weights[c][0][0, 0])

        def s_act(c):
            gu = st[c].pop("gu")
            st[c]["he"] = (jax.nn.silu(gu[:, :De]) * gu[:, De:]).astype(BF16)

        def s_out(c):
            st[c]["y"] = _dot(st[c].pop("he"), weights[c][1][0, 0])

        _emit_skewed((s_in, s_act, s_out), 2)
        o_ref[...] = xa[:, D:D + 1] * st[0]["y"] + xa[:, D + 1:D + 2] * st[1]["y"]

    @pl.when(pl.program_id(0) >= nu_ref[0])
    def _():
        o_ref[...] = jnp.zeros_like(o_ref)


def moe_experts(xs, tile_ea, tile_eb, n_used, w_gu, w_down, layer):
    n_rows, W = xs.shape
    _, E, De, D = w_down.shape
    n_tiles = n_rows // MOE_TILE

    def row_map(i, ta, tb, nu):
        return (jnp.maximum(jnp.minimum(i, nu[0] - 1), 0), 0)

    wa = lambda i, ta, tb, nu: (layer, ta[i], 0, 0)
    wb = lambda i, ta, tb, nu: (layer, tb[i], 0, 0)
    grid_spec = pltpu.PrefetchScalarGridSpec(
        num_scalar_prefetch=3,
        grid=(n_tiles,),
        in_specs=[pl.BlockSpec((MOE_TILE, W), row_map),
                  pl.BlockSpec((1, 1, D, 2 * De), wa), pl.BlockSpec((1, 1, De, D), wa),
                  pl.BlockSpec((1, 1, D, 2 * De), wb), pl.BlockSpec((1, 1, De, D), wb)],
        out_specs=pl.BlockSpec((MOE_TILE, D), lambda i, ta, tb, nu: (i, 0)),
    )
    return pl.pallas_call(
        _expert_kernel,
        grid_spec=grid_spec,
        out_shape=jax.ShapeDtypeStruct((n_rows, D), F32),
        compiler_params=_cparams("arbitrary"),
        name="moe_experts",
    )(tile_ea, tile_eb, n_used, xs, w_gu, w_down, w_gu, w_down)


def _combine_kernel(pos_ref, posn_ref, h_ref, lg_ref, lb_ref, os_ref, of_ref, ob_ref, buf, sems):
    i, n = pl.program_id(0), pl.num_programs(0)
    tc = h_ref.shape[0]
    slot = i % 2

    def row_copy(s, src_row, r):
        return pltpu.make_async_copy(os_ref.at[pl.ds(src_row, 1)], buf.at[s, pl.ds(r, 1)], sems.at[s])

    def fetch(p_ref, s):
        def body(g, c):
            for j in range(DMA_UNROLL):
                r = g * DMA_UNROLL + j
                row_copy(s, p_ref[r], r).start(priority=j % 2)
            return c
        lax.fori_loop(0, tc // DMA_UNROLL, body, 0)

    pl.when(i == 0)(lambda: fetch(pos_ref, slot))
    pl.when(i + 1 < n)(lambda: fetch(posn_ref, 1 - slot))

    pltpu.make_async_copy(os_ref.at[pl.ds(0, tc)], buf.at[slot], sems.at[slot]).wait()
    out = _layer_norm_rows(ALPHA * h_ref[...] + buf[slot], lg_ref[...], lb_ref[...])
    of_ref[...] = out
    ob_ref[...] = out.astype(BF16)


def moe_combine(osorted, pos, h, ln_g, ln_b, tc=1024):
    T, D = h.shape
    nt = T // tc
    return pl.pallas_call(
        _combine_kernel,
        grid=(nt,),
        in_specs=[pl.BlockSpec((tc,), lambda i: (i,), memory_space=pltpu.SMEM),
                  pl.BlockSpec((tc,), lambda i: (jnp.minimum(i + 1, nt - 1),), memory_space=pltpu.SMEM),
                  pl.BlockSpec((tc, D), lambda i: (i, 0)),
                  pl.BlockSpec((1, D), lambda i: (0, 0)),
                  pl.BlockSpec((1, D), lambda i: (0, 0)),
                  pl.BlockSpec(memory_space=pl.ANY)],
        out_specs=[pl.BlockSpec((tc, D), lambda i: (i, 0)),
                   pl.BlockSpec((tc, D), lambda i: (i, 0))],
        out_shape=[jax.ShapeDtypeStruct((T, D), F32), jax.ShapeDtypeStruct((T, D), BF16)],
        scratch_shapes=[pltpu.VMEM((2, tc, D), F32), pltpu.SemaphoreType.DMA((2,))],
        compiler_params=_cparams("arbitrary"),
        name="moe_combine",
    )(pos, pos, h, ln_g.reshape(1, D), ln_b.reshape(1, D), osorted)


_PAIRS = ((0, 1), (0, 2), (0, 3), (1, 2), (1, 3), (2, 3))


def grouped_moe_ln(h, router_w, router_b, w_gu, w_down, ln_g, ln_b, layer):
    T, D = h.shape
    idx, gate, cnt = moe_router(h, router_w, router_b)
    bucket, rank = idx[0], idx[1]
    tiles = (cnt + MOE_TILE - 1) // MOE_TILE
    tile_end = jnp.cumsum(tiles)
    start = (tile_end - tiles) * MOE_TILE
    pos = start[bucket] + rank
    n_tiles = T // MOE_TILE + N_BUCKETS
    tile_ids = jnp.arange(n_tiles, dtype=I32)
    tile_bucket = jnp.minimum(
        jnp.sum((tile_ids[:, None] >= tile_end[None, :]).astype(I32), axis=1), N_BUCKETS - 1)
    ea = jnp.array([g * EXPERTS_PER_GROUP + a for g in range(N_GROUPS) for a, _ in _PAIRS], I32)
    eb = jnp.array([g * EXPERTS_PER_GROUP + b for g in range(N_GROUPS) for _, b in _PAIRS], I32)
    n_used = tile_end[-1:].astype(I32)
    gcols = jnp.pad(gate.T, ((0, 0), (0, GATE_LANES - 2)))
    xs = moe_dispatch(h, pos, gcols, n_tiles * MOE_TILE)
    osorted = moe_experts(xs, ea[tile_bucket], eb[tile_bucket], n_used, w_gu, w_down, layer)
    return moe_combine(osorted, pos, h, ln_g, ln_b)


def _forget_kernel(h_ref, wft_ref, bf_ref, ui_ref, c_ref, carry_ref):
    @pl.when(pl.program_id(1) == 0)
    def _():
        carry_ref[...] = jnp.zeros_like(carry_ref)

    logits = _dot_nt_split(wft_ref[...], h_ref[...])
    log_f = jax.nn.log_sigmoid(logits + bf_ref[...])
    hi, mid, lo = _split3(log_f)
    ui = ui_ref[...]
    cum = _dot(hi, ui) + _dot(mid, ui) + _dot(lo, ui) + carry_ref[...]
    c_ref[0] = cum * LOG2E
    carry_ref[...] = cum[:, -1:]


def fox_forget_cumsum(h, w_f, b_f, B, S, ts=512):
    T, D = h.shape
    H = w_f.shape[1]
    ui = jnp.triu(jnp.ones((ts, ts), BF16))
    ns = S // ts
    return pl.pallas_call(
        _forget_kernel,
        grid=(B, ns),
        in_specs=[pl.BlockSpec((ts, D), lambda b, i: (b * ns + i, 0)),
                  pl.BlockSpec((H, D), lambda b, i: (0, 0)),
                  pl.BlockSpec((H, 1), lambda b, i: (0, 0)),
                  pl.BlockSpec((ts, ts), lambda b, i: (0, 0))],
        out_specs=pl.BlockSpec((1, H, ts), lambda b, i: (b, 0, i)),
        out_shape=jax.ShapeDtypeStruct((B, H, S), F32),
        scratch_shapes=[pltpu.VMEM((H, 1), F32)],
        compiler_params=_cparams("parallel", "arbitrary"),
        name="fox_forget",
    )(h, w_f.T, b_f.reshape(H, 1), ui)


def _head_pair_masks(q):
    lane = lax.broadcasted_iota(I32, q.shape, 1)
    zero = jnp.zeros_like(q)
    return jnp.where(lane < HEAD_DIM, q, zero), jnp.where(lane >= HEAD_DIM, q, zero)


def _lane_tile(x, n):
    return x if n == 1 else jnp.concatenate([x] * n, axis=1)


def _causal_steps(nq, ratio, descending):
    qi, kb = [], []
    for i in range(nq):
        ks = list(range(ratio * (i + 1)))
        for k in (reversed(ks) if descending else ks):
            qi.append(i)
            kb.append(k)
    return jnp.array(qi, I32), jnp.array(kb, I32)


def _emit_skewed(stages, n_chains):
    for t in range(n_chains + len(stages) - 1):
        for s in reversed(range(len(stages))):
            if 0 <= t - s < n_chains:
                stages[s](t - s)


def _pair(p):
    return slice(p * LANES, (p + 1) * LANES)


def _fox_kernel(qi_ref, kb_ref, q_ref, k_ref, v_ref, crow_ref, ccol_ref, og_ref, o_ref, qm, ct, m, acc):
    hg, step_id = pl.program_id(1), pl.program_id(2)
    i, j = qi_ref[step_id], kb_ref[step_id]
    tq, tk = q_ref.shape[1], k_ref.shape[1]
    n_heads = qm.shape[0]

    @pl.when(j == 0)
    def _():
        q = q_ref[0]
        cc = ccol_ref[0]
        head = lax.broadcasted_iota(I32, cc.shape, 1)
        for hh in range(n_heads):
            qm[hh] = _head_pair_masks(q[:, _pair(hh // 2)])[hh % 2]
            col = jnp.sum(jnp.where(head == n_heads * hg + hh, cc, 0.0), axis=1, keepdims=True)
            ct[hh] = jnp.broadcast_to(col, (tq, LANES))
        m[...] = jnp.full(m.shape, -jnp.inf, F32)
        acc[...] = jnp.zeros_like(acc)

    def step(masked):
        ks = [k_ref[0, :, _pair(p)] for p in range(n_heads // 2)]
        ones = jnp.ones((tk, LANES), BF16)
        v_ones = [jnp.concatenate([v_ref[0, :, _pair(p)], ones], axis=1) for p in range(n_heads // 2)]
        crow = crow_ref[0]
        if masked:
            row = lax.broadcasted_iota(I32, (tq, tk), 0)
            col = lax.broadcasted_iota(I32, (tq, tk), 1)
            keep = col <= row
        rp = tq // ATT_ROW_PARTS
        chains = [(hh, slice(p * rp, (p + 1) * rp)) for hh in range(n_heads) for p in range(ATT_ROW_PARTS)]
        st = [dict() for _ in chains]

        def s_scores(c):
            hh, rows = chains[c]
            st[c]["s"] = _dot_nt(qm[hh, rows, :], ks[hh // 2])

        def s_softmax(c):
            hh, rows = chains[c]
            s = st[c].pop("s") - crow[hh // 2, hh % 2:hh % 2 + 1, :]
            if masked:
                s = jnp.where(keep[rows, :], s, -jnp.inf)
            c_t = ct[hh, rows, :]
            m_prev = m[hh, rows, :]
            m_new = jnp.maximum(m_prev, jnp.max(s, axis=1, keepdims=True) + c_t)
            st[c]["alpha"] = jnp.exp2(m_prev - m_new)
            st[c]["p"] = jnp.exp2(s - _lane_tile(m_new - c_t, tk // LANES)).astype(BF16)
            m[hh, rows, :] = m_new

        def s_values(c):
            hh, rows = chains[c]
            acc[hh, rows, :] = (_lane_tile(st[c].pop("alpha"), 2) * acc[hh, rows, :]
                                + _dot(st[c].pop("p"), v_ones[hh // 2]))

        _emit_skewed((s_scores, s_softmax, s_values), len(chains))

    pl.when(j < i)(lambda: step(False))
    pl.when(j == i)(lambda: step(True))

    @pl.when(j == i)
    def _():
        lane = lax.broadcasted_iota(I32, (tq, LANES), 1)
        for p in range(n_heads // 2):
            a0, a1 = acc[2 * p], acc[2 * p + 1]
            o = jnp.where(lane < HEAD_DIM, a0[:, :LANES] / a0[:, LANES:], a1[:, :LANES] / a1[:, LANES:])
            o_ref[0, :, _pair(p)] = (o * jax.nn.sigmoid(og_ref[0, :, _pair(p)].astype(F32))).astype(BF16)


def fox_attention(qkv, og, c, B, S, tq=512):
    T = qkv.shape[0]
    D = N_HEADS * HEAD_DIM
    hp_n = N_HEADS // 2
    nq = S // tq
    qkv3 = qkv.reshape(B, S, 3 * D)
    crow = c.reshape(B, hp_n, 2, S)
    ccol = jnp.transpose(c, (0, 2, 1))
    qi, kb = _causal_steps(nq, 1, descending=False)
    w = ATT_PAIRS * LANES
    hg_n = hp_n // ATT_PAIRS
    nh = 2 * ATT_PAIRS
    q_blk = lambda b, hg, s, qi, kb: (b, qi[s], hg)
    kv_blk = lambda off: (lambda b, hg, s, qi, kb: (b, kb[s], off + hg))
    grid_spec = pltpu.PrefetchScalarGridSpec(
        num_scalar_prefetch=2,
        grid=(B, hg_n, qi.shape[0]),
        in_specs=[pl.BlockSpec((1, tq, w), q_blk),
                  pl.BlockSpec((1, tq, w), kv_blk(hg_n)),
                  pl.BlockSpec((1, tq, w), kv_blk(2 * hg_n)),
                  pl.BlockSpec((1, ATT_PAIRS, 2, tq), lambda b, hg, s, qi, kb: (b, hg, 0, kb[s])),
                  pl.BlockSpec((1, tq, N_HEADS), lambda b, hg, s, qi, kb: (b, qi[s], 0)),
                  pl.BlockSpec((1, tq, w), q_blk)],
        out_specs=pl.BlockSpec((1, tq, w), q_blk),
        scratch_shapes=[pltpu.VMEM((nh, tq, LANES), BF16),
                        pltpu.VMEM((nh, tq, LANES), F32), pltpu.VMEM((nh, tq, LANES), F32),
                        pltpu.VMEM((nh, tq, 2 * LANES), F32)],
    )
    out = pl.pallas_call(
        _fox_kernel,
        grid_spec=grid_spec,
        out_shape=jax.ShapeDtypeStruct((B, S, D), BF16),
        compiler_params=_cparams("parallel", "parallel", "arbitrary"),
        name="fox_attention",
    )(qi, kb, qkv3, qkv3, qkv3, crow, ccol, og.reshape(B, S, D))
    return out.reshape(T, D)


def _sb_kernel(qi_ref, kb_ref, q_ref, k_ref, v_ref, u_ref, o_ref, qm, rs, acc):
    step_id = pl.program_id(2)
    i, kb = qi_ref[step_id], kb_ref[step_id]
    tq, tk = q_ref.shape[1], k_ref.shape[1]
    ratio = tq // tk

    n_heads = qm.shape[0]

    @pl.when(kb == ratio * (i + 1) - 1)
    def _():
        q = q_ref[0]
        for hh in range(n_heads):
            qm[hh] = _head_pair_masks(q[:, _pair(hh // 2)])[hh % 2]
        rs[...] = jnp.zeros_like(rs)
        acc[...] = jnp.zeros_like(acc)

    def step(masked):
        ks = [k_ref[0, :, _pair(p)] for p in range(n_heads // 2)]
        vs = [v_ref[0, :, _pair(p)] for p in range(n_heads // 2)]
        u = u_ref[...]
        if masked:
            row = lax.broadcasted_iota(I32, (tq, tk), 0)
            col = lax.broadcasted_iota(I32, (tq, tk), 1)
            keep = col - row < i * tq - kb * tk
        rp = tq // ATT_ROW_PARTS
        chains = [(hh, slice(p * rp, (p + 1) * rp)) for hh in range(n_heads) for p in range(ATT_ROW_PARTS)]
        st = [dict() for _ in chains]

        def s_scores(c):
            x, rows = chains[c]
            st[c]["z"] = _dot_nt(qm[x, rows, :], ks[x // 2])

        def s_logs(c):
            x, rows = chains[c]
            z = st[c]["z"]
            zb = z.astype(BF16)
            e = jnp.exp2((jnp.abs(z) * (-LOG2E)).astype(BF16))
            nl = jnp.maximum(zb, 0) + jnp.log(jnp.ones((), BF16) + e)
            if masked:
                nl = jnp.where(keep[rows, :], nl, jnp.zeros((), BF16))
            st[c]["nl"] = nl

        def s_suffix(c):
            st[c]["ci"] = _dot(st[c].pop("nl"), u)

        def s_weights(c):
            x, rows = chains[c]
            prev = rs[x, rows, :]
            ci = st[c].pop("ci")
            a = jnp.exp2((st[c].pop("z") - ci - _lane_tile(prev, tk // LANES)) * LOG2E)
            if masked:
                a = jnp.where(keep[rows, :], a, 0.0)
            st[c]["a"] = a.astype(BF16)
            rs[x, rows, :] = prev + ci[:, 0:1]

        def s_values(c):
            x, rows = chains[c]
            acc[x, rows, :] = acc[x, rows, :] + _dot(st[c].pop("a"), vs[x // 2])

        _emit_skewed((s_scores, s_logs, s_suffix, s_weights, s_values), len(chains))

    pl.when(kb >= ratio * i)(lambda: step(True))
    pl.when(kb < ratio * i)(lambda: step(False))

    @pl.when(kb == 0)
    def _():
        lane = lax.broadcasted_iota(I32, (tq, LANES), 1)
        for p in range(n_heads // 2):
            o_ref[0, :, _pair(p)] = jnp.where(lane < HEAD_DIM, acc[2 * p], acc[2 * p + 1]).astype(BF16)


def stick_breaking_attention(qkv, B, S, tq=512, tk=256):
    T = qkv.shape[0]
    D = N_HEADS * HEAD_DIM
    hp_n = N_HEADS // 2
    nq = S // tq
    qkv3 = qkv.reshape(B, S, 3 * D)
    u = jnp.tril(jnp.ones((tk, tk), BF16))
    qi, kb = _causal_steps(nq, tq // tk, descending=True)
    w = ATT_PAIRS * LANES
    hg_n = hp_n // ATT_PAIRS
    nh = 2 * ATT_PAIRS
    q_blk = lambda b, hg, s, qi, kb: (b, qi[s], hg)
    kv_blk = lambda off: (lambda b, hg, s, qi, kb: (b, kb[s], off + hg))
    grid_spec = pltpu.PrefetchScalarGridSpec(
        num_scalar_prefetch=2,
        grid=(B, hg_n, qi.shape[0]),
        in_specs=[pl.BlockSpec((1, tq, w), q_blk),
                  pl.BlockSpec((1, tk, w), kv_blk(hg_n)),
                  pl.BlockSpec((1, tk, w), kv_blk(2 * hg_n)),
                  pl.BlockSpec((tk, tk), lambda b, hg, s, qi, kb: (0, 0))],
        out_specs=pl.BlockSpec((1, tq, w), q_blk),
        scratch_shapes=[pltpu.VMEM((nh, tq, LANES), BF16),
                        pltpu.VMEM((nh, tq, LANES), F32), pltpu.VMEM((nh, tq, LANES), F32)],
    )
    out = pl.pallas_call(
        _sb_kernel,
        grid_spec=grid_spec,
        out_shape=jax.ShapeDtypeStruct((B, S, D), BF16),
        compiler_params=_cparams("parallel", "parallel", "arbitrary"),
        name="stick_breaking",
    )(qi, kb, qkv3, qkv3, qkv3, u)
    return out.reshape(T, D)


def _gmlp_kernel(h_ref, win_ref, bin_ref, lg_ref, lb_ref, ws_ref, bst_ref, o_ref):
    tm = h_ref.shape[0]
    dg = o_ref.shape[1]
    row = lax.broadcasted_iota(I32, (CHUNK, CHUNK), 0)
    col = lax.broadcasted_iota(I32, (CHUNK, CHUNK), 1)
    gw = dg // GMLP_GROUPS
    wcs = [jnp.where(col <= row, ws_ref[g], 0.0).astype(BF16) for g in range(GMLP_GROUPS)]
    rp = tm // GMLP_ROW_PARTS
    parts = [slice(p * rp, (p + 1) * rp) for p in range(GMLP_ROW_PARTS)]
    zs = [None] * GMLP_ROW_PARTS

    def s_proj(p):
        zs[p] = _dot(h_ref[parts[p], :], win_ref[...])

    def s_gate(p):
        z = jax.nn.gelu(zs[p] + bin_ref[...])
        u = z[:, :dg]
        v = _layer_norm_rows(z[:, dg:], lg_ref[...], lb_ref[...]).astype(BF16)
        for g in range(GMLP_GROUPS):
            cs = slice(g * gw, (g + 1) * gw)
            for c in range(rp // CHUNK):
                rs = slice(c * CHUNK, (c + 1) * CHUNK)
                sv = _dot(wcs[g], v[rs, cs]) + bst_ref[:, g:g + 1]
                o_ref[pl.ds(p * rp + c * CHUNK, CHUNK), cs] = (u[rs, cs] * sv).astype(BF16)

    _emit_skewed((s_proj, s_gate), GMLP_ROW_PARTS)


def gmlp_gate(hb, w_in, b_in, ln_g, ln_b, w_s, b_s, tm=512):
    T, D = hb.shape
    dg = w_in.shape[1] // 2
    return pl.pallas_call(
        _gmlp_kernel,
        grid=(T // tm,),
        in_specs=[pl.BlockSpec((tm, D), lambda i: (i, 0)),
                  pl.BlockSpec((D, 2 * dg), lambda i: (0, 0)),
                  pl.BlockSpec((1, 2 * dg), lambda i: (0, 0)),
                  pl.BlockSpec((1, dg), lambda i: (0, 0)),
                  pl.BlockSpec((1, dg), lambda i: (0, 0)),
                  pl.BlockSpec((GMLP_GROUPS, CHUNK, CHUNK), lambda i: (0, 0, 0)),
                  pl.BlockSpec((CHUNK, GMLP_GROUPS), lambda i: (0, 0))],
        out_specs=pl.BlockSpec((tm, dg), lambda i: (i, 0)),
        out_shape=jax.ShapeDtypeStruct((T, dg), BF16),
        compiler_params=_cparams("parallel"),
        name="gmlp_gate",
    )(hb, w_in, b_in.reshape(1, -1), ln_g.reshape(1, -1), ln_b.reshape(1, -1), w_s, b_s.T)


RW_LANES = 4 * HEAD_DIM
RW_STREAMS = 6


def _seg_sum(x, bones):
    hi = x.astype(BF16)
    lo = (x - hi.astype(F32)).astype(BF16)
    return _dot(hi, bones) + _dot(lo, bones)


def _rwkv_prep_kernel(seq_len, h_ref, hp_ref, mu_ref, wrkv_ref, w1_ref, w2_ref, a1_ref, a2_ref,
                      g1_ref, g2_ref, w0_ref, a0_ref, kk_ref, ka_ref, bones_ref, lchunk_ref,
                      dec_ref, str_ref):
    i = pl.program_id(0)
    tm, D = h_ref.shape
    h = h_ref[...]
    at_start = (i * tm) % seq_len == 0
    prev_row = jnp.where(at_start, 0.0, hp_ref[7:8, :])
    row = lax.broadcasted_iota(I32, (tm, D), 0)
    h_prev = jnp.where(row == 0, prev_row, pltpu.roll(h, 1, axis=0))
    dx = h_prev - h

    def mixed(c):
        return (h + dx * mu_ref[c:c + 1, :]).astype(BF16)

    r = _dot(mixed(0), wrkv_ref[0])
    k = _dot(mixed(1), wrkv_ref[1])
    v = _dot(mixed(2), wrkv_ref[2])
    d = w0_ref[...] + _dot(jnp.tanh(_dot(mixed(3), w1_ref[...])).astype(BF16), w2_ref[...])
    lw = -jnp.exp(-jax.nn.softplus(-d) - 0.5)
    a = jax.nn.sigmoid(a0_ref[...] + _dot(_dot(mixed(4), a1_ref[...]).astype(BF16), a2_ref[...]))
    g = _dot(jax.nn.sigmoid(_dot(mixed(5), g1_ref[...])).astype(BF16), g2_ref[...])
    kk = k * kk_ref[...]
    bones = bones_ref[...]
    norm_sq = jnp.concatenate(
        [_seg_sum(jnp.square(kk[:, q * RW_LANES:(q + 1) * RW_LANES]), bones) for q in range(D // RW_LANES)],
        axis=1)
    kkn = kk / jnp.maximum(jnp.sqrt(norm_sq), 1e-12)
    hi, mid, lo = _split3(lw)
    lchunk = lchunk_ref[...]
    cl = _dot(lchunk, hi) + _dot(lchunk, mid) + _dot(lchunk, lo)
    dec_ref[:, :D] = cl
    dec_ref[:, D:] = cl - lw
    streams = (r, k * (1.0 + (a - 1.0) * ka_ref[...]), v, kkn, kkn * a, g)
    for s, val in enumerate(streams):
        str_ref[:, s * D:(s + 1) * D] = val.astype(BF16)


def _block_ones():
    idx = jnp.arange(RW_LANES) // HEAD_DIM
    return (idx[:, None] == idx[None, :]).astype(BF16)


def rwkv_prep(h, mu, w_rkv, w0, w1, w2, a0, a1, a2, g1, g2, k_k, k_a, S, tm=256):
    T, D = h.shape
    row = lambda i: (i, 0)
    full2 = lambda i: (0, 0)
    vec = lambda x: x.reshape(1, D)
    bf = lambda x: x.astype(BF16)
    pos = jnp.arange(tm)
    lchunk = jnp.logical_and(pos[:, None] // RW_CHUNK == pos[None, :] // RW_CHUNK,
                             pos[None, :] <= pos[:, None]).astype(BF16)
    outs = pl.pallas_call(
        functools.partial(_rwkv_prep_kernel, S),
        grid=(T // tm,),
        in_specs=[pl.BlockSpec((tm, D), row),
                  pl.BlockSpec((8, D), lambda i: (jnp.maximum(i * (tm // 8) - 1, 0), 0)),
                  pl.BlockSpec((6, D), full2),
                  pl.BlockSpec((3, D, D), lambda i: (0, 0, 0)),
                  pl.BlockSpec(w1.shape, full2), pl.BlockSpec(w2.shape, full2),
                  pl.BlockSpec(a1.shape, full2), pl.BlockSpec(a2.shape, full2),
                  pl.BlockSpec(g1.shape, full2), pl.BlockSpec(g2.shape, full2),
                  pl.BlockSpec((1, D), full2), pl.BlockSpec((1, D), full2),
                  pl.BlockSpec((1, D), full2), pl.BlockSpec((1, D), full2),
                  pl.BlockSpec((RW_LANES, RW_LANES), full2),
                  pl.BlockSpec((tm, tm), full2)],
        out_specs=[pl.BlockSpec((tm, 2 * D), row), pl.BlockSpec((tm, RW_STREAMS * D), row)],
        out_shape=[jax.ShapeDtypeStruct((T, 2 * D), F32), jax.ShapeDtypeStruct((T, RW_STREAMS * D), BF16)],
        compiler_params=_cparams("parallel"),
        name="rwkv_prep",
    )(h, h, mu, bf(w_rkv), bf(w1), bf(w2), bf(a1), bf(a2), bf(g1), bf(g2),
      vec(w0), vec(a0), vec(k_k), vec(k_a), _block_ones(), lchunk)
    return outs


def _rwkv_local(r, cl, clx, k, v, kk, b, cst):
    C = RW_CHUNK
    mbd, eye, strict, incl, levels = cst
    eye_f = jnp.where(eye, 1.0, 0.0)
    eye_b = eye_f.astype(BF16)

    def each(f, *lists):
        return [f(*xs) for xs in zip(*lists)]

    def bd(y):
        return jnp.where(mbd, jnp.concatenate([y.astype(BF16)] * 4, axis=0), jnp.zeros((), BF16))

    def rows(*parts):
        return jnp.concatenate([p.astype(BF16) for p in parts], axis=0)

    cl_last = each(lambda c: c[C - 1:C, :], cl)
    kkt = each(lambda a, c: a * jnp.exp(c), kk, clx)
    rt = each(lambda a, c: a * jnp.exp(c), r, cl)
    g_inv = each(lambda c: jnp.exp(-c), cl)
    g_tail = each(lambda cl_, c: jnp.exp(cl_ - c), cl_last, cl)
    kk_r = each(rows, kkt, rt)
    ab_k = each(lambda l, a, g: _dot_nt(l, bd(a * g)), kk_r, k, g_inv)
    ab_b = each(lambda l, a, g: _dot_nt(l, bd(a * g)), kk_r, b, g_inv)
    a_k = each(lambda x: jnp.where(strict, x[:C], 0.0), ab_k)
    b_k = each(lambda x: jnp.where(incl, x[C:], 0.0), ab_k)
    a_b = each(lambda x: jnp.where(strict, x[:C], 0.0), ab_b)
    b_b = each(lambda x: jnp.where(incl, x[C:], 0.0), ab_b)
    kh_t = each(lambda a, g: _dot_nt(eye_b, bd(a * g)), k, g_tail)
    bh_t = each(lambda a, g: _dot_nt(eye_b, bd(a * g)), b, g_tail)

    dm = each(lambda a: eye_f - jnp.where(levels[0], a, 0.0), a_b)
    for lvl in levels[1:]:
        w = each(lambda d, a: _dot(d.astype(BF16), bd(jnp.where(lvl, a, 0.0))), dm, a_b)
        dm = each(lambda d, w_: d - _dot(w_.astype(BF16), bd(d)), dm, w)
    tm_b = each(lambda d: d.astype(BF16), dm)

    xv = each(lambda a, b_, c, v_: _dot(rows(a, b_, c), bd(v_)), a_k, b_k, kh_t, v)
    kp = each(lambda t, a: _dot(t, bd(a)), tm_b, kkt)
    vp = each(lambda t, x: _dot(t, bd(x[:C])), tm_b, xv)
    bb_bh = each(rows, b_b, bh_t)
    xkp = each(lambda l, a: _dot(l, bd(a)), bb_bh, kp)
    xvp = each(lambda l, a: _dot(l, bd(a)), bb_bh, vp)
    y_loc = each(lambda x, z: x[C:2 * C] - z[:C], xv, xvp)
    r_eff = each(lambda a, z: a - z[:C], rt, xkp)
    m = each(lambda c, z: eye_f * jnp.exp(c) - z[C:], cl_last, xkp)
    n = each(lambda x, z: x[2 * C:] - z[C:], xv, xvp)
    return r_eff, y_loc, m, n


def _rwkv_apply(r_eff, y_loc, m, n, P, mbd):
    C = RW_CHUNK

    def each(f, *lists):
        return [f(*xs) for xs in zip(*lists)]

    def bd(y):
        return jnp.where(mbd, jnp.concatenate([y.astype(BF16)] * 4, axis=0), jnp.zeros((), BF16))

    def rows(*parts):
        return jnp.concatenate([p.astype(BF16) for p in parts], axis=0)

    p_hi = each(lambda p: p.astype(BF16), P)
    m_hi = each(lambda a: a.astype(BF16), m)
    z_hi = each(lambda a, mh, m_, ph: _dot(rows(a, mh, m_ - mh.astype(F32)), bd(ph)), r_eff, m_hi, m, p_hi)
    z_lo = each(lambda a, mh, p, ph: _dot(rows(a, mh), bd(p - ph.astype(F32))), r_eff, m_hi, P, p_hi)
    y = each(lambda zh, zl, yl: zh[:C] + zl[:C] + yl, z_hi, z_lo, y_loc)
    p_new = each(lambda zh, zl, n_: zh[C:2 * C] + zh[2 * C:] + zl[C:] + n_, z_hi, z_lo, n)
    return y, p_new


def _rwkv_scan_kernel(dec_ref, str_ref, rk_ref, gg_ref, gb_ref, bones_ref, o_ref, state):
    C = RW_CHUNK
    D = o_ref.shape[1]

    @pl.when(pl.program_id(1) == 0)
    def _():
        state[...] = jnp.zeros_like(state)

    t = lax.broadcasted_iota(I32, (C, RW_LANES), 0)
    i = lax.broadcasted_iota(I32, (C, RW_LANES), 1) % C
    levels = []
    for sh in range(6):
        tb, ib = t >> sh, i >> sh
        levels.append(jnp.logical_and(tb == ib + 1, (tb & 1) == 1))
    rr = lax.broadcasted_iota(I32, (RW_LANES, RW_LANES), 0) // HEAD_DIM
    cc = lax.broadcasted_iota(I32, (RW_LANES, RW_LANES), 1) // HEAD_DIM
    cst = (rr == cc, i == t, i < t, i <= t, levels)
    bones = bones_ref[...]

    n_groups = D // RW_LANES
    n_chunks = o_ref.shape[0] // C
    tiles = [(slice(c * C, (c + 1) * C), slice(q * RW_LANES, (q + 1) * RW_LANES))
             for c in range(n_chunks) for q in range(n_groups)]

    def cut(ref, s):
        return [ref[rs, pl.ds(s * D + ls.start, RW_LANES)] for rs, ls in tiles]

    cl, clx = cut(dec_ref, 0), cut(dec_ref, 1)
    r, k, v, kk, b, g = (cut(str_ref, s) for s in range(RW_STREAMS))
    r_eff, y_loc, m, n = _rwkv_local(r, cl, clx, k, v, kk, b, cst)
    p = [state[q] for q in range(n_groups)]
    ys = []
    for c in range(n_chunks):
        part = slice(c * n_groups, (c + 1) * n_groups)
        y_c, p = _rwkv_apply(r_eff[part], y_loc[part], m[part], n[part], p, cst[0])
        ys += y_c
    for q in range(n_groups):
        state[q] = p[q]
    stack = lambda xs: jnp.concatenate(xs, axis=0)
    y = stack(ys)
    mean = _seg_sum(y, bones) * (1.0 / HEAD_DIM)
    yc = y - mean
    var = _seg_sum(yc * yc, bones) * (1.0 / HEAD_DIM)
    yn = yc * lax.rsqrt(var + GN_EPS)
    rkr = stack([r[idx].astype(F32) * k[idx].astype(F32) * rk_ref[:, ls]
                 for idx, (rs, ls) in enumerate(tiles)])
    dots = _seg_sum(rkr, bones)
    for idx, (rs, ls) in enumerate(tiles):
        part = slice(idx * C, (idx + 1) * C)
        out = yn[part] * gg_ref[:, ls] + gb_ref[:, ls] + dots[part] * v[idx]
        o_ref[rs, ls] = (out * g[idx]).astype(BF16)


def rwkv_scan(dec, streams, r_k, gn_g, gn_b, B, S):
    T = dec.shape[0]
    D = dec.shape[1] // 2
    rows_per_step = RW_CHUNK * RW_CHUNKS_PER_STEP
    nc = S // rows_per_step
    row = lambda bi, ci: (bi * nc + ci, 0)
    full2 = lambda bi, ci: (0, 0)
    return pl.pallas_call(
        _rwkv_scan_kernel,
        grid=(B, nc),
        in_specs=[pl.BlockSpec((rows_per_step, 2 * D), row), pl.BlockSpec((rows_per_step, RW_STREAMS * D), row)]
        + [pl.BlockSpec((1, D), full2)] * 3 + [pl.BlockSpec((RW_LANES, RW_LANES), full2)],
        out_specs=pl.BlockSpec((rows_per_step, D), row),
        out_shape=jax.ShapeDtypeStruct((T, D), BF16),
        scratch_shapes=[pltpu.VMEM((D // RW_LANES, HEAD_DIM, RW_LANES), F32)],
        compiler_params=_cparams("parallel", "arbitrary"),
        name="rwkv_scan",
    )(dec, streams, r_k.reshape(1, D), gn_g.reshape(1, D), gn_b.reshape(1, D), _block_ones())


def rwkv_layer(h, mu, w_rkv, w0, w1, w2, a0, a1, a2, g1, g2, k_k, k_a, r_k, gn_g, gn_b, w_out,
               ln_g, ln_b, B, S):
    dec, streams = rwkv_prep(h, mu, w_rkv, w0, w1, w2, a0, a1, a2, g1, g2, k_k, k_a, S)
    y = rwkv_scan(dec, streams, r_k, gn_g, gn_b, B, S)
    return proj_ln(y, w_out.astype(BF16), h, ln_g, ln_b)


def fox_layer(h, hb, w_in, b_f, w_out, ln_g, ln_b, B, S):
    D = h.shape[1]
    scale = HEAD_DIM ** -0.5 * LOG2E
    w_qkv = jnp.concatenate([w_in[:, :D] * scale, w_in[:, D:3 * D]], axis=1).astype(BF16)
    w_f = w_in[:, 3 * D:3 * D + N_HEADS]
    w_og = w_in[:, 3 * D + N_HEADS:].astype(BF16)
    qkv = linear(hb, w_qkv, BF16)
    og = linear(hb, w_og, BF16)
    c = fox_forget_cumsum(h, w_f, b_f, B, S)
    o = fox_attention(qkv, og, c, B, S)
    return proj_ln(o, w_out.astype(BF16), h, ln_g, ln_b)


def sb_layer(h, hb, w_in, w_out, ln_g, ln_b, B, S):
    D = h.shape[1]
    scale = HEAD_DIM ** -0.5
    w_qkv = jnp.concatenate([w_in[:, :D] * scale, w_in[:, D:]], axis=1).astype(BF16)
    qkv = linear(hb, w_qkv, BF16)
    o = stick_breaking_attention(qkv, B, S)
    return proj_ln(o, w_out.astype(BF16), h, ln_g, ln_b)


def gmlp_layer(h, hb, w_in, b_in, gm_ln_g, gm_ln_b, w_s, b_s, w_out, ln_g, ln_b):
    y = gmlp_gate(hb, w_in.astype(BF16), b_in, gm_ln_g, gm_ln_b, w_s, b_s)
    return proj_ln(y, w_out.astype(BF16), h, ln_g, ln_b)


def kernel(x, ln1_g, ln1_b, ln2_g, ln2_b, fox_w_in, fox_b_f, fox_w_out, gm_w_in, gm_b_in, gm_ln_g,
           gm_ln_b, gm_w_s, gm_b_s, gm_w_out, sb_w_in, sb_w_out, rw_mu, rw_w_rkv, rw_w0, rw_w1, rw_w2,
           rw_a0, rw_a1, rw_a2, rw_g1, rw_g2, rw_k_k, rw_k_a, rw_r_k, rw_gn_g, rw_gn_b, rw_w_out,
           router_w, router_b, moe_w_gate, moe_w_up, moe_w_down):
    B, S, D = x.shape
    h = x.reshape(B * S, D)
    hb = h.astype(BF16)
    w_gu_b = jnp.concatenate([moe_w_gate.astype(BF16), moe_w_up.astype(BF16)], axis=-1)
    w_down_b = moe_w_down.astype(BF16)
    for i in range(DEPTH):
        kind, j = i % 4, i // 4
        if kind == 0:
            h, hb = fox_layer(h, hb, fox_w_in[j], fox_b_f[j], fox_w_out[j], ln1_g[i], ln1_b[i], B, S)
        elif kind == 1:
            h, hb = gmlp_layer(h, hb, gm_w_in[j], gm_b_in[j], gm_ln_g[j], gm_ln_b[j], gm_w_s[j],
                               gm_b_s[j], gm_w_out[j], ln1_g[i], ln1_b[i])
        elif kind == 2:
            h, hb = sb_layer(h, hb, sb_w_in[j], sb_w_out[j], ln1_g[i], ln1_b[i], B, S)
        else:
            h, hb = rwkv_layer(h, rw_mu[j], rw_w_rkv[j], rw_w0[j], rw_w1[j], rw_w2[j], rw_a0[j],
                               rw_a1[j], rw_a2[j], rw_g1[j], rw_g2[j], rw_k_k[j], rw_k_a[j], rw_r_k[j],
                               rw_gn_g[j], rw_gn_b[j], rw_w_out[j], ln1_g[i], ln1_b[i], B, S)
        h, hb = grouped_moe_ln(h, router_w, router_b, w_gu_b, w_down_b, ln2_g[i], ln2_b[i], i)
    return h.reshape(B, S, D)
```

```python
import functools

import jax
import jax.numpy as jnp
from jax import lax
from jax.experimental import pallas as pl
from jax.experimental.pallas import tpu as pltpu

F32 = jnp.float32
BF16 = jnp.bfloat16
I32 = jnp.int32

N_HEADS = 16
HEAD_DIM = 64
CHUNK = 128
GMLP_GROUPS = 8
N_EXPERTS = 16
N_GROUPS = 4
EXPERTS_PER_GROUP = 4
PAIRS_PER_GROUP = 6
N_BUCKETS = N_GROUPS * PAIRS_PER_GROUP
N_BUCKETS_PAD = 32
PROJ_ROW_PARTS = 2
GMLP_ROW_PARTS = 2
DMA_UNROLL = 8
GATE_LANES = 128
DEPTH = 4
ALPHA = (2 * DEPTH) ** 0.25
LN_EPS = 1e-5
GN_EPS = 64e-5
LOG2E = 1.4426950408889634

VMEM_LIMIT_BYTES = 48 * 1024 * 1024
MOE_TILE = 256
RW_CHUNK = 64
RW_CHUNKS_PER_STEP = 4
LANES = 128
ATT_PAIRS = 8
ATT_ROW_PARTS = 1


def _cparams(*sem):
    return pltpu.CompilerParams(dimension_semantics=sem, vmem_limit_bytes=VMEM_LIMIT_BYTES)


def _layer_norm_rows(y, g, b):
    mu = jnp.mean(y, axis=-1, keepdims=True)
    yc = y - mu
    var = jnp.mean(yc * yc, axis=-1, keepdims=True)
    return yc * lax.rsqrt(var + LN_EPS) * g + b


def _split3(x):
    hi = x.astype(BF16)
    r1 = x - hi.astype(F32)
    mid = r1.astype(BF16)
    lo = (r1 - mid.astype(F32)).astype(BF16)
    return hi, mid, lo


def _dot(a, b):
    return jnp.dot(a, b, preferred_element_type=F32)


def _dot_nt(a, b):
    return lax.dot_general(a, b, (((1,), (1,)), ((), ())), preferred_element_type=F32)


def _dot_nt_split(w, x):
    n = w.shape[0]
    w_hi = w.astype(BF16)
    w_lo = (w - w_hi.astype(F32)).astype(BF16)
    x_hi = x.astype(BF16)
    x_lo = (x - x_hi.astype(F32)).astype(BF16)
    both = _dot_nt(jnp.concatenate([w_hi, w_lo], axis=0), x_hi)
    return both[:n] + both[n:] + _dot_nt(w_hi, x_lo)


def _linear_kernel(x_ref, w_ref, o_ref):
    o_ref[...] = _dot(x_ref[...], w_ref[...]).astype(o_ref.dtype)


def linear(x, w, out_dtype, tm=1024, tn=1024):
    M, K = x.shape
    N = w.shape[1]
    tn = min(tn, N)
    return pl.pallas_call(
        _linear_kernel,
        grid=(M // tm, N // tn),
        in_specs=[pl.BlockSpec((tm, K), lambda i, j: (i, 0)),
                  pl.BlockSpec((K, tn), lambda i, j: (0, j))],
        out_specs=pl.BlockSpec((tm, tn), lambda i, j: (i, j)),
        out_shape=jax.ShapeDtypeStruct((M, N), out_dtype),
        compiler_params=_cparams("parallel", "parallel"),
        name="linear",
    )(x, w)


def _proj_ln_kernel(a_ref, w_ref, h_ref, g_ref, b_ref, of_ref, ob_ref):
    tm = a_ref.shape[0]
    rp = tm // PROJ_ROW_PARTS
    parts = [slice(p * rp, (p + 1) * rp) for p in range(PROJ_ROW_PARTS)]
    mix = [None] * PROJ_ROW_PARTS

    def s_proj(c):
        mix[c] = _dot(a_ref[parts[c], :], w_ref[...])

    def s_norm(c):
        out = _layer_norm_rows(ALPHA * h_ref[parts[c], :] + mix[c], g_ref[...], b_ref[...])
        of_ref[parts[c], :] = out
        ob_ref[parts[c], :] = out.astype(BF16)

    _emit_skewed((s_proj, s_norm), PROJ_ROW_PARTS)


def proj_ln(a, w, h, g, b, tm=512):
    M, K = a.shape
    D = w.shape[1]
    return pl.pallas_call(
        _proj_ln_kernel,
        grid=(M // tm,),
        in_specs=[pl.BlockSpec((tm, K), lambda i: (i, 0)),
                  pl.BlockSpec((K, D), lambda i: (0, 0)),
                  pl.BlockSpec((tm, D), lambda i: (i, 0)),
                  pl.BlockSpec((1, D), lambda i: (0, 0)),
                  pl.BlockSpec((1, D), lambda i: (0, 0))],
        out_specs=[pl.BlockSpec((tm, D), lambda i: (i, 0)),
                   pl.BlockSpec((tm, D), lambda i: (i, 0))],
        out_shape=[jax.ShapeDtypeStruct((M, D), F32), jax.ShapeDtypeStruct((M, D), BF16)],
        compiler_params=_cparams("parallel"),
        name="proj_ln",
    )(a, w, h, g.reshape(1, D), b.reshape(1, D))


def _router_kernel(h_ref, rwt_ref, rb_ref, su_ref, idx_ref, gate_ref, cnt_ref, base_ref):
    step = pl.program_id(0)
    tr = h_ref.shape[0]

    @pl.when(step == 0)
    def _():
        base_ref[...] = jnp.zeros_like(base_ref)

    logits = _dot_nt_split(rwt_ref[...], h_ref[...])
    scores = jax.nn.sigmoid(logits)
    sel = scores + rb_ref[...]
    s = [sel[e:e + 1, :] for e in range(N_EXPERTS)]
    sc = [scores[e:e + 1, :] for e in range(N_EXPERTS)]

    def top2sum(v):
        best = v[0] + v[1]
        for a in range(4):
            for b in range(a + 1, 4):
                if (a, b) != (0, 1):
                    best = jnp.maximum(best, v[a] + v[b])
        return best

    gs = [top2sum(s[4 * g:4 * g + 4]) for g in range(N_GROUPS)]
    best, gi = gs[0], jnp.zeros((1, tr), I32)
    for g in range(1, N_GROUPS):
        better = gs[g] > best
        gi = jnp.where(better, g, gi)
        best = jnp.where(better, gs[g], best)

    def pick_group(rows):
        out = []
        for j in range(EXPERTS_PER_GROUP):
            v = rows[j]
            for g in range(1, N_GROUPS):
                v = jnp.where(gi == g, rows[4 * g + j], v)
            out.append(v)
        return out

    v = pick_group(s)
    raw = pick_group(sc)
    m1, l1, r1 = v[0], jnp.zeros((1, tr), I32), raw[0]
    for j in range(1, 4):
        better = v[j] > m1
        l1 = jnp.where(better, j, l1)
        m1 = jnp.where(better, v[j], m1)
        r1 = jnp.where(better, raw[j], r1)
    m2 = jnp.full((1, tr), -jnp.inf, F32)
    l2 = jnp.zeros((1, tr), I32)
    r2 = jnp.zeros((1, tr), F32)
    for j in range(4):
        better = jnp.logical_and(l1 != j, v[j] > m2)
        l2 = jnp.where(better, j, l2)
        m2 = jnp.where(better, v[j], m2)
        r2 = jnp.where(better, raw[j], r2)
    den = r1 + r2
    first_low = l1 < l2
    lo = jnp.where(first_low, l1, l2)
    hi = jnp.where(first_low, l2, l1)
    g_lo = jnp.where(first_low, r1, r2) / den
    g_hi = jnp.where(first_low, r2, r1) / den
    gate_ref[...] = jnp.concatenate([g_lo, g_hi], axis=0)
    bucket = gi * PAIRS_PER_GROUP + 2 * lo + hi - 1 - jnp.where(lo == 2, 1, 0)

    bidx = lax.broadcasted_iota(I32, (N_BUCKETS_PAD, tr), 0)
    hit = bidx == bucket
    oh = jnp.where(hit, 1.0, 0.0)
    rank = _dot(oh.astype(BF16), su_ref[...]) + base_ref[...]
    rk = jnp.sum(jnp.where(hit, rank, 0.0), axis=0, keepdims=True).astype(I32)
    idx_ref[...] = jnp.concatenate([bucket, rk], axis=0)
    base_ref[...] = base_ref[...] + jnp.sum(oh, axis=1, keepdims=True)
    cnt_ref[...] = jnp.broadcast_to(base_ref[...], cnt_ref.shape)


def moe_router(h, router_w, router_b, tr=512):
    T, D = h.shape
    su = jnp.triu(jnp.ones((tr, tr), BF16), k=1)
    idx, gate, cnt = pl.pallas_call(
        _router_kernel,
        grid=(T // tr,),
        in_specs=[pl.BlockSpec((tr, D), lambda i: (i, 0)),
                  pl.BlockSpec((N_EXPERTS, D), lambda i: (0, 0)),
                  pl.BlockSpec((N_EXPERTS, 1), lambda i: (0, 0)),
                  pl.BlockSpec((tr, tr), lambda i: (0, 0))],
        out_specs=[pl.BlockSpec((2, tr), lambda i: (0, i)),
                   pl.BlockSpec((2, tr), lambda i: (0, i)),
                   pl.BlockSpec((N_BUCKETS_PAD, LANES), lambda i: (0, 0))],
        out_shape=[jax.ShapeDtypeStruct((2, T), I32), jax.ShapeDtypeStruct((2, T), F32),
                   jax.ShapeDtypeStruct((N_BUCKETS_PAD, LANES), F32)],
        scratch_shapes=[pltpu.VMEM((N_BUCKETS_PAD, 1), F32)],
        compiler_params=_cparams("arbitrary"),
        name="moe_router",
    )(h, router_w.T, router_b.reshape(N_EXPERTS, 1), su)
    return idx, gate, cnt[:N_BUCKETS, 0].astype(I32)


def _dispatch_kernel(pos_ref, h_ref, gate_ref, z_ref, xs_ref, aug, sems):
    del z_ref
    i, n = pl.program_id(0), pl.num_programs(0)
    td, D = h_ref.shape
    slot = i % 2
    stage = aug.at[slot]
    stage[:, :D] = h_ref[...]
    stage[:, D:] = gate_ref[...]

    def row_copy(s, r, dst_row):
        return pltpu.make_async_copy(aug.at[s, pl.ds(r, 1)], xs_ref.at[pl.ds(dst_row, 1)], sems.at[s])

    def issue(g, c):
        for j in range(DMA_UNROLL):
            r = g * DMA_UNROLL + j
            row_copy(slot, r, pos_ref[r]).start(priority=j % 2)
        return c

    lax.fori_loop(0, td // DMA_UNROLL, issue, 0)

    def drain(s):
        pltpu.make_async_copy(aug.at[s], xs_ref.at[pl.ds(0, td)], sems.at[s]).wait()

    pl.when(i > 0)(lambda: drain(1 - slot))
    pl.when(i == n - 1)(lambda: drain(slot))


def moe_dispatch(h, pos, gcols, n_rows, td=1024):
    T, D = h.shape
    W = D + GATE_LANES
    zeros = jnp.zeros((n_rows, W), h.dtype)
    return pl.pallas_call(
        _dispatch_kernel,
        grid=(T // td,),
        in_specs=[pl.BlockSpec((td,), lambda i: (i,), memory_space=pltpu.SMEM),
                  pl.BlockSpec((td, D), lambda i: (i, 0)),
                  pl.BlockSpec((td, GATE_LANES), lambda i: (i, 0)),
                  pl.BlockSpec(memory_space=pl.ANY)],
        out_specs=pl.BlockSpec(memory_space=pl.ANY),
        out_shape=jax.ShapeDtypeStruct((n_rows, W), h.dtype),
        scratch_shapes=[pltpu.VMEM((2, td, W), h.dtype), pltpu.SemaphoreType.DMA((2,))],
        input_output_aliases={3: 0},
        compiler_params=_cparams("arbitrary"),
        name="moe_dispatch",
    )(pos, h, gcols, zeros)


def _expert_kernel(ta_ref, tb_ref, nu_ref, x_ref, wga, wua, wda, wgb, wub, wdb, o_ref):
    del ta_ref, tb_ref
    D = o_ref.shape[1]

    @pl.when(pl.program_id(0) < nu_ref[0])
    def _():
        xa = x_ref[...]
        x = xa[:, :D].astype(BF16)

        weights = ((wga, wua, wda), (wgb, wub, wdb))
        st = [dict(), dict()]

        def s_in(c):
            st[c]["g"] = _dot(x, weights[c][0][0, 0])
            st[c]["u"] = _dot(x, weights[c][1][0, 0])

        def s_act(c):
            st[c]["he"] = (jax.nn.silu(st[c].pop("g")) * st[c].pop("u")).astype(BF16)

        def s_out(c):
            st[c]["y"] = _dot(st[c].pop("he"), weights[c][2][0, 0])

        _emit_skewed((s_in, s_act, s_out), 2)
        o_ref[...] = xa[:, D:D + 1] * st[0]["y"] + xa[:, D + 1:D + 2] * st[1]["y"]

    @pl.when(pl.program_id(0) >= nu_ref[0])
    def _():
        o_ref[...] = jnp.zeros_like(o_ref)


def moe_experts(xs, tile_ea, tile_eb, n_used, w_gate, w_up, w_down, layer):
    n_rows, W = xs.shape
    _, E, D, De = w_gate.shape
    n_tiles = n_rows // MOE_TILE

    def row_map(i, ta, tb, nu):
        return (jnp.maximum(jnp.minimum(i, nu[0] - 1), 0), 0)

    wa = lambda i, ta, tb, nu: (layer, ta[i], 0, 0)
    wb = lambda i, ta, tb, nu: (layer, tb[i], 0, 0)
    grid_spec = pltpu.PrefetchScalarGridSpec(
        num_scalar_prefetch=3,
        grid=(n_tiles,),
        in_specs=[pl.BlockSpec((MOE_TILE, W), row_map),
                  pl.BlockSpec((1, 1, D, De), wa), pl.BlockSpec((1, 1, D, De), wa),
                  pl.BlockSpec((1, 1, De, D), wa),
                  pl.BlockSpec((1, 1, D, De), wb), pl.BlockSpec((1, 1, D, De), wb),
                  pl.BlockSpec((1, 1, De, D), wb)],
        out_specs=pl.BlockSpec((MOE_TILE, D), lambda i, ta, tb, nu: (i, 0)),
    )
    return pl.pallas_call(
        _expert_kernel,
        grid_spec=grid_spec,
        out_shape=jax.ShapeDtypeStruct((n_rows, D), F32),
        compiler_params=_cparams("arbitrary"),
        name="moe_experts",
    )(tile_ea, tile_eb, n_used, xs, w_gate, w_up, w_down, w_gate, w_up, w_down)


def _combine_kernel(pos_ref, posn_ref, h_ref, lg_ref, lb_ref, os_ref, of_ref, ob_ref, buf, sems):
    i, n = pl.program_id(0), pl.num_programs(0)
    tc = h_ref.shape[0]
    slot = i % 2

    def row_copy(s, src_row, r):
        return pltpu.make_async_copy(os_ref.at[pl.ds(src_row, 1)], buf.at[s, pl.ds(r, 1)], sems.at[s])

    def fetch(p_ref, s):
        def body(g, c):
            for j in range(DMA_UNROLL):
                r = g * DMA_UNROLL + j
                row_copy(s, p_ref[r], r).start(priority=j % 2)
            return c
        lax.fori_loop(0, tc // DMA_UNROLL, body, 0)

    pl.when(i == 0)(lambda: fetch(pos_ref, slot))
    pl.when(i + 1 < n)(lambda: fetch(posn_ref, 1 - slot))

    pltpu.make_async_copy(os_ref.at[pl.ds(0, tc)], buf.at[slot], sems.at[slot]).wait()
    out = _layer_norm_rows(ALPHA * h_ref[...] + buf[slot], lg_ref[...], lb_ref[...])
    of_ref[...] = out
    ob_ref[...] = out.astype(BF16)


def moe_combine(osorted, pos, h, ln_g, ln_b, tc=1024):
    T, D = h.shape
    nt = T // tc
    return pl.pallas_call(
        _combine_kernel,
        grid=(nt,),
        in_specs=[pl.BlockSpec((tc,), lambda i: (i,), memory_space=pltpu.SMEM),
                  pl.BlockSpec((tc,), lambda i: (jnp.minimum(i + 1, nt - 1),), memory_space=pltpu.SMEM),
                  pl.BlockSpec((tc, D), lambda i: (i, 0)),
                  pl.BlockSpec((1, D), lambda i: (0, 0)),
                  pl.BlockSpec((1, D), lambda i: (0, 0)),
                  pl.BlockSpec(memory_space=pl.ANY)],
        out_specs=[pl.BlockSpec((tc, D), lambda i: (i, 0)),
                   pl.BlockSpec((tc, D), lambda i: (i, 0))],
        out_shape=[jax.ShapeDtypeStruct((T, D), F32), jax.ShapeDtypeStruct((T, D), BF16)],
        scratch_shapes=[pltpu.VMEM((2, tc, D), F32), pltpu.SemaphoreType.DMA((2,))],
        compiler_params=_cparams("arbitrary"),
        name="moe_combine",
    )(pos, pos, h, ln_g.reshape(1, D), ln_b.reshape(1, D), osorted)


_PAIRS = ((0, 1), (0, 2), (0, 3), (1, 2), (1, 3), (2, 3))


def grouped_moe_ln(h, router_w, router_b, w_gate, w_up, w_down, ln_g, ln_b, layer):
    T, D = h.shape
    idx, gate, cnt = moe_router(h, router_w, router_b)
    bucket, rank = idx[0], idx[1]
    tiles = (cnt + MOE_TILE - 1) // MOE_TILE
    tile_end = jnp.cumsum(tiles)
    start = (tile_end - tiles) * MOE_TILE
    pos = start[bucket] + rank
    n_tiles = T // MOE_TILE + N_BUCKETS
    tile_ids = jnp.arange(n_tiles, dtype=I32)
    tile_bucket = jnp.minimum(
        jnp.sum((tile_ids[:, None] >= tile_end[None, :]).astype(I32), axis=1), N_BUCKETS - 1)
    ea = jnp.array([g * EXPERTS_PER_GROUP + a for g in range(N_GROUPS) for a, _ in _PAIRS], I32)
    eb = jnp.array([g * EXPERTS_PER_GROUP + b for g in range(N_GROUPS) for _, b in _PAIRS], I32)
    n_used = tile_end[-1:].astype(I32)
    gcols = jnp.pad(gate.T, ((0, 0), (0, GATE_LANES - 2)))
    xs = moe_dispatch(h, pos, gcols, n_tiles * MOE_TILE)
    osorted = moe_experts(xs, ea[tile_bucket], eb[tile_bucket], n_used, w_gate, w_up, w_down, layer)
    return moe_combine(osorted, pos, h, ln_g, ln_b)


def _forget_kernel(h_ref, wft_ref, bf_ref, ui_ref, c_ref, carry_ref):
    @pl.when(pl.program_id(1) == 0)
    def _():
        carry_ref[...] = jnp.zeros_like(carry_ref)

    logits = _dot_nt_split(wft_ref[...], h_ref[...])
    log_f = jax.nn.log_sigmoid(logits + bf_ref[...])
    hi, mid, lo = _split3(log_f)
    ui = ui_ref[...]
    cum = _dot(hi, ui) + _dot(mid, ui) + _dot(lo, ui) + carry_ref[...]
    c_ref[0] = cum * LOG2E
    carry_ref[...] = cum[:, -1:]


def fox_forget_cumsum(h, w_f, b_f, B, S, ts=512):
    T, D = h.shape
    H = w_f.shape[1]
    ui = jnp.triu(jnp.ones((ts, ts), BF16))
    ns = S // ts
    return pl.pallas_call(
        _forget_kernel,
        grid=(B, ns),
        in_specs=[pl.BlockSpec((ts, D), lambda b, i: (b * ns + i, 0)),
                  pl.BlockSpec((H, D), lambda b, i: (0, 0)),
                  pl.BlockSpec((H, 1), lambda b, i: (0, 0)),
                  pl.BlockSpec((ts, ts), lambda b, i: (0, 0))],
        out_specs=pl.BlockSpec((1, H, ts), lambda b, i: (b, 0, i)),
        out_shape=jax.ShapeDtypeStruct((B, H, S), F32),
        scratch_shapes=[pltpu.VMEM((H, 1), F32)],
        compiler_params=_cparams("parallel", "arbitrary"),
        name="fox_forget",
    )(h, w_f.T, b_f.reshape(H, 1), ui)


def _head_pair_masks(q):
    lane = lax.broadcasted_iota(I32, q.shape, 1)
    zero = jnp.zeros_like(q)
    return jnp.where(lane < HEAD_DIM, q, zero), jnp.where(lane >= HEAD_DIM, q, zero)


def _lane_tile(x, n):
    return x if n == 1 else jnp.concatenate([x] * n, axis=1)


def _causal_steps(nq, ratio, descending):
    qi, kb = [], []
    for i in range(nq):
        ks = list(range(ratio * (i + 1)))
        for k in (reversed(ks) if descending else ks):
            qi.append(i)
            kb.append(k)
    return jnp.array(qi, I32), jnp.array(kb, I32)


def _emit_skewed(stages, n_chains):
    for t in range(n_chains + len(stages) - 1):
        for s in reversed(range(len(stages))):
            if 0 <= t - s < n_chains:
                stages[s](t - s)


def _pair(p):
    return slice(p * LANES, (p + 1) * LANES)


def _fox_kernel(qi_ref, kb_ref, q_ref, k_ref, v_ref, crow_ref, ccol_ref, og_ref, o_ref, qm, ct, m, acc):
    hg, step_id = pl.program_id(1), pl.program_id(2)
    i, j = qi_ref[step_id], kb_ref[step_id]
    tq, tk = q_ref.shape[1], k_ref.shape[1]
    n_heads = qm.shape[0]

    @pl.when(j == 0)
    def _():
        q = q_ref[0]
        cc = ccol_ref[0]
        head = lax.broadcasted_iota(I32, cc.shape, 1)
        for hh in range(n_heads):
            qm[hh] = _head_pair_masks(q[:, _pair(hh // 2)])[hh % 2]
            col = jnp.sum(jnp.where(head == n_heads * hg + hh, cc, 0.0), axis=1, keepdims=True)
            ct[hh] = jnp.broadcast_to(col, (tq, LANES))
        m[...] = jnp.full(m.shape, -jnp.inf, F32)
        acc[...] = jnp.zeros_like(acc)

    def step(masked):
        ks = [k_ref[0, :, _pair(p)] for p in range(n_heads // 2)]
        ones = jnp.ones((tk, LANES), BF16)
        v_ones = [jnp.concatenate([v_ref[0, :, _pair(p)], ones], axis=1) for p in range(n_heads // 2)]
        crow = crow_ref[0]
        if masked:
            row = lax.broadcasted_iota(I32, (tq, tk), 0)
            col = lax.broadcasted_iota(I32, (tq, tk), 1)
            keep = col <= row
        rp = tq // ATT_ROW_PARTS
        chains = [(hh, slice(p * rp, (p + 1) * rp)) for hh in range(n_heads) for p in range(ATT_ROW_PARTS)]
        st = [dict() for _ in chains]

        def s_scores(c):
            hh, rows = chains[c]
            st[c]["s"] = _dot_nt(qm[hh, rows, :], ks[hh // 2])

        def s_softmax(c):
            hh, rows = chains[c]
            s = st[c].pop("s") - crow[hh // 2, hh % 2:hh % 2 + 1, :]
            if masked:
                s = jnp.where(keep[rows, :], s, -jnp.inf)
            c_t = ct[hh, rows, :]
            m_prev = m[hh, rows, :]
            m_new = jnp.maximum(m_prev, jnp.max(s, axis=1, keepdims=True) + c_t)
            st[c]["alpha"] = jnp.exp2(m_prev - m_new)
            st[c]["p"] = jnp.exp2(s - _lane_tile(m_new - c_t, tk // LANES)).astype(BF16)
            m[hh, rows, :] = m_new

        def s_values(c):
            hh, rows = chains[c]
            acc[hh, rows, :] = (_lane_tile(st[c].pop("alpha"), 2) * acc[hh, rows, :]
                                + _dot(st[c].pop("p"), v_ones[hh // 2]))

        _emit_skewed((s_scores, s_softmax, s_values), len(chains))

    pl.when(j < i)(lambda: step(False))
    pl.when(j == i)(lambda: step(True))

    @pl.when(j == i)
    def _():
        lane = lax.broadcasted_iota(I32, (tq, LANES), 1)
        for p in range(n_heads // 2):
            a0, a1 = acc[2 * p], acc[2 * p + 1]
            o = jnp.where(lane < HEAD_DIM, a0[:, :LANES] / a0[:, LANES:], a1[:, :LANES] / a1[:, LANES:])
            o_ref[0, :, _pair(p)] = (o * jax.nn.sigmoid(og_ref[0, :, _pair(p)].astype(F32))).astype(BF16)


def fox_attention(qkv, og, c, B, S, tq=512):
    T = qkv.shape[0]
    D = N_HEADS * HEAD_DIM
    hp_n = N_HEADS // 2
    nq = S // tq
    qkv3 = qkv.reshape(B, S, 3 * D)
    crow = c.reshape(B, hp_n, 2, S)
    ccol = jnp.transpose(c, (0, 2, 1))
    qi, kb = _causal_steps(nq, 1, descending=False)
    w = ATT_PAIRS * LANES
    hg_n = hp_n // ATT_PAIRS
    nh = 2 * ATT_PAIRS
    q_blk = lambda b, hg, s, qi, kb: (b, qi[s], hg)
    kv_blk = lambda off: (lambda b, hg, s, qi, kb: (b, kb[s], off + hg))
    grid_spec = pltpu.PrefetchScalarGridSpec(
        num_scalar_prefetch=2,
        grid=(B, hg_n, qi.shape[0]),
        in_specs=[pl.BlockSpec((1, tq, w), q_blk),
                  pl.BlockSpec((1, tq, w), kv_blk(hg_n)),
                  pl.BlockSpec((1, tq, w), kv_blk(2 * hg_n)),
                  pl.BlockSpec((1, ATT_PAIRS, 2, tq), lambda b, hg, s, qi, kb: (b, hg, 0, kb[s])),
                  pl.BlockSpec((1, tq, N_HEADS), lambda b, hg, s, qi, kb: (b, qi[s], 0)),
                  pl.BlockSpec((1, tq, w), q_blk)],
        out_specs=pl.BlockSpec((1, tq, w), q_blk),
        scratch_shapes=[pltpu.VMEM((nh, tq, LANES), BF16),
                        pltpu.VMEM((nh, tq, LANES), F32), pltpu.VMEM((nh, tq, LANES), F32),
                        pltpu.VMEM((nh, tq, 2 * LANES), F32)],
    )
    out = pl.pallas_call(
        _fox_kernel,
        grid_spec=grid_spec,
        out_shape=jax.ShapeDtypeStruct((B, S, D), BF16),
        compiler_params=_cparams("parallel", "parallel", "arbitrary"),
        name="fox_attention",
    )(qi, kb, qkv3, qkv3, qkv3, crow, ccol, og.reshape(B, S, D))
    return out.reshape(T, D)


def _sb_kernel(qi_ref, kb_ref, q_ref, k_ref, v_ref, u_ref, o_ref, qm, rs, acc):
    step_id = pl.program_id(2)
    i, kb = qi_ref[step_id], kb_ref[step_id]
    tq, tk = q_ref.shape[1], k_ref.shape[1]
    ratio = tq // tk

    n_heads = qm.shape[0]

    @pl.when(kb == ratio * (i + 1) - 1)
    def _():
        q = q_ref[0]
        for hh in range(n_heads):
            qm[hh] = _head_pair_masks(q[:, _pair(hh // 2)])[hh % 2]
        rs[...] = jnp.zeros_like(rs)
        acc[...] = jnp.zeros_like(acc)

    def step(masked):
        ks = [k_ref[0, :, _pair(p)] for p in range(n_heads // 2)]
        vs = [v_ref[0, :, _pair(p)] for p in range(n_heads // 2)]
        u = u_ref[...]
        if masked:
            row = lax.broadcasted_iota(I32, (tq, tk), 0)
            col = lax.broadcasted_iota(I32, (tq, tk), 1)
            keep = col - row < i * tq - kb * tk
        rp = tq // ATT_ROW_PARTS
        chains = [(hh, slice(p * rp, (p + 1) * rp)) for hh in range(n_heads) for p in range(ATT_ROW_PARTS)]
        st = [dict() for _ in chains]

        def s_scores(c):
            x, rows = chains[c]
            st[c]["z"] = _dot_nt(qm[x, rows, :], ks[x // 2])

        def s_logs(c):
            x, rows = chains[c]
            z = st[c]["z"]
            zb = z.astype(BF16)
            e = jnp.exp2((jnp.abs(z) * (-LOG2E)).astype(BF16))
            nl = jnp.maximum(zb, 0) + jnp.log(jnp.ones((), BF16) + e)
            if masked:
                nl = jnp.where(keep[rows, :], nl, jnp.zeros((), BF16))
            st[c]["nl"] = nl

        def s_suffix(c):
            st[c]["ci"] = _dot(st[c].pop("nl"), u)

        def s_weights(c):
            x, rows = chains[c]
            prev = rs[x, rows, :]
            ci = st[c].pop("ci")
            a = jnp.exp2((st[c].pop("z") - ci - _lane_tile(prev, tk // LANES)) * LOG2E)
            if masked:
                a = jnp.where(keep[rows, :], a, 0.0)
            st[c]["a"] = a.astype(BF16)
            rs[x, rows, :] = prev + ci[:, 0:1]

        def s_values(c):
            x, rows = chains[c]
            acc[x, rows, :] = acc[x, rows, :] + _dot(st[c].pop("a"), vs[x // 2])

        _emit_skewed((s_scores, s_logs, s_suffix, s_weights, s_values), len(chains))

    pl.when(kb >= ratio * i)(lambda: step(True))
    pl.when(kb < ratio * i)(lambda: step(False))

    @pl.when(kb == 0)
    def _():
        lane = lax.broadcasted_iota(I32, (tq, LANES), 1)
        for p in range(n_heads // 2):
            o_ref[0, :, _pair(p)] = jnp.where(lane < HEAD_DIM, acc[2 * p], acc[2 * p + 1]).astype(BF16)


def stick_breaking_attention(qkv, B, S, tq=512, tk=256):
    T = qkv.shape[0]
    D = N_HEADS * HEAD_DIM
    hp_n = N_HEADS // 2
    nq = S // tq
    qkv3 = qkv.reshape(B, S, 3 * D)
    u = jnp.tril(jnp.ones((tk, tk), BF16))
    qi, kb = _causal_steps(nq, tq // tk, descending=True)
    w = ATT_PAIRS * LANES
    hg_n = hp_n // ATT_PAIRS
    nh = 2 * ATT_PAIRS
    q_blk = lambda b, hg, s, qi, kb: (b, qi[s], hg)
    kv_blk = lambda off: (lambda b, hg, s, qi, kb: (b, kb[s], off + hg))
    grid_spec = pltpu.PrefetchScalarGridSpec(
        num_scalar_prefetch=2,
        grid=(B, hg_n, qi.shape[0]),
        in_specs=[pl.BlockSpec((1, tq, w), q_blk),
                  pl.BlockSpec((1, tk, w), kv_blk(hg_n)),
                  pl.BlockSpec((1, tk, w), kv_blk(2 * hg_n)),
                  pl.BlockSpec((tk, tk), lambda b, hg, s, qi, kb: (0, 0))],
        out_specs=pl.BlockSpec((1, tq, w), q_blk),
        scratch_shapes=[pltpu.VMEM((nh, tq, LANES), BF16),
                        pltpu.VMEM((nh, tq, LANES), F32), pltpu.VMEM((nh, tq, LANES), F32)],
    )
    out = pl.pallas_call(
        _sb_kernel,
        grid_spec=grid_spec,
        out_shape=jax.ShapeDtypeStruct((B, S, D), BF16),
        compiler_params=_cparams("parallel", "parallel", "arbitrary"),
        name="stick_breaking",
    )(qi, kb, qkv3, qkv3, qkv3, u)
    return out.reshape(T, D)


def _gmlp_kernel(h_ref, win_ref, bin_ref, lg_ref, lb_ref, ws_ref, bst_ref, o_ref):
    tm = h_ref.shape[0]
    dg = o_ref.shape[1]
    row = lax.broadcasted_iota(I32, (CHUNK, CHUNK), 0)
    col = lax.broadcasted_iota(I32, (CHUNK, CHUNK), 1)
    gw = dg // GMLP_GROUPS
    wcs = [jnp.where(col <= row, ws_ref[g], 0.0).astype(BF16) for g in range(GMLP_GROUPS)]
    rp = tm // GMLP_ROW_PARTS
    parts = [slice(p * rp, (p + 1) * rp) for p in range(GMLP_ROW_PARTS)]
    zs = [None] * GMLP_ROW_PARTS

    def s_proj(p):
        zs[p] = _dot(h_ref[parts[p], :], win_ref[...])

    def s_gate(p):
        z = jax.nn.gelu(zs[p] + bin_ref[...])
        u = z[:, :dg]
        v = _layer_norm_rows(z[:, dg:], lg_ref[...], lb_ref[...]).astype(BF16)
        for g in range(GMLP_GROUPS):
            cs = slice(g * gw, (g + 1) * gw)
            for c in range(rp // CHUNK):
                rs = slice(c * CHUNK, (c + 1) * CHUNK)
                sv = _dot(wcs[g], v[rs, cs]) + bst_ref[:, g:g + 1]
                o_ref[pl.ds(p * rp + c * CHUNK, CHUNK), cs] = (u[rs, cs] * sv).astype(BF16)

    _emit_skewed((s_proj, s_gate), GMLP_ROW_PARTS)


def gmlp_gate(hb, w_in, b_in, ln_g, ln_b, w_s, b_s, tm=512):
    T, D = hb.shape
    dg = w_in.shape[1] // 2
    return pl.pallas_call(
        _gmlp_kernel,
        grid=(T // tm,),
        in_specs=[pl.BlockSpec((tm, D), lambda i: (i, 0)),
                  pl.BlockSpec((D, 2 * dg), lambda i: (0, 0)),
                  pl.BlockSpec((1, 2 * dg), lambda i: (0, 0)),
                  pl.BlockSpec((1, dg), lambda i: (0, 0)),
                  pl.BlockSpec((1, dg), lambda i: (0, 0)),
                  pl.BlockSpec((GMLP_GROUPS, CHUNK, CHUNK), lambda i: (0, 0, 0)),
                  pl.BlockSpec((CHUNK, GMLP_GROUPS), lambda i: (0, 0))],
        out_specs=pl.BlockSpec((tm, dg), lambda i: (i, 0)),
        out_shape=jax.ShapeDtypeStruct((T, dg), BF16),
        compiler_params=_cparams("parallel"),
        name="gmlp_gate",
    )(hb, w_in, b_in.reshape(1, -1), ln_g.reshape(1, -1), ln_b.reshape(1, -1), w_s, b_s.T)


RW_LANES = 4 * HEAD_DIM


def _seg_sum(x, bones):
    hi = x.astype(BF16)
    lo = (x - hi.astype(F32)).astype(BF16)
    return _dot(hi, bones) + _dot(lo, bones)


def _rwkv_prep_kernel(seq_len, h_ref, hp_ref, mu_ref, wrkv_ref, w1_ref, w2_ref, a1_ref, a2_ref,
                      g1_ref, g2_ref, w0_ref, a0_ref, kk_ref, ka_ref, bones_ref, lchunk_ref,
                      r_ref, cl_ref, clx_ref, k_ref, v_ref, kkn_ref, b_ref, g_ref):
    i = pl.program_id(0)
    tm, D = h_ref.shape
    h = h_ref[...]
    at_start = (i * tm) % seq_len == 0
    prev_row = jnp.where(at_start, 0.0, hp_ref[7:8, :])
    row = lax.broadcasted_iota(I32, (tm, D), 0)
    h_prev = jnp.where(row == 0, prev_row, pltpu.roll(h, 1, axis=0))
    dx = h_prev - h

    def mixed(c):
        return (h + dx * mu_ref[c:c + 1, :]).astype(BF16)

    r = _dot(mixed(0), wrkv_ref[0])
    k = _dot(mixed(1), wrkv_ref[1])
    v = _dot(mixed(2), wrkv_ref[2])
    d = w0_ref[...] + _dot(jnp.tanh(_dot(mixed(3), w1_ref[...])).astype(BF16), w2_ref[...])
    lw = -jnp.exp(-jax.nn.softplus(-d) - 0.5)
    a = jax.nn.sigmoid(a0_ref[...] + _dot(_dot(mixed(4), a1_ref[...]).astype(BF16), a2_ref[...]))
    g = _dot(jax.nn.sigmoid(_dot(mixed(5), g1_ref[...])).astype(BF16), g2_ref[...])
    kk = k * kk_ref[...]
    bones = bones_ref[...]
    norm_sq = jnp.concatenate(
        [_seg_sum(jnp.square(kk[:, q * RW_LANES:(q + 1) * RW_LANES]), bones) for q in range(D // RW_LANES)],
        axis=1)
    kkn = kk / jnp.maximum(jnp.sqrt(norm_sq), 1e-12)
    hi, mid, lo = _split3(lw)
    lchunk = lchunk_ref[...]
    cl = _dot(lchunk, hi) + _dot(lchunk, mid) + _dot(lchunk, lo)
    r_ref[...] = r.astype(BF16)
    cl_ref[...] = cl
    clx_ref[...] = cl - lw
    k_ref[...] = (k * (1.0 + (a - 1.0) * ka_ref[...])).astype(BF16)
    v_ref[...] = v.astype(BF16)
    kkn_ref[...] = kkn.astype(BF16)
    b_ref[...] = (kkn * a).astype(BF16)
    g_ref[...] = g.astype(BF16)


def _block_ones():
    idx = jnp.arange(RW_LANES) // HEAD_DIM
    return (idx[:, None] == idx[None, :]).astype(BF16)


def rwkv_prep(h, mu, w_rkv, w0, w1, w2, a0, a1, a2, g1, g2, k_k, k_a, S, tm=256):
    T, D = h.shape
    row = lambda i: (i, 0)
    full2 = lambda i: (0, 0)
    vec = lambda x: x.reshape(1, D)
    bf = lambda x: x.astype(BF16)
    pos = jnp.arange(tm)
    lchunk = jnp.logical_and(pos[:, None] // RW_CHUNK == pos[None, :] // RW_CHUNK,
                             pos[None, :] <= pos[:, None]).astype(BF16)
    outs = pl.pallas_call(
        functools.partial(_rwkv_prep_kernel, S),
        grid=(T // tm,),
        in_specs=[pl.BlockSpec((tm, D), row),
                  pl.BlockSpec((8, D), lambda i: (jnp.maximum(i * (tm // 8) - 1, 0), 0)),
                  pl.BlockSpec((6, D), full2),
                  pl.BlockSpec((3, D, D), lambda i: (0, 0, 0)),
                  pl.BlockSpec(w1.shape, full2), pl.BlockSpec(w2.shape, full2),
                  pl.BlockSpec(a1.shape, full2), pl.BlockSpec(a2.shape, full2),
                  pl.BlockSpec(g1.shape, full2), pl.BlockSpec(g2.shape, full2),
                  pl.BlockSpec((1, D), full2), pl.BlockSpec((1, D), full2),
                  pl.BlockSpec((1, D), full2), pl.BlockSpec((1, D), full2),
                  pl.BlockSpec((RW_LANES, RW_LANES), full2),
                  pl.BlockSpec((tm, tm), full2)],
        out_specs=[pl.BlockSpec((tm, D), row)] * 8,
        out_shape=[jax.ShapeDtypeStruct((T, D), dt) for dt in (BF16, F32, F32, BF16, BF16, BF16, BF16, BF16)],
        compiler_params=_cparams("parallel"),
        name="rwkv_prep",
    )(h, h, mu, bf(w_rkv), bf(w1), bf(w2), bf(a1), bf(a2), bf(g1), bf(g2),
      vec(w0), vec(a0), vec(k_k), vec(k_a), _block_ones(), lchunk)
    return outs


def _rwkv_local(r, cl, clx, k, v, kk, b, cst):
    C = RW_CHUNK
    mbd, eye, strict, incl, levels = cst
    eye_f = jnp.where(eye, 1.0, 0.0)
    eye_b = eye_f.astype(BF16)

    def each(f, *lists):
        return [f(*xs) for xs in zip(*lists)]

    def bd(y):
        return jnp.where(mbd, jnp.concatenate([y.astype(BF16)] * 4, axis=0), jnp.zeros((), BF16))

    def rows(*parts):
        return jnp.concatenate([p.astype(BF16) for p in parts], axis=0)

    cl_last = each(lambda c: c[C - 1:C, :], cl)
    kkt = each(lambda a, c: a * jnp.exp(c), kk, clx)
    rt = each(lambda a, c: a * jnp.exp(c), r, cl)
    g_inv = each(lambda c: jnp.exp(-c), cl)
    g_tail = each(lambda cl_, c: jnp.exp(cl_ - c), cl_last, cl)
    kk_r = each(rows, kkt, rt)
    ab_k = each(lambda l, a, g: _dot_nt(l, bd(a * g)), kk_r, k, g_inv)
    ab_b = each(lambda l, a, g: _dot_nt(l, bd(a * g)), kk_r, b, g_inv)
    a_k = each(lambda x: jnp.where(strict, x[:C], 0.0), ab_k)
    b_k = each(lambda x: jnp.where(incl, x[C:], 0.0), ab_k)
    a_b = each(lambda x: jnp.where(strict, x[:C], 0.0), ab_b)
    b_b = each(lambda x: jnp.where(incl, x[C:], 0.0), ab_b)
    kh_t = each(lambda a, g: _dot_nt(eye_b, bd(a * g)), k, g_tail)
    bh_t = each(lambda a, g: _dot_nt(eye_b, bd(a * g)), b, g_tail)

    dm = each(lambda a: eye_f - jnp.where(levels[0], a, 0.0), a_b)
    for lvl in levels[1:]:
        w = each(lambda d, a: _dot(d.astype(BF16), bd(jnp.where(lvl, a, 0.0))), dm, a_b)
        dm = each(lambda d, w_: d - _dot(w_.astype(BF16), bd(d)), dm, w)
    tm_b = each(lambda d: d.astype(BF16), dm)

    xv = each(lambda a, b_, c, v_: _dot(rows(a, b_, c), bd(v_)), a_k, b_k, kh_t, v)
    kp = each(lambda t, a: _dot(t, bd(a)), tm_b, kkt)
    vp = each(lambda t, x: _dot(t, bd(x[:C])), tm_b, xv)
    bb_bh = each(rows, b_b, bh_t)
    xkp = each(lambda l, a: _dot(l, bd(a)), bb_bh, kp)
    xvp = each(lambda l, a: _dot(l, bd(a)), bb_bh, vp)
    y_loc = each(lambda x, z: x[C:2 * C] - z[:C], xv, xvp)
    r_eff = each(lambda a, z: a - z[:C], rt, xkp)
    m = each(lambda c, z: eye_f * jnp.exp(c) - z[C:], cl_last, xkp)
    n = each(lambda x, z: x[2 * C:] - z[C:], xv, xvp)
    return r_eff, y_loc, m, n


def _rwkv_apply(r_eff, y_loc, m, n, P, mbd):
    C = RW_CHUNK

    def each(f, *lists):
        return [f(*xs) for xs in zip(*lists)]

    def bd(y):
        return jnp.where(mbd, jnp.concatenate([y.astype(BF16)] * 4, axis=0), jnp.zeros((), BF16))

    def rows(*parts):
        return jnp.concatenate([p.astype(BF16) for p in parts], axis=0)

    p_hi = each(lambda p: p.astype(BF16), P)
    m_hi = each(lambda a: a.astype(BF16), m)
    z_hi = each(lambda a, mh, m_, ph: _dot(rows(a, mh, m_ - mh.astype(F32)), bd(ph)), r_eff, m_hi, m, p_hi)
    z_lo = each(lambda a, mh, p, ph: _dot(rows(a, mh), bd(p - ph.astype(F32))), r_eff, m_hi, P, p_hi)
    y = each(lambda zh, zl, yl: zh[:C] + zl[:C] + yl, z_hi, z_lo, y_loc)
    p_new = each(lambda zh, zl, n_: zh[C:2 * C] + zh[2 * C:] + zl[C:] + n_, z_hi, z_lo, n)
    return y, p_new


def _rwkv_scan_kernel(r_ref, cl_ref, clx_ref, k_ref, v_ref, kk_ref, b_ref, g_ref, rk_ref, gg_ref, gb_ref,
                      bones_ref, o_ref, state):
    C = RW_CHUNK

    @pl.when(pl.program_id(1) == 0)
    def _():
        state[...] = jnp.zeros_like(state)

    t = lax.broadcasted_iota(I32, (C, RW_LANES), 0)
    i = lax.broadcasted_iota(I32, (C, RW_LANES), 1) % C
    levels = []
    for sh in range(6):
        tb, ib = t >> sh, i >> sh
        levels.append(jnp.logical_and(tb == ib + 1, (tb & 1) == 1))
    rr = lax.broadcasted_iota(I32, (RW_LANES, RW_LANES), 0) // HEAD_DIM
    cc = lax.broadcasted_iota(I32, (RW_LANES, RW_LANES), 1) // HEAD_DIM
    cst = (rr == cc, i == t, i < t, i <= t, levels)
    bones = bones_ref[...]

    n_groups = r_ref.shape[1] // RW_LANES
    n_chunks = r_ref.shape[0] // C
    tiles = [(slice(c * C, (c + 1) * C), slice(q * RW_LANES, (q + 1) * RW_LANES))
             for c in range(n_chunks) for q in range(n_groups)]
    cut = lambda ref: [ref[rs, ls] for rs, ls in tiles]
    r, k, v = cut(r_ref), cut(k_ref), cut(v_ref)
    r_eff, y_loc, m, n = _rwkv_local(r, cut(cl_ref), cut(clx_ref), k, v, cut(kk_ref), cut(b_ref), cst)
    p = [state[q] for q in range(n_groups)]
    ys = []
    for c in range(n_chunks):
        part = slice(c * n_groups, (c + 1) * n_groups)
        y_c, p = _rwkv_apply(r_eff[part], y_loc[part], m[part], n[part], p, cst[0])
        ys += y_c
    for q in range(n_groups):
        state[q] = p[q]
    stack = lambda xs: jnp.concatenate(xs, axis=0)
    y = stack(ys)
    mean = _seg_sum(y, bones) * (1.0 / HEAD_DIM)
    yc = y - mean
    var = _seg_sum(yc * yc, bones) * (1.0 / HEAD_DIM)
    yn = yc * lax.rsqrt(var + GN_EPS)
    rkr = stack([r[idx].astype(F32) * k[idx].astype(F32) * rk_ref[:, ls]
                 for idx, (rs, ls) in enumerate(tiles)])
    dots = _seg_sum(rkr, bones)
    for idx, (rs, ls) in enumerate(tiles):
        part = slice(idx * C, (idx + 1) * C)
        out = yn[part] * gg_ref[:, ls] + gb_ref[:, ls] + dots[part] * v[idx]
        o_ref[rs, ls] = (out * g_ref[rs, ls]).astype(BF16)


def rwkv_scan(r, cl, clx, k, v, kkn, b, g, r_k, gn_g, gn_b, B, S):
    T, D = r.shape
    rows_per_step = RW_CHUNK * RW_CHUNKS_PER_STEP
    nc = S // rows_per_step
    row = lambda bi, ci: (bi * nc + ci, 0)
    full2 = lambda bi, ci: (0, 0)
    return pl.pallas_call(
        _rwkv_scan_kernel,
        grid=(B, nc),
        in_specs=[pl.BlockSpec((rows_per_step, D), row)] * 8 + [pl.BlockSpec((1, D), full2)] * 3
        + [pl.BlockSpec((RW_LANES, RW_LANES), full2)],
        out_specs=pl.BlockSpec((rows_per_step, D), row),
        out_shape=jax.ShapeDtypeStruct((T, D), BF16),
        scratch_shapes=[pltpu.VMEM((D // RW_LANES, HEAD_DIM, RW_LANES), F32)],
        compiler_params=_cparams("parallel", "arbitrary"),
        name="rwkv_scan",
    )(r, cl, clx, k, v, kkn, b, g, r_k.reshape(1, D), gn_g.reshape(1, D), gn_b.reshape(1, D),
      _block_ones())


def rwkv_layer(h, mu, w_rkv, w0, w1, w2, a0, a1, a2, g1, g2, k_k, k_a, r_k, gn_g, gn_b, w_out,
               ln_g, ln_b, B, S):
    r, cl, clx, k, v, kkn, b, g = rwkv_prep(h, mu, w_rkv, w0, w1, w2, a0, a1, a2, g1, g2, k_k, k_a, S)
    y = rwkv_scan(r, cl, clx, k, v, kkn, b, g, r_k, gn_g, gn_b, B, S)
    return proj_ln(y, w_out.astype(BF16), h, ln_g, ln_b)


def fox_layer(h, hb, w_in, b_f, w_out, ln_g, ln_b, B, S):
    D = h.shape[1]
    scale = HEAD_DIM ** -0.5 * LOG2E
    w_qkv = jnp.concatenate([w_in[:, :D] * scale, w_in[:, D:3 * D]], axis=1).astype(BF16)
    w_f = w_in[:, 3 * D:3 * D + N_HEADS]
    w_og = w_in[:, 3 * D + N_HEADS:].astype(BF16)
    qkv = linear(hb, w_qkv, BF16)
    og = linear(hb, w_og, BF16)
    c = fox_forget_cumsum(h, w_f, b_f, B, S)
    o = fox_attention(qkv, og, c, B, S)
    return proj_ln(o, w_out.astype(BF16), h, ln_g, ln_b)


def sb_layer(h, hb, w_in, w_out, ln_g, ln_b, B, S):
    D = h.shape[1]
    scale = HEAD_DIM ** -0.5
    w_qkv = jnp.concatenate([w_in[:, :D] * scale, w_in[:, D:]], axis=1).astype(BF16)
    qkv = linear(hb, w_qkv, BF16)
    o = stick_breaking_attention(qkv, B, S)
    return proj_ln(o, w_out.astype(BF16), h, ln_g, ln_b)


def gmlp_layer(h, hb, w_in, b_in, gm_ln_g, gm_ln_b, w_s, b_s, w_out, ln_g, ln_b):
    y = gmlp_gate(hb, w_in.astype(BF16), b_in, gm_ln_g, gm_ln_b, w_s, b_s)
    return proj_ln(y, w_out.astype(BF16), h, ln_g, ln_b)


def kernel(x, ln1_g, ln1_b, ln2_g, ln2_b, fox_w_in, fox_b_f, fox_w_out, gm_w_in, gm_b_in, gm_ln_g,
           gm_ln_b, gm_w_s, gm_b_s, gm_w_out, sb_w_in, sb_w_out, rw_mu, rw_w_rkv, rw_w0, rw_w1, rw_w2,
           rw_a0, rw_a1, rw_a2, rw_g1, rw_g2, rw_k_k, rw_k_a, rw_r_k, rw_gn_g, rw_gn_b, rw_w_out,
           router_w, router_b, moe_w_gate, moe_w_up, moe_w_down):
    B, S, D = x.shape
    h = x.reshape(B * S, D)
    hb = h.astype(BF16)
    w_gate_b, w_up_b, w_down_b = (w.astype(BF16) for w in (moe_w_gate, moe_w_up, moe_w_down))
    for i in range(DEPTH):
        kind, j = i % 4, i // 4
        if kind == 0:
            h, hb = fox_layer(h, hb, fox_w_in[j], fox_b_f[j], fox_w_out[j], ln1_g[i], ln1_b[i], B, S)
        elif kind == 1:
            h, hb = gmlp_layer(h, hb, gm_w_in[j], gm_b_in[j], gm_ln_g[j], gm_ln_b[j], gm_w_s[j],
                               gm_b_s[j], gm_w_out[j], ln1_g[i], ln1_b[i])
        elif kind == 2:
            h, hb = sb_layer(h, hb, sb_w_in[j], sb_w_out[j], ln1_g[i], ln1_b[i], B, S)
        else:
            h, hb = rwkv_layer(h, rw_mu[j], rw_w_rkv[j], rw_w0[j], rw_w1[j], rw_w2[j], rw_a0[j],
                               rw_a1[j], rw_a2[j], rw_g1[j], rw_g2[j], rw_k_k[j], rw_k_a[j], rw_r_k[j],
                               rw_gn_g[j], rw_gn_b[j], rw_w_out[j], ln1_g[i], ln1_b[i], B, S)
        h, hb = grouped_moe_ln(h, router_w, router_b, w_gate_b, w_up_b, w_down_b, ln2_g[i], ln2_b[i], i)
    return h.reshape(B, S, D)
```
